```python
import math
import jax, jax.numpy as jnp
from jax import lax
import numpy as np

D_MODEL = 2048
BATCH = 4
SEQ = 2048
DEPTH = 4
DEC_BATCH = 128
DEC_SEQ = 4
PAST_LEN = 16384
PAGE_SIZE = 128

MIX_W = D_MODEL // 2
N_BRANCH = 3
DN_HEAD_DIM = 128
DN_HEADS = MIX_W // DN_HEAD_DIM
DN_CHUNK = 64
CONV_W = 4
GM_CHUNK = 128
GM_GROUPS = MIX_W // 128
GM_GROUP_W = MIX_W // GM_GROUPS
LRU_BLOCKS = 8
LRU_BLOCK_W = MIX_W // LRU_BLOCKS
LRU_C = 8.0
D_FF = 11 * D_MODEL // 4
N_EXPERTS = 8
TOP_K = 2
N_MOD = 6
IN_SIZES = (3 * MIX_W, MIX_W, DN_HEADS, DN_HEADS, 2 * MIX_W, 2 * MIX_W, N_BRANCH * D_MODEL)
N_IN = sum(IN_SIZES)
RMS_EPS = 1e-6
LN_EPS = 1e-5

kernel_name = 'hybrid_deltanet_gmlp_rglru_moe_decode_step'


def rms_norm(x, g):
    xf = x.astype(jnp.float32)
    y = xf * lax.rsqrt(jnp.mean(xf * xf, axis=-1, keepdims=True) + RMS_EPS)
    return (y * g.astype(jnp.float32)).astype(x.dtype)


def layer_norm(x, g, b):
    xf = x.astype(jnp.float32)
    mu = jnp.mean(xf, axis=-1, keepdims=True)
    xc = xf - mu
    y = xc * lax.rsqrt(jnp.mean(xc * xc, axis=-1, keepdims=True) + LN_EPS)
    return (y * g.astype(jnp.float32) + b.astype(jnp.float32)).astype(x.dtype)


def l2_normalize(x):
    xf = x.astype(jnp.float32)
    return xf * lax.rsqrt(jnp.sum(xf * xf, axis=-1, keepdims=True) + 1e-6)


def split_cols(z, sizes):
    idx = np.cumsum(np.array(sizes))[:-1].tolist()
    return jnp.split(z, idx, axis=-1)


def causal_conv(x, buf, w):
    L = x.shape[1]
    xp = jnp.concatenate([buf.astype(x.dtype), x], axis=1)
    y = xp[:, 0:L] * w[0]
    for j in range(1, CONV_W):
        y = y + xp[:, j:j + L] * w[j]
    return y, xp[:, L:]


def _to_blocks(t, n, c):
    t = t.astype(jnp.float32)
    pad = n * c - t.shape[1]
    t = jnp.pad(t, [(0, 0), (0, pad)] + [(0, 0)] * (t.ndim - 2))
    t = t.reshape((t.shape[0], n, c) + t.shape[2:])
    return jnp.moveaxis(t, (1, 3), (0, 2))


def gated_delta_rule(q, k, v, beta, g, S0):
    Bn, L = q.shape[0], q.shape[1]
    dv = v.shape[-1]
    C = min(DN_CHUNK, L)
    n = -(-L // C)
    qb, kb, vb, bb, gb = [_to_blocks(t, n, C) for t in (q, k, v, beta, g)]
    gcum = jnp.cumsum(gb, axis=-1)
    tri = jnp.tril(jnp.ones((C, C), bool))
    strict = jnp.tril(jnp.ones((C, C), bool), -1)
    decay = jnp.exp(jnp.where(tri, gcum[..., :, None] - gcum[..., None, :], -jnp.inf))
    k_beta = kb * bb[..., None]
    lower = jnp.where(strict, jnp.einsum('nbhcd,nbhsd->nbhcs', k_beta, kb) * decay, 0.0)
    a_mat = lower + jnp.eye(C, dtype=jnp.float32)
    rhs = jnp.concatenate([vb * bb[..., None], k_beta * jnp.exp(gcum)[..., None]], axis=-1)
    sol = lax.linalg.triangular_solve(a_mat, rhs, left_side=True, lower=True, unit_diagonal=True)
    u, w = sol[..., :dv], sol[..., dv:]

    def chunk_step(S, blk):
        qc, kc, uc, wc, gc, dc = blk
        v_new = uc - jnp.einsum('bhcd,bhde->bhce', wc, S)
        scores = jnp.einsum('bhcd,bhsd->bhcs', qc, kc) * dc
        o = (jnp.einsum('bhcd,bhde->bhce', qc * jnp.exp(gc)[..., None], S)
             + jnp.einsum('bhcs,bhse->bhce', scores, v_new))
        g_last = gc[..., -1:]
        S = (S * jnp.exp(g_last)[..., None]
             + jnp.einsum('bhcd,bhce->bhde', kc * jnp.exp(g_last - gc)[..., None], v_new))
        return S, o

    S, o = lax.scan(chunk_step, S0.astype(jnp.float32), (qb, kb, u, w, gcum, decay))
    o = jnp.moveaxis(o, (0, 2), (1, 3)).reshape(Bn, n * C, q.shape[2], dv)[:, :L]
    return o, S


def chunk_spatial_gate(v, ws, bs):
    Bn, L, _ = v.shape
    n = -(-L // GM_CHUNK)
    pad = n * GM_CHUNK - L
    vp = jnp.pad(v, ((0, 0), (0, pad), (0, 0))).reshape(Bn, n, GM_CHUNK, GM_GROUPS, GM_GROUP_W)
    w_causal = jnp.where(jnp.tril(jnp.ones((GM_CHUNK, GM_CHUNK), bool)), ws, 0.0)
    s = jnp.einsum('gts,bnsgc->bntgc', w_causal.astype(v.dtype), vp) + bs.T.astype(v.dtype)[:, :, None]
    return s.reshape(Bn, n * GM_CHUNK, MIX_W)[:, :L]


def _lin_combine(left, right):
    a_l, b_l = left
    a_r, b_r = right
    return a_l * a_r, a_r * b_l + b_r


def rg_lru(x, h0, pos0, wa, ba, wx, bx, lam):
    Bn, L, W = x.shape
    xf = x.astype(jnp.float32)
    xb = xf.reshape(Bn, L, LRU_BLOCKS, LRU_BLOCK_W)
    r = jax.nn.sigmoid(jnp.einsum('blhi,hij->blhj', xb, wa.astype(jnp.float32)).reshape(Bn, L, W) + ba)
    i = jax.nn.sigmoid(jnp.einsum('blhi,hij->blhj', xb, wx.astype(jnp.float32)).reshape(Bn, L, W) + bx)
    log_a = -LRU_C * r * jax.nn.softplus(-lam.astype(jnp.float32))
    reset = ((jnp.arange(L) + pos0) == 0)[None, :, None]
    a = jnp.where(reset, 0.0, jnp.exp(log_a))
    mult = jnp.where(reset, 1.0, jnp.sqrt(-jnp.expm1(2.0 * log_a)))
    b = mult * i * xf
    b = b.at[:, 0].add(a[:, 0] * h0.astype(jnp.float32))
    _, hs = lax.associative_scan(_lin_combine, (a, b), axis=1)
    return hs.astype(x.dtype), hs[:, -1]


def swiglu(h, wg, wu, wd):
    return (jax.nn.silu(h @ wg) * (h @ wu)) @ wd


def moe_swiglu(h, rw, rb, wg, wu, wd):
    logits = (h @ rw).astype(jnp.float32) + rb.astype(jnp.float32)
    top_v, top_i = lax.top_k(logits, TOP_K)
    probs = jax.nn.softmax(top_v, axis=-1)
    gates = jnp.sum(jax.nn.one_hot(top_i, N_EXPERTS, dtype=jnp.float32) * probs[..., None], axis=-2).astype(h.dtype)
    out = jnp.zeros_like(h)
    for e in range(N_EXPERTS):
        out = out + gates[..., e:e + 1] * swiglu(h, wg[e], wu[e], wd[e])
    return out


def token_mixers(h, pos0, dn_S, dn_buf, lru_h, lru_buf, p, l):
    Bn, L, _ = h.shape
    z = h @ p['w_in'][l]
    qkv, dn_z, dn_b, dn_a, gm_uv, lru_xy, merge = split_cols(z, IN_SIZES)
    qkv, dn_buf_new = causal_conv(qkv, dn_buf, p['dn_conv_w'][l])
    qkv = jax.nn.silu(qkv)
    q, k, v = [t.reshape(Bn, L, DN_HEADS, DN_HEAD_DIM) for t in jnp.split(qkv, 3, axis=-1)]
    q = l2_normalize(q) * (DN_HEAD_DIM ** -0.5)
    k = l2_normalize(k)
    beta = jax.nn.sigmoid(dn_b.astype(jnp.float32))
    g = -jnp.exp(p['dn_a_log'][l].astype(jnp.float32)) * jax.nn.softplus(
        dn_a.astype(jnp.float32) + p['dn_dt_bias'][l].astype(jnp.float32))
    o, dn_S_new = gated_delta_rule(q, k, v, beta, g, dn_S)
    o = rms_norm(o.astype(h.dtype), p['dn_norm_g'][l]) * jax.nn.silu(dn_z.reshape(Bn, L, DN_HEADS, DN_HEAD_DIM))
    o_a = o.reshape(Bn, L, MIX_W)
    gm_u, gm_v = jnp.split(jax.nn.gelu(gm_uv), 2, axis=-1)
    gm_v = layer_norm(gm_v, p['gm_ln_g'][l], p['gm_ln_b'][l])
    o_b = gm_u * chunk_spatial_gate(gm_v, p['gm_ws'][l], p['gm_bs'][l])
    lru_x, lru_y = jnp.split(lru_xy, 2, axis=-1)
    xc, lru_buf_new = causal_conv(lru_x, lru_buf, p['lru_conv_w'][l])
    xc = xc + p['lru_conv_b'][l]
    hs, lru_h_new = rg_lru(xc, lru_h, pos0, p['lru_wa'][l], p['lru_ba'][l], p['lru_wx'][l],
                           p['lru_bx'][l], p['lru_lambda'][l])
    o_c = jax.nn.gelu(lru_y) * hs
    g_a, g_b, g_c = jnp.split(jax.nn.sigmoid(merge), 3, axis=-1)
    wb = p['w_branch'][l]
    m = g_a * (o_a @ wb[0]) + g_b * (o_b @ wb[1]) + g_c * (o_c @ wb[2])
    y = m @ p['w_out'][l]
    return y, dn_S_new, dn_buf_new, lru_h_new, lru_buf_new, gm_v


def run_trunk(x, c, pos0, dn_S, dn_conv, lru_h, lru_conv, p, keep_chunk_rows):
    outs_S, outs_dc, outs_h, outs_lc, outs_v = [], [], [], [], []
    for l in range(DEPTH):
        mod = (jax.nn.silu(c) @ p['w_mod'][l] + p['b_mod'][l])[:, None, :]
        sh_m, sc_m, gt_m, sh_f, sc_f, gt_f = jnp.split(mod, N_MOD, axis=-1)
        ng = p['norm_g'][l]
        hh = rms_norm(x, ng[0]) * (1.0 + sc_m) + sh_m
        y, S_n, dc_n, h_n, lc_n, v_rows = token_mixers(hh, pos0, dn_S[l], dn_conv[l], lru_h[l], lru_conv[l], p, l)
        x = x + gt_m * rms_norm(y, ng[1])
        hh = rms_norm(x, ng[2]) * (1.0 + sc_f) + sh_f
        if l % 2 == 0:
            j = l // 2
            y = swiglu(hh, p['ffn_w_gate'][j], p['ffn_w_up'][j], p['ffn_w_down'][j])
        else:
            j = l // 2
            y = moe_swiglu(hh, p['router_w'][j], p['router_b'][j], p['moe_w_gate'][j],
                           p['moe_w_up'][j], p['moe_w_down'][j])
        x = x + gt_f * rms_norm(y, ng[3])
        outs_S.append(S_n)
        outs_dc.append(dc_n)
        outs_h.append(h_n)
        outs_lc.append(lc_n)
        if keep_chunk_rows:
            outs_v.append(v_rows)
    v_out = jnp.stack(outs_v) if keep_chunk_rows else None
    return x, jnp.stack(outs_S), jnp.stack(outs_dc), jnp.stack(outs_h), jnp.stack(outs_lc), v_out


def setup_inputs(seed: int = 0) -> dict:
    key = jax.random.key(seed)
    ks = iter(jax.random.split(key, 48))

    def nrm(shape, scale):
        return jax.random.normal(next(ks), shape, jnp.float32) * scale

    def unif(shape, lo, hi):
        return jax.random.uniform(next(ks), shape, jnp.float32, lo, hi)

    n_dense = (DEPTH + 1) // 2
    n_moe = DEPTH // 2
    lru_a = unif((DEPTH, MIX_W), 0.9, 0.999) ** (1.0 / LRU_C)
    dt = jnp.exp(unif((DEPTH, DN_HEADS), math.log(1e-3), math.log(1e-1)))
    return {
        'x_prompt': nrm((BATCH, SEQ, D_MODEL), 1.0),
        'x_sample': nrm((DEC_BATCH, DEC_SEQ, D_MODEL), 1.0),
        'state_dn_S': nrm((DEPTH, DEC_BATCH, DN_HEADS, DN_HEAD_DIM, DN_HEAD_DIM), 0.1),
        'state_dn_conv': nrm((DEPTH, DEC_BATCH, CONV_W - 1, 3 * MIX_W), 1.0),
        'state_lru_h': nrm((DEPTH, DEC_BATCH, MIX_W), 0.5),
        'state_lru_conv': nrm((DEPTH, DEC_BATCH, CONV_W - 1, MIX_W), 1.0),
        'c_prompt': nrm((BATCH, D_MODEL), 1.0),
        'c_sample': nrm((DEC_BATCH, D_MODEL), 1.0),
        'w_mod': nrm((DEPTH, D_MODEL, N_MOD * D_MODEL), 0.5 * D_MODEL ** -0.5),
        'b_mod': nrm((DEPTH, N_MOD * D_MODEL), 0.02),
        'norm_g': 1.0 + nrm((DEPTH, 4, D_MODEL), 0.05),
        'w_in': nrm((DEPTH, D_MODEL, N_IN), D_MODEL ** -0.5),
        'dn_conv_w': nrm((DEPTH, CONV_W, 3 * MIX_W), CONV_W ** -0.5),
        'dn_a_log': jnp.log(unif((DEPTH, DN_HEADS), 1.0, 16.0)),
        'dn_dt_bias': dt + jnp.log(-jnp.expm1(-dt)),
        'dn_norm_g': 1.0 + nrm((DEPTH, DN_HEAD_DIM), 0.05),
        'gm_ln_g': 1.0 + nrm((DEPTH, MIX_W), 0.05),
        'gm_ln_b': nrm((DEPTH, MIX_W), 0.02),
        'gm_ws': nrm((DEPTH, GM_GROUPS, GM_CHUNK, GM_CHUNK), GM_CHUNK ** -0.5),
        'gm_bs': 1.0 + nrm((DEPTH, GM_GROUPS, GM_CHUNK), 0.1),
        'lru_conv_w': nrm((DEPTH, CONV_W, MIX_W), CONV_W ** -0.5),
        'lru_conv_b': nrm((DEPTH, MIX_W), 0.02),
        'lru_wa': nrm((DEPTH, LRU_BLOCKS, LRU_BLOCK_W, LRU_BLOCK_W), LRU_BLOCK_W ** -0.5),
        'lru_ba': nrm((DEPTH, MIX_W), 0.02),
        'lru_wx': nrm((DEPTH, LRU_BLOCKS, LRU_BLOCK_W, LRU_BLOCK_W), LRU_BLOCK_W ** -0.5),
        'lru_bx': nrm((DEPTH, MIX_W), 0.02),
        'lru_lambda': jnp.log(lru_a) - jnp.log1p(-lru_a),
        'w_branch': nrm((DEPTH, N_BRANCH, MIX_W, D_MODEL), MIX_W ** -0.5),
        'w_out': nrm((DEPTH, D_MODEL, D_MODEL), D_MODEL ** -0.5),
        'ffn_w_gate': nrm((n_dense, D_MODEL, D_FF), D_MODEL ** -0.5),
        'ffn_w_up': nrm((n_dense, D_MODEL, D_FF), D_MODEL ** -0.5),
        'ffn_w_down': nrm((n_dense, D_FF, D_MODEL), D_FF ** -0.5),
        'router_w': nrm((n_moe, D_MODEL, N_EXPERTS), D_MODEL ** -0.5),
        'router_b': nrm((n_moe, N_EXPERTS), 0.01),
        'moe_w_gate': nrm((n_moe, N_EXPERTS, D_MODEL, D_FF), D_MODEL ** -0.5),
        'moe_w_up': nrm((n_moe, N_EXPERTS, D_MODEL, D_FF), D_MODEL ** -0.5),
        'moe_w_down': nrm((n_moe, N_EXPERTS, D_FF, D_MODEL), D_FF ** -0.5),
    }


def reference(x_prompt, x_sample, state_dn_S, state_dn_conv, state_lru_h, state_lru_conv, c_prompt, c_sample,
              w_mod, b_mod, norm_g, w_in, dn_conv_w, dn_a_log, dn_dt_bias, dn_norm_g, gm_ln_g, gm_ln_b, gm_ws, gm_bs,
              lru_conv_w, lru_conv_b, lru_wa, lru_ba, lru_wx, lru_bx, lru_lambda, w_branch, w_out,
              ffn_w_gate, ffn_w_up, ffn_w_down, router_w, router_b, moe_w_gate, moe_w_up, moe_w_down):
    p = dict(w_mod=w_mod, b_mod=b_mod, norm_g=norm_g, w_in=w_in, dn_conv_w=dn_conv_w, dn_a_log=dn_a_log,
             dn_dt_bias=dn_dt_bias, dn_norm_g=dn_norm_g, gm_ln_g=gm_ln_g, gm_ln_b=gm_ln_b, gm_ws=gm_ws, gm_bs=gm_bs,
             lru_conv_w=lru_conv_w, lru_conv_b=lru_conv_b, lru_wa=lru_wa, lru_ba=lru_ba, lru_wx=lru_wx,
             lru_bx=lru_bx, lru_lambda=lru_lambda, w_branch=w_branch, w_out=w_out, ffn_w_gate=ffn_w_gate,
             ffn_w_up=ffn_w_up, ffn_w_down=ffn_w_down, router_w=router_w, router_b=router_b,
             moe_w_gate=moe_w_gate, moe_w_up=moe_w_up, moe_w_down=moe_w_down)
    bp = x_prompt.shape[0]
    z_S = jnp.zeros((DEPTH, bp, DN_HEADS, DN_HEAD_DIM, DN_HEAD_DIM), jnp.float32)
    z_dc = jnp.zeros((DEPTH, bp, CONV_W - 1, 3 * MIX_W), x_prompt.dtype)
    z_h = jnp.zeros((DEPTH, bp, MIX_W), jnp.float32)
    z_lc = jnp.zeros((DEPTH, bp, CONV_W - 1, MIX_W), x_prompt.dtype)
    y_prompt, S_p, dc_p, h_p, lc_p, _ = run_trunk(x_prompt, c_prompt, 0, z_S, z_dc, z_h, z_lc, p, False)
    y_sample, S_s, dc_s, h_s, lc_s, v_s = run_trunk(x_sample, c_sample, PAST_LEN, state_dn_S, state_dn_conv,
                                                    state_lru_h, state_lru_conv, p, True)
    return (y_prompt, y_sample, S_p, dc_p, h_p, lc_p, S_s, dc_s, h_s, lc_s, v_s)
```

```python
import functools

import jax
import jax.numpy as jnp
from jax import lax
from jax.experimental import pallas as pl
from jax.experimental.pallas import tpu as pltpu

F32, BF16, I32 = jnp.float32, jnp.bfloat16, jnp.int32
HI = lax.Precision.HIGHEST

LANE = 128
SUBLANE = 8
VMEM_LIMIT = 56 << 20
DN_HEAD_DIM = 128
DN_CHUNK = 64
CONV_W = 4
GM_CHUNK = 128
LRU_BLOCKS = 8
LRU_C = 8.0
N_MOD = 6
TOP_K = 2
RMS_EPS = 1e-6
LN_EPS = 1e-5
L2_EPS = 1e-6
NEG_BIG = -1e30

TM = 512
TN = 512
TE = 256
TG = 512


def _params(sem):
    return pltpu.CompilerParams(dimension_semantics=sem, vmem_limit_bytes=VMEM_LIMIT)


def _act(name, x):
    if name is None:
        return x
    if name == "silu":
        return x * jax.nn.sigmoid(x)
    if name == "gelu":
        return jax.nn.gelu(x)
    if name == "sigmoid":
        return jax.nn.sigmoid(x)
    raise ValueError(name)


def _rms(x, g):
    return x * lax.rsqrt(jnp.mean(x * x, axis=-1, keepdims=True) + RMS_EPS) * g


def _cast_once(meta_ref, i, pairs):
    prev = meta_ref[jnp.maximum(i - 1, 0)]

    @pl.when((i == 0) | (meta_ref[i] != prev))
    def _():
        for src, dst in pairs:
            dst[...] = src[...].astype(BF16)


def _mm_body(meta_ref, x_ref, w_ref, *rest, ni, act, pre_act, cast_w, has_bias):
    rest = list(rest)
    b_ref = rest.pop(0) if has_bias else None
    o_ref = rest.pop(0)
    wbf_ref = rest.pop(0) if cast_w else None
    i = pl.program_id(1)
    nused = meta_ref[ni]

    @pl.when(i < nused)
    def _():
        if cast_w:
            _cast_once(meta_ref, i, [(w_ref, wbf_ref)])
            w = wbf_ref[...]
        else:
            w = w_ref[...]
        x = x_ref[...]
        if pre_act is not None:
            x = _act(pre_act, x.astype(F32))
        acc = jnp.dot(x.astype(BF16), w, preferred_element_type=F32)
        if has_bias:
            acc = acc + b_ref[...]
        o_ref[...] = _act(act, acc).astype(o_ref.dtype)

    @pl.when(i >= nused)
    def _():
        o_ref[...] = jnp.zeros_like(o_ref)


def _mm(x, w, meta, *, name, n_off=0, n=None, tm=TM, tn=TN, act=None, pre_act=None,
        bias=None, out_dtype=F32):
    m_rows, k = x.shape
    n = w.shape[2] if n is None else n
    tm = min(tm, m_rows)
    ni, nj = pl.cdiv(m_rows, tm), n // tn
    assert n % tn == 0 and meta.shape[0] == ni + 1
    cast_w = w.dtype != BF16
    body = functools.partial(_mm_body, ni=ni, act=act, pre_act=pre_act, cast_w=cast_w,
                             has_bias=bias is not None)
    in_specs = [pl.BlockSpec((tm, k), lambda j, i, m: (jnp.minimum(i, m[ni] - 1), 0)),
                pl.BlockSpec((None, k, tn), lambda j, i, m: (m[i], 0, j + n_off))]
    args = [x, w]
    if bias is not None:
        in_specs.append(pl.BlockSpec((None, 1, tn), lambda j, i, m: (m[i], 0, j + n_off)))
        args.append(bias)
    scratch = [pltpu.VMEM((k, tn), BF16)] if cast_w else []
    return pl.pallas_call(
        body,
        grid_spec=pltpu.PrefetchScalarGridSpec(
            num_scalar_prefetch=1, grid=(nj, ni), in_specs=in_specs,
            out_specs=pl.BlockSpec((tm, tn), lambda j, i, m: (i, j)), scratch_shapes=scratch),
        out_shape=jax.ShapeDtypeStruct((m_rows, n), out_dtype),
        compiler_params=_params(("arbitrary", "arbitrary")), name=name)(meta, *args)


def _ffn_up_body(meta_ref, x_ref, wg_ref, wu_ref, o_ref, wgb_ref, wub_ref, *, ni):
    i = pl.program_id(1)
    nused = meta_ref[ni]

    @pl.when(i < nused)
    def _():
        _cast_once(meta_ref, i, [(wg_ref, wgb_ref), (wu_ref, wub_ref)])
        x = x_ref[...]
        g = jnp.dot(x, wgb_ref[...], preferred_element_type=F32)
        u = jnp.dot(x, wub_ref[...], preferred_element_type=F32)
        o_ref[...] = (g * jax.nn.sigmoid(g) * u).astype(o_ref.dtype)

    @pl.when(i >= nused)
    def _():
        o_ref[...] = jnp.zeros_like(o_ref)


def _ffn_up(x, wg, wu, meta, *, name, tm=TM, tn=TN):
    m_rows, k = x.shape
    n = wg.shape[2]
    ni, nj = m_rows // tm, n // tn
    assert m_rows % tm == 0 and n % tn == 0 and meta.shape[0] == ni + 1
    wspec = pl.BlockSpec((None, k, tn), lambda j, i, m: (m[i], 0, j))
    return pl.pallas_call(
        functools.partial(_ffn_up_body, ni=ni),
        grid_spec=pltpu.PrefetchScalarGridSpec(
            num_scalar_prefetch=1, grid=(nj, ni),
            in_specs=[pl.BlockSpec((tm, k), lambda j, i, m: (jnp.minimum(i, m[ni] - 1), 0)),
                      wspec, wspec],
            out_specs=pl.BlockSpec((tm, tn), lambda j, i, m: (i, j)),
            scratch_shapes=[pltpu.VMEM((k, tn), BF16), pltpu.VMEM((k, tn), BF16)]),
        out_shape=jax.ShapeDtypeStruct((m_rows, n), BF16),
        compiler_params=_params(("arbitrary", "arbitrary")), name=name)(meta, x, wg, wu)


def _branch_body(meta_ref, xa_ref, xb_ref, xc_ref, ga_ref, gb_ref, gc_ref,
                 w0_ref, w1_ref, w2_ref, o_ref, s0_ref, s1_ref, s2_ref):
    i = pl.program_id(1)
    _cast_once(meta_ref, i, [(w0_ref, s0_ref), (w1_ref, s1_ref), (w2_ref, s2_ref)])
    acc = ga_ref[...] * jnp.dot(xa_ref[...], s0_ref[...], preferred_element_type=F32)
    acc = acc + gb_ref[...] * jnp.dot(xb_ref[...], s1_ref[...], preferred_element_type=F32)
    acc = acc + gc_ref[...] * jnp.dot(xc_ref[...], s2_ref[...], preferred_element_type=F32)
    o_ref[...] = acc.astype(o_ref.dtype)


def _branch_mm(oa, ob, oc, gates, wb, meta, *, name, tm=TM, tn=TN):
    m_rows, k = oa.shape
    n = wb.shape[2]
    ni, nj = m_rows // tm, n // tn
    xspec = pl.BlockSpec((tm, k), lambda j, i, m: (i, 0))

    def gspec(b):
        return pl.BlockSpec((tm, tn), lambda j, i, m: (i, j + b * nj))

    def wspec(b):
        return pl.BlockSpec((None, k, tn), lambda j, i, m: (m[i] * 3 + b, 0, j))

    return pl.pallas_call(
        _branch_body,
        grid_spec=pltpu.PrefetchScalarGridSpec(
            num_scalar_prefetch=1, grid=(nj, ni),
            in_specs=[xspec, xspec, xspec, gspec(0), gspec(1), gspec(2), wspec(0), wspec(1), wspec(2)],
            out_specs=pl.BlockSpec((tm, tn), lambda j, i, m: (i, j)),
            scratch_shapes=[pltpu.VMEM((k, tn), BF16)] * 3),
        out_shape=jax.ShapeDtypeStruct((m_rows, n), BF16),
        compiler_params=_params(("arbitrary", "arbitrary")), name=name)(
            meta, oa, ob, oc, gates, gates, gates, wb, wb, wb)


def _route(hh, rw_ref, rb_ref):
    lg = jnp.dot(hh, rw_ref[...], precision=HI, preferred_element_type=F32) + rb_ref[...]
    lane = lax.broadcasted_iota(I32, lg.shape, 1)
    m1 = jnp.max(lg, axis=1, keepdims=True)
    i1 = jnp.min(jnp.where(lg == m1, lane, LANE), axis=1, keepdims=True)
    lg2 = jnp.where(lane == i1, NEG_BIG * 2, lg)
    m2 = jnp.max(lg2, axis=1, keepdims=True)
    i2 = jnp.min(jnp.where(lg2 == m2, lane, LANE), axis=1, keepdims=True)
    e = jnp.exp(m2 - m1)
    p1 = 1.0 / (1.0 + e)
    p2 = e / (1.0 + e)
    return jnp.where(lane == 0, p1, jnp.where(lane == 1, p2, jnp.where(
        lane == 2, i1.astype(F32), jnp.where(lane == 3, i2.astype(F32), 0.0))))


def _post_body(*refs, npt, reps, res, nxt, router):
    it = iter(refs)
    x_ref = next(it)
    y_ref = next(it) if res else None
    gres_ref = next(it) if res else None
    gnx_ref = next(it) if nxt else None
    mpr_ref = next(it) if res else None
    mpn_ref = next(it) if nxt else None
    msg_ref = next(it) if res else None
    msc_ref = next(it) if nxt else None
    msh_ref = next(it) if nxt else None
    rw_ref = next(it) if router else None
    rb_ref = next(it) if router else None
    xo_ref = next(it) if res else None
    hh_ref = next(it) if nxt else None
    rt_ref = next(it) if router else None
    i = pl.program_id(0)

    def compute(gate, sc, sh):
        x = x_ref[...]
        if res:
            x = x + gate * _rms(y_ref[...], gres_ref[...])
            xo_ref[...] = x
        if nxt:
            hh = _rms(x, gnx_ref[...]) * (1.0 + sc) + sh
            hh_ref[...] = hh.astype(hh_ref.dtype)
            if router:
                rt_ref[...] = _route(hh, rw_ref, rb_ref)

    def rep(r):
        return jnp.concatenate([r[...]] * reps, axis=0) if reps > 1 else r[...]

    @pl.when(i < npt)
    def _():
        compute(mpr_ref[res[2]:res[2] + 1, :] if res else None,
                mpn_ref[nxt[2]:nxt[2] + 1, :] if nxt else None,
                mpn_ref[nxt[3]:nxt[3] + 1, :] if nxt else None)

    @pl.when(i >= npt)
    def _():
        compute(rep(msg_ref) if res else None, rep(msc_ref) if nxt else None,
                rep(msh_ref) if nxt else None)


def _post(x, y, cfg, *, res, nxt, router=None, hh_dtype=BF16, name):
    t_rows, d = x.shape
    te, bp, bs = TE, cfg["bp"], cfg["bs"]
    npt, nt, tpb = cfg["tp"] // te, t_rows // te, cfg["seq"] // te
    assert te % bs == 0 and cfg["seq"] % te == 0 and t_rows % te == 0
    row = pl.BlockSpec((te, d), lambda i: (i, 0))
    in_specs, args = [row], [x]
    if res:
        in_specs.append(row)
        args.append(y)

    def gspec(layer, r):
        return pl.BlockSpec((None, 1, d), lambda i: (layer * 4 + r, 0, 0))

    def mpspec(layer):
        return pl.BlockSpec((None, N_MOD, d), lambda i: (layer * bp + jnp.minimum(i // tpb, bp - 1), 0, 0))

    def msspec(layer, plane):
        return pl.BlockSpec((None, bs, d), lambda i: (layer * N_MOD + plane, 0, 0))

    if res:
        in_specs.append(gspec(res[0], res[1])); args.append(cfg["norm_g"])
    if nxt:
        in_specs.append(gspec(nxt[0], nxt[1])); args.append(cfg["norm_g"])
    if res:
        in_specs.append(mpspec(res[0])); args.append(cfg["mod_p"])
    if nxt:
        in_specs.append(mpspec(nxt[0])); args.append(cfg["mod_p"])
    if res:
        in_specs.append(msspec(res[0], res[2])); args.append(cfg["mod_s"])
    if nxt:
        in_specs.append(msspec(nxt[0], nxt[2])); args.append(cfg["mod_s"])
        in_specs.append(msspec(nxt[0], nxt[3])); args.append(cfg["mod_s"])
    if router:
        rw, rb, slab = router
        in_specs.append(pl.BlockSpec((None, d, LANE), lambda i: (slab, 0, 0))); args.append(rw)
        in_specs.append(pl.BlockSpec((None, 1, LANE), lambda i: (slab, 0, 0))); args.append(rb)
    out_specs, out_shape = [], []
    if res:
        out_specs.append(row); out_shape.append(jax.ShapeDtypeStruct((t_rows, d), F32))
    if nxt:
        out_specs.append(row); out_shape.append(jax.ShapeDtypeStruct((t_rows, d), hh_dtype))
    if router:
        out_specs.append(pl.BlockSpec((te, LANE), lambda i: (i, 0)))
        out_shape.append(jax.ShapeDtypeStruct((t_rows, LANE), F32))
    body = functools.partial(_post_body, npt=npt, reps=te // bs, res=res, nxt=nxt, router=bool(router))
    return pl.pallas_call(body, grid=(nt,), in_specs=in_specs, out_specs=out_specs, out_shape=out_shape,
                          compiler_params=_params(("arbitrary",)), name=name)(*args)


def _row_copy(src_hbm, row, dst_ref, r, sem):
    return pltpu.make_async_copy(src_hbm.at[pl.ds(row, 1), :], dst_ref.at[pl.ds(r, 1), :], sem)


def _gather_rows(idx_ref, base, src_hbm, dst_ref, sem, n):
    def start(r, c):
        _row_copy(src_hbm, idx_ref[base + r], dst_ref, r, sem).start()
        return c

    def wait(r, c):
        _row_copy(src_hbm, 0, dst_ref, r, sem).wait()
        return c

    lax.fori_loop(0, n, start, 0)
    lax.fori_loop(0, n, wait, 0)


def _dispatch_body(src_ref, nused_ref, x_hbm, o_ref, buf_ref, sem, *, tm):
    i = pl.program_id(0)

    @pl.when(i < nused_ref[0])
    def _():
        _gather_rows(src_ref, i * tm, x_hbm, buf_ref, sem, tm)
        o_ref[...] = buf_ref[...].astype(o_ref.dtype)

    @pl.when(i >= nused_ref[0])
    def _():
        o_ref[...] = jnp.zeros_like(o_ref)


def _dispatch(hh, src, nused, *, tm, name):
    cap = src.shape[0]
    d = hh.shape[1]
    return pl.pallas_call(
        functools.partial(_dispatch_body, tm=tm),
        grid_spec=pltpu.PrefetchScalarGridSpec(
            num_scalar_prefetch=2, grid=(cap // tm,),
            in_specs=[pl.BlockSpec(memory_space=pl.ANY)],
            out_specs=pl.BlockSpec((tm, d), lambda i, s, n: (i, 0)),
            scratch_shapes=[pltpu.VMEM((tm, d), hh.dtype), pltpu.SemaphoreType.DMA(())]),
        out_shape=jax.ShapeDtypeStruct((cap, d), BF16),
        compiler_params=_params(("arbitrary",)), name=name)(src, nused, hh)


def _combine_body(d0_ref, d1_ref, y_hbm, p_ref, o_ref, a_ref, b_ref, sem_a, sem_b, *, te):
    i = pl.program_id(0)

    def start(r, c):
        _row_copy(y_hbm, d0_ref[i * te + r], a_ref, r, sem_a).start()
        _row_copy(y_hbm, d1_ref[i * te + r], b_ref, r, sem_b).start()
        return c

    def wait(r, c):
        _row_copy(y_hbm, 0, a_ref, r, sem_a).wait()
        _row_copy(y_hbm, 0, b_ref, r, sem_b).wait()
        return c

    lax.fori_loop(0, te, start, 0)
    lax.fori_loop(0, te, wait, 0)
    p = p_ref[...]
    o_ref[...] = p[:, 0:1] * a_ref[...] + p[:, 1:2] * b_ref[...]


def _combine(ys, dest0, dest1, route, *, name):
    t_rows = route.shape[0]
    d = ys.shape[1]
    te = TE
    return pl.pallas_call(
        functools.partial(_combine_body, te=te),
        grid_spec=pltpu.PrefetchScalarGridSpec(
            num_scalar_prefetch=2, grid=(t_rows // te,),
            in_specs=[pl.BlockSpec(memory_space=pl.ANY),
                      pl.BlockSpec((te, LANE), lambda i, a, b: (i, 0))],
            out_specs=pl.BlockSpec((te, d), lambda i, a, b: (i, 0)),
            scratch_shapes=[pltpu.VMEM((te, d), F32), pltpu.VMEM((te, d), F32),
                            pltpu.SemaphoreType.DMA(()), pltpu.SemaphoreType.DMA(())]),
        out_shape=jax.ShapeDtypeStruct((t_rows, d), F32),
        compiler_params=_params(("arbitrary",)), name=name)(dest0, dest1, ys, route)


def _dn_post(y, j):
    y = y * jax.nn.sigmoid(y)
    outs = []
    for h in range(y.shape[1] // DN_HEAD_DIM):
        yh = y[:, h * DN_HEAD_DIM:(h + 1) * DN_HEAD_DIM]
        n = lax.rsqrt(jnp.sum(yh * yh, axis=-1, keepdims=True) + L2_EPS)
        s = jnp.where(j == 0, n * (DN_HEAD_DIM ** -0.5), jnp.where(j == 1, n, 1.0))
        outs.append(yh * s)
    return jnp.concatenate(outs, axis=1)


def _conv_rows(x_ref, prev_ref, buf_ref, w, first):
    te = x_ref.shape[0]
    x = x_ref[...]
    buf_ref[0:SUBLANE, :] = jnp.where(first, 0.0, prev_ref[...])
    buf_ref[SUBLANE:, :] = x
    y = w[CONV_W - 1:CONV_W, :] * x
    for dlt in range(1, CONV_W):
        y = y + w[CONV_W - 1 - dlt:CONV_W - dlt, :] * buf_ref[SUBLANE - dlt:SUBLANE - dlt + te, :]
    return y


def _conv_slabs(x_ref, hist_ref, w, bs):
    xp = jnp.concatenate([hist_ref[...], x_ref[...]], axis=0)
    steps = x_ref.shape[0] // bs
    ys = []
    for t in range(steps):
        y = w[0:1, :] * xp[t * bs:(t + 1) * bs]
        for j in range(1, CONV_W):
            y = y + w[j:j + 1, :] * xp[(t + j) * bs:(t + j + 1) * bs]
        ys.append(y)
    return ys


def _dn_prep_p_body(x_ref, prev_ref, w_ref, o_ref, buf_ref, *, tps):
    i, j = pl.program_id(0), pl.program_id(1)
    y = _conv_rows(x_ref, prev_ref, buf_ref, w_ref[...], i % tps == 0)
    o_ref[...] = _dn_post(y, j)


def _dn_prep_s_body(x_ref, hist_ref, w_ref, o_ref, *, bs):
    j = pl.program_id(0)
    ys = _conv_slabs(x_ref, hist_ref, w_ref[...], bs)
    o_ref[...] = _dn_post(jnp.concatenate(ys, axis=0), j)


def _dn_prep(qkv, hist_s, conv_w, layer, cfg):
    tp, ts, bs, te = cfg["tp"], cfg["ts"], cfg["bs"], TE
    w3 = qkv.shape[1]
    cb = w3 // 3
    rpb = te // SUBLANE
    qkv_p = pl.pallas_call(
        functools.partial(_dn_prep_p_body, tps=cfg["seq"] // te),
        grid=(tp // te, 3),
        in_specs=[pl.BlockSpec((te, cb), lambda i, j: (i, j)),
                  pl.BlockSpec((SUBLANE, cb), lambda i, j: (jnp.maximum(i * rpb - 1, 0), j)),
                  pl.BlockSpec((None, CONV_W, cb), lambda i, j: (layer, 0, j))],
        out_specs=pl.BlockSpec((te, cb), lambda i, j: (i, j)),
        out_shape=jax.ShapeDtypeStruct((tp, w3), F32),
        scratch_shapes=[pltpu.VMEM((te + SUBLANE, cb), F32)],
        compiler_params=_params(("arbitrary", "arbitrary")), name="dn_prep_p")(qkv, qkv, conv_w)
    qkv_s = pl.pallas_call(
        functools.partial(_dn_prep_s_body, bs=bs),
        grid=(3,),
        in_specs=[pl.BlockSpec((ts, cb), lambda j: (tp // ts, j)),
                  pl.BlockSpec(((CONV_W - 1) * bs, cb), lambda j: (0, j)),
                  pl.BlockSpec((None, CONV_W, cb), lambda j: (layer, 0, j))],
        out_specs=pl.BlockSpec((ts, cb), lambda j: (0, j)),
        out_shape=jax.ShapeDtypeStruct((ts, w3), F32),
        compiler_params=_params(("arbitrary",)), name="dn_prep_s")(qkv, hist_s, conv_w)
    return qkv_p, qkv_s


def _dotf(a, b):
    return jnp.dot(a, b, precision=HI, preferred_element_type=F32)


def _inv_unit_lower(lm, rpg):
    c = lm.shape[0]
    row = lax.broadcasted_iota(I32, (c, c), 0)
    col = lax.broadcasted_iota(I32, (c, c), 1)
    eye = (row == col).astype(F32)
    blk = (row // SUBLANE) == (col // SUBLANE)
    ld = jnp.where(blk, lm, 0.0)
    l2 = _dotf(ld, ld)
    l4 = _dotf(l2, l2)
    t1 = _dotf(_dotf(eye - ld, eye + l2), eye + l4)
    if rpg <= SUBLANE:
        return t1
    n1 = _dotf(t1, lm - ld)
    n2 = _dotf(n1, n1)
    n4 = _dotf(n2, n2)
    minv = _dotf(_dotf(eye - n1, eye + n2), eye + n4)
    return _dotf(minv, t1)


def _dn_chunk_body(q_ref, k_ref, v_ref, z_ref, ba_ref, s0_ref, al_ref, dt_ref, ng_ref,
                   o_ref, so_ref, s_ref, *, groups, n_valid):
    c = q_ref.shape[0]
    heads = q_ref.shape[1] // DN_HEAD_DIM
    rpg = c // groups
    ci = pl.program_id(1)

    @pl.when(ci == 0)
    def _():
        s_ref[...] = s0_ref[...]

    row = lax.broadcasted_iota(I32, (c, c), 0)
    col = lax.broadcasted_iota(I32, (c, c), 1)
    same = (row // rpg) == (col // rpg)
    tri = (row >= col) & same
    strict = (row > col) & same
    rowl = lax.broadcasted_iota(I32, (c, LANE), 0)
    lanel = lax.broadcasted_iota(I32, (c, LANE), 1)
    valid = (rowl % rpg) < n_valid
    ba = ba_ref[...]
    beta_all = jnp.where(valid & (lanel < heads), jax.nn.sigmoid(ba), 0.0)
    g_all = jnp.where(valid, -jnp.exp(al_ref[...]) * jax.nn.softplus(ba + dt_ref[...]), 0.0)
    gcum_all = _dotf(tri.astype(F32), g_all)
    gcum_t = jnp.concatenate([gcum_all, jnp.zeros((LANE - c, LANE), F32)], axis=0).T
    rg = lax.broadcasted_iota(I32, (c, 1), 0) // rpg

    for h in range(heads):
        hs = slice(h * DN_HEAD_DIM, (h + 1) * DN_HEAD_DIM)
        q, k, v = q_ref[:, hs], k_ref[:, hs], v_ref[:, hs]
        beta = beta_all[:, h:h + 1]
        gcol = gcum_all[:, heads + h:heads + h + 1]
        grow = gcum_t[heads + h:heads + h + 1, 0:c]
        decay = jnp.exp(jnp.where(tri, gcol - grow, NEG_BIG))
        kb = k * beta
        k16 = k.astype(BF16)
        lfull = lax.dot_general(kb.astype(BF16), k16, (((1,), (1,)), ((), ())), preferred_element_type=F32)
        tinv = _inv_unit_lower(jnp.where(strict, lfull * decay, 0.0), rpg)
        scores = lax.dot_general(q.astype(BF16), k16, (((1,), (1,)), ((), ())),
                                 preferred_element_type=F32) * decay
        egc = jnp.exp(gcol)
        lhs = jnp.concatenate([kb * egc, q * egc], axis=0).astype(BF16)
        ws = jnp.zeros((2 * c, DN_HEAD_DIM), F32)
        glast_rows = jnp.zeros((c, 1), F32)
        glasts = []
        for gi in range(groups):
            r = jnp.dot(lhs, s_ref[gi, h].astype(BF16), preferred_element_type=F32)
            gl = gcol[(gi + 1) * rpg - 1:(gi + 1) * rpg, :]
            glasts.append(gl)
            if groups == 1:
                ws, glast_rows = r, jnp.broadcast_to(gl, (c, 1))
            else:
                m2 = jnp.concatenate([rg, rg], axis=0) == gi
                ws = jnp.where(m2, r, ws)
                glast_rows = jnp.where(rg == gi, gl, glast_rows)
        v_new = _dotf(tinv, v * beta - ws[0:c])
        vn16 = v_new.astype(BF16)
        o = ws[c:2 * c] + jnp.dot(scores.astype(BF16), vn16, preferred_element_type=F32)
        kdec = k * jnp.exp(glast_rows - gcol)
        for gi in range(groups):
            kd = kdec if groups == 1 else jnp.where(rg == gi, kdec, 0.0)
            upd = lax.dot_general(kd.astype(BF16), vn16, (((0,), (0,)), ((), ())), preferred_element_type=F32)
            s_ref[gi, h] = s_ref[gi, h] * jnp.exp(glasts[gi]) + upd
        o = _rms(o, ng_ref[...]) * z_ref[:, hs]
        o_ref[:, hs] = o.astype(o_ref.dtype)

    @pl.when(ci == pl.num_programs(1) - 1)
    def _():
        so_ref[...] = s_ref[...]


def _dn_chunks(qkv, z, ba, s0, al, dt, ng, layer, *, groups, n_valid, n_chunks, name):
    rows, w3 = qkv.shape
    w = w3 // 3
    heads = w // DN_HEAD_DIM
    c = DN_CHUNK
    nblk = rows // (c * n_chunks)
    rspec = lambda col: pl.BlockSpec((c, w), lambda b, n: (b * n_chunks + n, col))
    vec = pl.BlockSpec((None, 1, LANE), lambda b, n: (layer, 0, 0))
    sspec = pl.BlockSpec((groups, heads, DN_HEAD_DIM, DN_HEAD_DIM), lambda b, n: (b, 0, 0, 0))
    return pl.pallas_call(
        functools.partial(_dn_chunk_body, groups=groups, n_valid=n_valid),
        grid=(nblk, n_chunks),
        in_specs=[rspec(0), rspec(1), rspec(2), rspec(0),
                  pl.BlockSpec((c, LANE), lambda b, n: (b * n_chunks + n, 0)),
                  sspec, vec, vec, vec],
        out_specs=[rspec(0), sspec],
        out_shape=[jax.ShapeDtypeStruct((rows, w), BF16), jax.ShapeDtypeStruct(s0.shape, F32)],
        scratch_shapes=[pltpu.VMEM((groups, heads, DN_HEAD_DIM, DN_HEAD_DIM), F32)],
        compiler_params=_params(("arbitrary", "arbitrary")), name=name)(
            qkv, qkv, qkv, z, ba, s0, al, dt, ng)


def _layer_norm(v, g, b):
    mu = jnp.mean(v, axis=-1, keepdims=True)
    vc = v - mu
    return vc * lax.rsqrt(jnp.mean(vc * vc, axis=-1, keepdims=True) + LN_EPS) * g + b


def _gmlp_p_body(uv_ref, g_ref, b_ref, ws_ref, bst_ref, o_ref):
    w = uv_ref.shape[1] // 2
    groups = ws_ref.shape[0]
    gw = w // groups
    vn = _layer_norm(uv_ref[:, w:], g_ref[...], b_ref[...]).astype(BF16)
    row = lax.broadcasted_iota(I32, (GM_CHUNK, GM_CHUNK), 0)
    col = lax.broadcasted_iota(I32, (GM_CHUNK, GM_CHUNK), 1)
    wcs = [jnp.where(row >= col, ws_ref[g], 0.0).astype(BF16) for g in range(groups)]
    bst = bst_ref[...]
    for c in range(uv_ref.shape[0] // GM_CHUNK):
        rs = slice(c * GM_CHUNK, (c + 1) * GM_CHUNK)
        for g in range(groups):
            cs = slice(g * gw, (g + 1) * gw)
            s = jnp.dot(wcs[g], vn[rs, cs], preferred_element_type=F32) + bst[:, g:g + 1]
            o_ref[rs, cs] = (uv_ref[rs, cs] * s).astype(o_ref.dtype)


def _gmlp_s_body(uv_ref, g_ref, b_ref, cw_ref, cb_ref, o_ref, vn_ref, *, bs):
    w = uv_ref.shape[1] // 2
    steps = uv_ref.shape[0] // bs
    vn = _layer_norm(uv_ref[:, w:], g_ref[...], b_ref[...])
    vn_ref[...] = vn
    for t in range(steps):
        s = cb_ref[t:t + 1, :]
        for u in range(t + 1):
            s = s + cw_ref[t * steps + u:t * steps + u + 1, :] * vn[u * bs:(u + 1) * bs]
        o_ref[t * bs:(t + 1) * bs, :] = (uv_ref[t * bs:(t + 1) * bs, 0:w] * s).astype(o_ref.dtype)


def _gmlp(uv, ln_g, ln_b, ws, bst, cw, cb, layer, cfg):
    tp, ts, bs = cfg["tp"], cfg["ts"], cfg["bs"]
    w2 = uv.shape[1]
    w = w2 // 2
    groups = ws.shape[1]
    vec = lambda *a: pl.BlockSpec((None, 1, w), lambda *_: (layer, 0, 0))
    ob_p = pl.pallas_call(
        _gmlp_p_body, grid=(tp // TG,),
        in_specs=[pl.BlockSpec((TG, w2), lambda i: (i, 0)), vec(), vec(),
                  pl.BlockSpec((None, groups, GM_CHUNK, GM_CHUNK), lambda i: (layer, 0, 0, 0)),
                  pl.BlockSpec((None, GM_CHUNK, LANE), lambda i: (layer, 0, 0))],
        out_specs=pl.BlockSpec((TG, w), lambda i: (i, 0)),
        out_shape=jax.ShapeDtypeStruct((tp, w), BF16),
        compiler_params=_params(("arbitrary",)), name="gmlp_p")(uv, ln_g, ln_b, ws, bst)
    steps = ts // bs
    ob_s, vn_s = pl.pallas_call(
        functools.partial(_gmlp_s_body, bs=bs), grid=(1,),
        in_specs=[pl.BlockSpec((ts, w2), lambda i: (tp // ts, 0)), vec(), vec(),
                  pl.BlockSpec((None, steps * steps, w), lambda i: (layer, 0, 0)),
                  pl.BlockSpec((None, SUBLANE, w), lambda i: (layer, 0, 0))],
        out_specs=[pl.BlockSpec((ts, w), lambda i: (0, 0)), pl.BlockSpec((ts, w), lambda i: (0, 0))],
        out_shape=[jax.ShapeDtypeStruct((ts, w), BF16), jax.ShapeDtypeStruct((ts, w), F32)],
        compiler_params=_params(("arbitrary",)), name="gmlp_s")(uv, ln_g, ln_b, cw, cb)
    return ob_p, ob_s, vn_s


def _lru_gates(xc, wa_ref, ba, wx_ref, bx, lam):
    nb = wa_ref.shape[0]
    bw = xc.shape[1] // nb
    x16 = xc.astype(BF16)
    rs, gs = [], []
    for h in range(nb):
        xs = x16[:, h * bw:(h + 1) * bw]
        rs.append(jnp.dot(xs, wa_ref[h].astype(BF16), preferred_element_type=F32))
        gs.append(jnp.dot(xs, wx_ref[h].astype(BF16), preferred_element_type=F32))
    r = jax.nn.sigmoid(jnp.concatenate(rs, axis=1) + ba)
    ig = jax.nn.sigmoid(jnp.concatenate(gs, axis=1) + bx)
    log_a = -LRU_C * r * jax.nn.softplus(-lam)
    th = jnp.tanh(log_a)
    return jnp.exp(log_a), jnp.sqrt(-2.0 * th / (1.0 - th)), ig


def _lru_p_body(x_ref, prev_ref, y_ref, cw_ref, cb_ref, wa_ref, ba_ref, wx_ref, bx_ref, lam_ref,
                o_ref, hl_ref, buf_ref, a_ref, b_ref, h_ref, *, tps):
    te, w = x_ref.shape
    i = pl.program_id(0)
    first = i % tps == 0
    xc = _conv_rows(x_ref, prev_ref, buf_ref, cw_ref[...], first) + cb_ref[...]
    a, mult, ig = _lru_gates(xc, wa_ref, ba_ref[...], wx_ref, bx_ref[...], lam_ref[...])
    reset = first & (lax.broadcasted_iota(I32, (te, w), 0) == 0)
    a_ref[...] = jnp.where(reset, 0.0, a)
    b_ref[...] = jnp.where(reset, 1.0, mult) * ig * xc

    @pl.when(first)
    def _():
        h_ref[...] = jnp.zeros_like(h_ref)

    sub = lax.broadcasted_iota(I32, (SUBLANE, w), 0)

    def slab(s, h):
        rs = pl.ds(pl.multiple_of(s * SUBLANE, SUBLANE), SUBLANE)
        av, bv = a_ref[rs, :], b_ref[rs, :]
        for dlt in (1, 2, 4):
            ash, bsh = pltpu.roll(av, dlt, 0), pltpu.roll(bv, dlt, 0)
            m = sub >= dlt
            bv = jnp.where(m, av * bsh + bv, bv)
            av = jnp.where(m, av * ash, av)
        hs = av * h + bv
        b_ref[rs, :] = hs
        return hs[SUBLANE - 1:SUBLANE, :]

    h_last = lax.fori_loop(0, te // SUBLANE, slab, h_ref[...])
    h_ref[...] = h_last
    hl_ref[...] = h_last
    o_ref[...] = (y_ref[...] * b_ref[...]).astype(o_ref.dtype)


def _lru_s_body(x_ref, hist_ref, y_ref, h0_ref, cw_ref, cb_ref, wa_ref, ba_ref, wx_ref, bx_ref, lam_ref,
                o_ref, hl_ref, *, bs, pos0):
    ys = _conv_slabs(x_ref, hist_ref, cw_ref[...], bs)
    h = h0_ref[...]
    for t, y in enumerate(ys):
        xc = y + cb_ref[...]
        a, mult, ig = _lru_gates(xc, wa_ref, ba_ref[...], wx_ref, bx_ref[...], lam_ref[...])
        if pos0 + t == 0:
            a, mult = jnp.zeros_like(a), jnp.ones_like(mult)
        h = a * h + mult * ig * xc
        o_ref[t * bs:(t + 1) * bs, :] = (y_ref[t * bs:(t + 1) * bs, :] * h).astype(o_ref.dtype)
    hl_ref[...] = h


def _lru(lx, ly, hist_s, h0_s, p, layer, cfg):
    tp, ts, bs, bp, te = cfg["tp"], cfg["ts"], cfg["bs"], cfg["bp"], TE
    w = lx.shape[1]
    nb = p["lru_wa"].shape[1]
    bw = w // nb
    tps = cfg["seq"] // te
    rpb = te // SUBLANE
    vec = lambda *a: pl.BlockSpec((None, 1, w), lambda *_: (layer, 0, 0))
    cws = lambda *a: pl.BlockSpec((None, CONV_W, w), lambda *_: (layer, 0, 0))
    blk = lambda *a: pl.BlockSpec((None, nb, bw, bw), lambda *_: (layer, 0, 0, 0))
    wargs = (p["lru_conv_w"], p["lru_conv_b"], p["lru_wa"], p["lru_ba"], p["lru_wx"], p["lru_bx"],
             p["lru_lambda"])
    wspecs = [cws(), vec(), blk(), vec(), blk(), vec(), vec()]
    oc_p, hl_p = pl.pallas_call(
        functools.partial(_lru_p_body, tps=tps), grid=(tp // te,),
        in_specs=[pl.BlockSpec((te, w), lambda i: (i, 0)),
                  pl.BlockSpec((SUBLANE, w), lambda i: (jnp.maximum(i * rpb - 1, 0), 0)),
                  pl.BlockSpec((te, w), lambda i: (i, 0))] + wspecs,
        out_specs=[pl.BlockSpec((te, w), lambda i: (i, 0)),
                   pl.BlockSpec((None, 1, w), lambda i: (i // tps, 0, 0))],
        out_shape=[jax.ShapeDtypeStruct((tp, w), BF16), jax.ShapeDtypeStruct((bp, 1, w), F32)],
        scratch_shapes=[pltpu.VMEM((te + SUBLANE, w), F32), pltpu.VMEM((te, w), F32),
                        pltpu.VMEM((te, w), F32), pltpu.VMEM((1, w), F32)],
        compiler_params=_params(("arbitrary",)), name="lru_p")(lx, lx, ly, *wargs)
    oc_s, hl_s = pl.pallas_call(
        functools.partial(_lru_s_body, bs=bs, pos0=cfg["pos0_s"]), grid=(1,),
        in_specs=[pl.BlockSpec((ts, w), lambda i: (tp // ts, 0)),
                  pl.BlockSpec(((CONV_W - 1) * bs, w), lambda i: (0, 0)),
                  pl.BlockSpec((ts, w), lambda i: (tp // ts, 0)),
                  pl.BlockSpec((bs, w), lambda i: (0, 0))] + wspecs,
        out_specs=[pl.BlockSpec((ts, w), lambda i: (0, 0)), pl.BlockSpec((bs, w), lambda i: (0, 0))],
        out_shape=[jax.ShapeDtypeStruct((ts, w), BF16), jax.ShapeDtypeStruct((bs, w), F32)],
        compiler_params=_params(("arbitrary",)), name="lru_s")(lx, hist_s, ly, h0_s, *wargs)
    return oc_p, hl_p.reshape(bp, w), oc_s, hl_s


def _time_major(a):
    return jnp.swapaxes(a, 0, 1).reshape(a.shape[0] * a.shape[1], a.shape[2])


def _batch_major(a, bs):
    return jnp.swapaxes(a.reshape(a.shape[0] // bs, bs, a.shape[1]), 0, 1)


def _moe_plan(route, n_exp, tm):
    t_rows = route.shape[0]
    flat_e = route[:, 2:2 + TOP_K].astype(I32).reshape(-1)
    oh = (flat_e[:, None] == jnp.arange(n_exp, dtype=I32)[None, :]).astype(I32)
    csum = jnp.cumsum(oh, axis=0)
    rank = jnp.take_along_axis(csum, flat_e[:, None], axis=1)[:, 0] - 1
    counts = csum[-1]
    tiles_e = (counts + tm - 1) // tm
    tile_end = jnp.cumsum(tiles_e)
    dest = ((tile_end - tiles_e) * tm)[flat_e] + rank
    n_tiles = pl.cdiv(TOP_K * t_rows, tm) + n_exp
    src = jnp.zeros((n_tiles * tm,), I32).at[dest].set(jnp.arange(TOP_K * t_rows, dtype=I32) // TOP_K)
    tile_e = jnp.minimum(jnp.searchsorted(tile_end, jnp.arange(n_tiles, dtype=I32), side="right"),
                         n_exp - 1).astype(I32)
    dest = dest.reshape(t_rows, TOP_K)
    return dest[:, 0], dest[:, 1], src, tile_e, tile_end[-1].astype(I32)


def kernel(x_prompt, x_sample, state_dn_S, state_dn_conv, state_lru_h, state_lru_conv, c_prompt, c_sample,
           w_mod, b_mod, norm_g, w_in, dn_conv_w, dn_a_log, dn_dt_bias, dn_norm_g, gm_ln_g, gm_ln_b, gm_ws,
           gm_bs, lru_conv_w, lru_conv_b, lru_wa, lru_ba, lru_wx, lru_bx, lru_lambda, w_branch, w_out,
           ffn_w_gate, ffn_w_up, ffn_w_down, router_w, router_b, moe_w_gate, moe_w_up, moe_w_down):
    bp, seq, d = x_prompt.shape
    bs, dec = x_sample.shape[0], x_sample.shape[1]
    depth = w_mod.shape[0]
    w = d // 2
    heads = w // DN_HEAD_DIM
    n_exp = moe_w_gate.shape[1]
    tp, ts = bp * seq, bs * dec
    t_rows = tp + ts
    past_len = 16384
    assert dec <= SUBLANE and dec >= CONV_W - 1 and seq % GM_CHUNK == 0 and seq % TG == 0
    assert tp % ts == 0 and t_rows % TM == 0 and ts % TE == 0 and bs % SUBLANE == 0
    cfg = dict(bp=bp, bs=bs, seq=seq, tp=tp, ts=ts, pos0_s=past_len)

    n_t = t_rows // TM
    layer_meta = [jnp.concatenate([jnp.full((n_t,), l, I32), jnp.array([n_t], I32)]) for l in range(depth)]

    n_c = bp + bs
    n_c_pad = -(-n_c // SUBLANE) * SUBLANE
    c_all = jnp.concatenate([c_prompt, c_sample, jnp.zeros((n_c_pad - n_c, d), F32)], axis=0)
    b_mod3 = b_mod.reshape(depth, 1, N_MOD * d)
    mods = []
    for l in range(depth):
        meta = jnp.array([l, 1], I32)
        mods.append(_mm(c_all, w_mod, meta, name="mod", tm=n_c_pad, pre_act="silu", bias=b_mod3))
    mod = jnp.stack(mods).reshape(depth, n_c_pad, N_MOD, d)
    cfg["mod_p"] = mod[:, :bp].reshape(depth * bp, N_MOD, d)
    cfg["mod_s"] = jnp.swapaxes(mod[:, bp:n_c], 1, 2).reshape(depth * N_MOD, bs, d)
    cfg["norm_g"] = norm_g.reshape(depth * 4, 1, d)

    n_qkv, n_z = 3 * w, w
    c_ba = n_qkv + n_z
    c_rest = c_ba + 2 * heads
    w_ba = jnp.pad(w_in[:, :, c_ba:c_rest], ((0, 0), (0, 0), (0, LANE - 2 * heads)))
    w_rest = w_in[:, :, c_rest:].astype(BF16)
    lane_pad = lambda a, off: jnp.pad(a, ((0, 0), (off, LANE - off - a.shape[1]))).reshape(depth, 1, LANE)
    al_pad, dt_pad = lane_pad(dn_a_log, heads), lane_pad(dn_dt_bias, heads)
    dn_ng = dn_norm_g.reshape(depth, 1, DN_HEAD_DIM)
    vec3 = lambda a: a.reshape(depth, 1, a.shape[-1])
    lru_p = dict(lru_conv_w=lru_conv_w, lru_conv_b=vec3(lru_conv_b), lru_wa=lru_wa, lru_ba=vec3(lru_ba),
                 lru_wx=lru_wx, lru_bx=vec3(lru_bx), lru_lambda=vec3(lru_lambda))
    gm_g3, gm_b3 = vec3(gm_ln_g), vec3(gm_ln_b)
    gm_bst = jnp.pad(jnp.swapaxes(gm_bs, 1, 2), ((0, 0), (0, 0), (0, LANE - gm_bs.shape[1])))
    gw = w // gm_ws.shape[1]
    gm_cw = jnp.repeat(jnp.swapaxes(gm_ws[:, :, :dec, :dec], 1, 3).reshape(depth, dec, dec, -1), gw, axis=-1)
    gm_cw = jnp.swapaxes(gm_cw, 1, 2).reshape(depth, dec * dec, w)
    gm_cb = jnp.repeat(jnp.swapaxes(gm_bs[:, :, :dec], 1, 2), gw, axis=-1)
    gm_cb = jnp.pad(gm_cb, ((0, 0), (0, SUBLANE - dec), (0, 0)))
    wb3 = w_branch.reshape(depth * 3, w, d)
    n_moe = router_w.shape[0]
    rw_pad = jnp.pad(router_w, ((0, 0), (0, 0), (0, LANE - n_exp)))
    rb_pad = jnp.pad(router_b, ((0, 0), (0, LANE - n_exp)), constant_values=NEG_BIG).reshape(n_moe, 1, LANE)
    d_ff = ffn_w_gate.shape[2]
    moe_wg = moe_w_gate.reshape(n_moe * n_exp, d, d_ff)
    moe_wu = moe_w_up.reshape(n_moe * n_exp, d, d_ff)
    moe_wd = moe_w_down.reshape(n_moe * n_exp, d_ff, d)

    x = jnp.concatenate([x_prompt.reshape(tp, d), _time_major(x_sample)], axis=0)
    s0_p = jnp.zeros((bp, heads, DN_HEAD_DIM, DN_HEAD_DIM), F32)
    outs = {k: [] for k in ("S_p", "dc_p", "h_p", "lc_p", "S_s", "dc_s", "h_s", "lc_s", "v_s")}

    (hh,) = _post(x, None, cfg, res=None, nxt=(0, 0, 1, 0), name="pre0")
    for l in range(depth):
        meta = layer_meta[l]
        qkv = _mm(hh, w_in, meta, name="in_qkv", n=n_qkv)
        zs = _mm(hh, w_in, meta, name="in_z", n_off=n_qkv // TN, n=n_z, act="silu")
        ba = _mm(hh, w_ba, meta, name="in_ba", tn=LANE)
        uv = _mm(hh, w_rest, meta, name="in_uv", n=2 * w, act="gelu")
        lx = _mm(hh, w_rest, meta, name="in_lx", n_off=2 * w // TN, n=w)
        ly = _mm(hh, w_rest, meta, name="in_ly", n_off=3 * w // TN, n=w, act="gelu")
        gates = _mm(hh, w_rest, meta, name="in_merge", n_off=4 * w // TN, n=3 * d, act="sigmoid")

        dn_hist = _time_major(state_dn_conv[l])
        qkv_p, qkv_s = _dn_prep(qkv, dn_hist, dn_conv_w, l, cfg)
        oa_p, s_p = _dn_chunks(qkv_p, zs, ba, s0_p, al_pad, dt_pad, dn_ng, l, groups=1,
                               n_valid=DN_CHUNK, n_chunks=seq // DN_CHUNK, name="dn_chunk_p")

        def to_blocks(a):
            a = jnp.pad(_batch_major(a, bs), ((0, 0), (0, SUBLANE - dec), (0, 0)))
            return a.reshape(bs * SUBLANE, a.shape[2])

        gs = DN_CHUNK // SUBLANE
        oa_sb, s_s = _dn_chunks(to_blocks(qkv_s), to_blocks(zs[tp:]), to_blocks(ba[tp:]), state_dn_S[l],
                                al_pad, dt_pad, dn_ng, l, groups=gs, n_valid=dec, n_chunks=1,
                                name="dn_chunk_s")
        oa_s = _time_major(oa_sb.reshape(bs, SUBLANE, w)[:, :dec])
        oa = jnp.concatenate([oa_p, oa_s], axis=0)
        dc_hist = jnp.concatenate([dn_hist, qkv[tp:]], axis=0)

        ob_p, ob_s, vn_s = _gmlp(uv, gm_g3, gm_b3, gm_ws, gm_bst, gm_cw, gm_cb, l, cfg)
        ob = jnp.concatenate([ob_p, ob_s], axis=0)
        lru_hist = _time_major(state_lru_conv[l])
        oc_p, hl_p, oc_s, hl_s = _lru(lx, ly, lru_hist, state_lru_h[l], lru_p, l, cfg)
        oc = jnp.concatenate([oc_p, oc_s], axis=0)
        lc_hist = jnp.concatenate([lru_hist, lx[tp:]], axis=0)

        outs["S_p"].append(s_p)
        outs["dc_p"].append(qkv[:tp].reshape(bp, seq, n_qkv)[:, seq - (CONV_W - 1):])
        outs["h_p"].append(hl_p)
        outs["lc_p"].append(lx[:tp].reshape(bp, seq, w)[:, seq - (CONV_W - 1):])
        outs["S_s"].append(s_s)
        outs["dc_s"].append(_batch_major(dc_hist[dec * bs:], bs))
        outs["h_s"].append(hl_s)
        outs["lc_s"].append(_batch_major(lc_hist[dec * bs:], bs))
        outs["v_s"].append(_batch_major(vn_s, bs))

        m = _branch_mm(oa, ob, oc, gates, wb3, meta, name="branch")
        y = _mm(m, w_out, meta, name="w_out")
        j = l // 2
        if l % 2 == 0:
            x, hh = _post(x, y, cfg, res=(l, 1, 2), nxt=(l, 2, 4, 3), name="post_mix")
            hid = _ffn_up(hh, ffn_w_gate, ffn_w_up, jnp.concatenate(
                [jnp.full((n_t,), j, I32), jnp.array([n_t], I32)]), name="ffn_up")
            y = _mm(hid, ffn_w_down, jnp.concatenate(
                [jnp.full((n_t,), j, I32), jnp.array([n_t], I32)]), name="ffn_down")
        else:
            x, hh32, route = _post(x, y, cfg, res=(l, 1, 2), nxt=(l, 2, 4, 3), router=(rw_pad, rb_pad, j),
                                   hh_dtype=F32, name="post_mix_r")
            d0, d1, src, tile_e, nused = _moe_plan(route, n_exp, TM)
            xs = _dispatch(hh32, src, nused.reshape(1), tm=TM, name="moe_dispatch")
            emeta = jnp.concatenate([j * n_exp + tile_e, nused.reshape(1)])
            hid = _ffn_up(xs, moe_wg, moe_wu, emeta, name="moe_up")
            ys = _mm(hid, moe_wd, emeta, name="moe_down")
            y = _combine(ys, d0, d1, route, name="moe_combine")
        if l + 1 < depth:
            x, hh = _post(x, y, cfg, res=(l, 3, 5), nxt=(l + 1, 0, 1, 0), name="post_ffn")
        else:
            (x,) = _post(x, y, cfg, res=(l, 3, 5), nxt=None, name="post_last")

    st = lambda k: jnp.stack(outs[k])
    return (x[:tp].reshape(bp, seq, d), _batch_major(x[tp:], bs),
            st("S_p"), st("dc_p"), st("h_p"), st("lc_p"),
            st("S_s"), st("dc_s"), st("h_s"), st("lc_s"), st("v_s"))
```

```python
import functools

import jax
import jax.numpy as jnp
from jax import lax
from jax.experimental import pallas as pl
from jax.experimental.pallas import tpu as pltpu

F32, BF16, I32 = jnp.float32, jnp.bfloat16, jnp.int32
HI = lax.Precision.HIGHEST

LANE = 128
SUBLANE = 8
VMEM_LIMIT = 56 << 20
DN_HEAD_DIM = 128
DN_CHUNK = 64
CONV_W = 4
GM_CHUNK = 128
LRU_BLOCKS = 8
LRU_C = 8.0
N_MOD = 6
TOP_K = 2
RMS_EPS = 1e-6
LN_EPS = 1e-5
L2_EPS = 1e-6
NEG_BIG = -1e30

TM = 512
TN = 512
TN_WIDE = 1024
DENSE_ROW_TILES = 8
TE = 256


def _params(sem):
    return pltpu.CompilerParams(dimension_semantics=sem, vmem_limit_bytes=VMEM_LIMIT)


def _act(name, x):
    if name is None:
        return x
    if name == "silu":
        return x * jax.nn.sigmoid(x)
    if name == "gelu":
        return jax.nn.gelu(x)
    if name == "sigmoid":
        return jax.nn.sigmoid(x)
    raise ValueError(name)


def _rms(x, g):
    return x * lax.rsqrt(jnp.mean(x * x, axis=-1, keepdims=True) + RMS_EPS) * g


def _cast_once(meta_ref, i, pairs):
    prev = meta_ref[jnp.maximum(i - 1, 0)]

    @pl.when((i == 0) | (meta_ref[i] != prev))
    def _():
        for src, dst in pairs:
            dst[...] = src[...].astype(BF16)


def _mm_body(meta_ref, x_ref, w_ref, *rest, ni, act, pre_act, cast_w, has_bias, w_nk):
    rest = list(rest)
    b_ref = rest.pop(0) if has_bias else None
    o_ref = rest.pop(0)
    wbf_ref = rest.pop(0) if cast_w else None
    i = pl.program_id(1)
    nused = meta_ref[ni]

    @pl.when(i < nused)
    def _():
        if cast_w:
            _cast_once(meta_ref, i, [(w_ref, wbf_ref)])
            w = wbf_ref[...]
        else:
            w = w_ref[...]
        w = w[0] if w_nk else w
        x = x_ref[...]
        if pre_act is not None:
            x = _act(pre_act, x.astype(F32))
        dims = (((1,), (1,)), ((), ())) if w_nk else (((1,), (0,)), ((), ()))
        acc = lax.dot_general(x.astype(BF16), w, dims, preferred_element_type=F32)
        if has_bias:
            acc = acc + b_ref[...]
        o_ref[...] = _act(act, acc).astype(o_ref.dtype)

    @pl.when(i >= nused)
    def _():
        o_ref[...] = jnp.zeros_like(o_ref)


def _mm(x, w, meta, *, name, n_off=0, n=None, tm=TM, tn=TN, act=None, pre_act=None,
        bias=None, out_dtype=F32, w_nk=False):
    m_rows, k = x.shape
    n = w.shape[1 if w_nk else 2] if n is None else n
    tm = min(tm, m_rows)
    ni, nj = pl.cdiv(m_rows, tm), n // tn
    assert n % tn == 0 and meta.shape[0] == ni + 1
    cast_w = w.dtype != BF16
    body = functools.partial(_mm_body, ni=ni, act=act, pre_act=pre_act, cast_w=cast_w,
                             has_bias=bias is not None, w_nk=w_nk)
    if w_nk:
        assert n_off % SUBLANE == 0 and tn % SUBLANE == 0
        wspec = pl.BlockSpec((pl.Element(1), pl.Element(tn), pl.Element(k)),
                             lambda j, i, m: (m[i], (n_off // SUBLANE + j * (tn // SUBLANE)) * SUBLANE, 0))
    else:
        wspec = pl.BlockSpec((None, k, tn), lambda j, i, m: (m[i], 0, j + n_off))
    in_specs = [pl.BlockSpec((tm, k), lambda j, i, m: (jnp.minimum(i, m[ni] - 1), 0)), wspec]
    args = [x, w]
    if bias is not None:
        in_specs.append(pl.BlockSpec((None, 1, tn), lambda j, i, m: (m[i], 0, j + n_off)))
        args.append(bias)
    scratch = [pltpu.VMEM((1, tn, k) if w_nk else (k, tn), BF16)] if cast_w else []
    return pl.pallas_call(
        body,
        grid_spec=pltpu.PrefetchScalarGridSpec(
            num_scalar_prefetch=1, grid=(nj, ni), in_specs=in_specs,
            out_specs=pl.BlockSpec((tm, tn), lambda j, i, m: (i, j)), scratch_shapes=scratch),
        out_shape=jax.ShapeDtypeStruct((m_rows, n), out_dtype),
        compiler_params=_params(("arbitrary", "arbitrary")), name=name)(meta, *args)


def _ffn_up_body(meta_ref, x_ref, wg_ref, wu_ref, o_ref, wgb_ref, wub_ref, *, ni):
    i = pl.program_id(1)
    nused = meta_ref[ni]

    @pl.when(i < nused)
    def _():
        _cast_once(meta_ref, i, [(wg_ref, wgb_ref), (wu_ref, wub_ref)])
        x = x_ref[...]
        g = jnp.dot(x, wgb_ref[...], preferred_element_type=F32)
        u = jnp.dot(x, wub_ref[...], preferred_element_type=F32)
        o_ref[...] = (g * jax.nn.sigmoid(g) * u).astype(o_ref.dtype)

    @pl.when(i >= nused)
    def _():
        o_ref[...] = jnp.zeros_like(o_ref)


def _ffn_up(x, wg, wu, meta, *, name, tm=TM, tn=TN):
    m_rows, k = x.shape
    n = wg.shape[2]
    ni, nj = m_rows // tm, n // tn
    assert m_rows % tm == 0 and n % tn == 0 and meta.shape[0] == ni + 1
    wspec = pl.BlockSpec((None, k, tn), lambda j, i, m: (m[i], 0, j))
    return pl.pallas_call(
        functools.partial(_ffn_up_body, ni=ni),
        grid_spec=pltpu.PrefetchScalarGridSpec(
            num_scalar_prefetch=1, grid=(nj, ni),
            in_specs=[pl.BlockSpec((tm, k), lambda j, i, m: (jnp.minimum(i, m[ni] - 1), 0)),
                      wspec, wspec],
            out_specs=pl.BlockSpec((tm, tn), lambda j, i, m: (i, j)),
            scratch_shapes=[pltpu.VMEM((k, tn), BF16), pltpu.VMEM((k, tn), BF16)]),
        out_shape=jax.ShapeDtypeStruct((m_rows, n), BF16),
        compiler_params=_params(("arbitrary", "arbitrary")), name=name)(meta, x, wg, wu)


def _branch_body(meta_ref, xa_ref, xb_ref, xc_ref, ga_ref, gb_ref, gc_ref,
                 w0_ref, w1_ref, w2_ref, o_ref, s0_ref, s1_ref, s2_ref):
    i = pl.program_id(1)
    _cast_once(meta_ref, i, [(w0_ref, s0_ref), (w1_ref, s1_ref), (w2_ref, s2_ref)])
    acc = ga_ref[...] * jnp.dot(xa_ref[...], s0_ref[...], preferred_element_type=F32)
    acc = acc + gb_ref[...] * jnp.dot(xb_ref[...], s1_ref[...], preferred_element_type=F32)
    acc = acc + gc_ref[...] * jnp.dot(xc_ref[...], s2_ref[...], preferred_element_type=F32)
    o_ref[...] = acc.astype(o_ref.dtype)


def _branch_mm(oa, ob, oc, gates, wb, meta, *, name, tm=TM, tn=TN):
    m_rows, k = oa.shape
    n = wb.shape[2]
    ni, nj = m_rows // tm, n // tn
    xspec = pl.BlockSpec((tm, k), lambda j, i, m: (i, 0))

    def gspec(b):
        return pl.BlockSpec((tm, tn), lambda j, i, m: (i, j + b * nj))

    def wspec(b):
        return pl.BlockSpec((None, k, tn), lambda j, i, m: (m[i] * 3 + b, 0, j))

    return pl.pallas_call(
        _branch_body,
        grid_spec=pltpu.PrefetchScalarGridSpec(
            num_scalar_prefetch=1, grid=(nj, ni),
            in_specs=[xspec, xspec, xspec, gspec(0), gspec(1), gspec(2), wspec(0), wspec(1), wspec(2)],
            out_specs=pl.BlockSpec((tm, tn), lambda j, i, m: (i, j)),
            scratch_shapes=[pltpu.VMEM((k, tn), BF16)] * 3),
        out_shape=jax.ShapeDtypeStruct((m_rows, n), BF16),
        compiler_params=_params(("arbitrary", "arbitrary")), name=name)(
            meta, oa, ob, oc, gates, gates, gates, wb, wb, wb)


def _route(hh, rw_ref, rb_ref):
    lg = jnp.dot(hh, rw_ref[...], precision=HI, preferred_element_type=F32) + rb_ref[...]
    lane = lax.broadcasted_iota(I32, lg.shape, 1)
    m1 = jnp.max(lg, axis=1, keepdims=True)
    i1 = jnp.min(jnp.where(lg == m1, lane, LANE), axis=1, keepdims=True)
    lg2 = jnp.where(lane == i1, NEG_BIG * 2, lg)
    m2 = jnp.max(lg2, axis=1, keepdims=True)
    i2 = jnp.min(jnp.where(lg2 == m2, lane, LANE), axis=1, keepdims=True)
    e = jnp.exp(m2 - m1)
    p1 = 1.0 / (1.0 + e)
    p2 = e / (1.0 + e)
    return jnp.where(lane == 0, p1, jnp.where(lane == 1, p2, jnp.where(
        lane == 2, i1.astype(F32), jnp.where(lane == 3, i2.astype(F32), 0.0))))


def _post_body(*refs, npt, reps, res, nxt, router):
    it = iter(refs)
    x_ref = next(it)
    y_ref = next(it) if res else None
    gres_ref = next(it) if res else None
    gnx_ref = next(it) if nxt else None
    mpr_ref = next(it) if res else None
    mpn_ref = next(it) if nxt else None
    msg_ref = next(it) if res else None
    msc_ref = next(it) if nxt else None
    msh_ref = next(it) if nxt else None
    rw_ref = next(it) if router else None
    rb_ref = next(it) if router else None
    xo_ref = next(it) if res else None
    hh_ref = next(it) if nxt else None
    rt_ref = next(it) if router else None
    i = pl.program_id(0)

    def compute(gate, sc, sh):
        x = x_ref[...]
        if res:
            x = x + gate * _rms(y_ref[...], gres_ref[...])
            xo_ref[...] = x
        if nxt:
            hh = _rms(x, gnx_ref[...]) * (1.0 + sc) + sh
            hh_ref[...] = hh.astype(hh_ref.dtype)
            if router:
                rt_ref[...] = _route(hh, rw_ref, rb_ref)

    def rep(r):
        return jnp.concatenate([r[...]] * reps, axis=0) if reps > 1 else r[...]

    @pl.when(i < npt)
    def _():
        compute(mpr_ref[res[2]:res[2] + 1, :] if res else None,
                mpn_ref[nxt[2]:nxt[2] + 1, :] if nxt else None,
                mpn_ref[nxt[3]:nxt[3] + 1, :] if nxt else None)

    @pl.when(i >= npt)
    def _():
        compute(rep(msg_ref) if res else None, rep(msc_ref) if nxt else None,
                rep(msh_ref) if nxt else None)


def _post(x, y, cfg, *, res, nxt, router=None, hh_dtype=BF16, name):
    t_rows, d = x.shape
    te, bp, bs = TE, cfg["bp"], cfg["bs"]
    npt, nt, tpb = cfg["tp"] // te, t_rows // te, cfg["seq"] // te
    assert te % bs == 0 and cfg["seq"] % te == 0 and t_rows % te == 0
    row = pl.BlockSpec((te, d), lambda i: (i, 0))
    in_specs, args = [row], [x]
    if res:
        in_specs.append(row)
        args.append(y)

    def gspec(layer, r):
        return pl.BlockSpec((None, 1, d), lambda i: (layer * 4 + r, 0, 0))

    def mpspec(layer):
        return pl.BlockSpec((None, N_MOD, d), lambda i: (layer * bp + jnp.minimum(i // tpb, bp - 1), 0, 0))

    def msspec(layer, plane):
        return pl.BlockSpec((None, bs, d), lambda i: (layer * N_MOD + plane, 0, 0))

    if res:
        in_specs.append(gspec(res[0], res[1])); args.append(cfg["norm_g"])
    if nxt:
        in_specs.append(gspec(nxt[0], nxt[1])); args.append(cfg["norm_g"])
    if res:
        in_specs.append(mpspec(res[0])); args.append(cfg["mod_p"])
    if nxt:
        in_specs.append(mpspec(nxt[0])); args.append(cfg["mod_p"])
    if res:
        in_specs.append(msspec(res[0], res[2])); args.append(cfg["mod_s"])
    if nxt:
        in_specs.append(msspec(nxt[0], nxt[2])); args.append(cfg["mod_s"])
        in_specs.append(msspec(nxt[0], nxt[3])); args.append(cfg["mod_s"])
    if router:
        rw, rb, slab = router
        in_specs.append(pl.BlockSpec((None, d, LANE), lambda i: (slab, 0, 0))); args.append(rw)
        in_specs.append(pl.BlockSpec((None, 1, LANE), lambda i: (slab, 0, 0))); args.append(rb)
    out_specs, out_shape = [], []
    if res:
        out_specs.append(row); out_shape.append(jax.ShapeDtypeStruct((t_rows, d), F32))
    if nxt:
        out_specs.append(row); out_shape.append(jax.ShapeDtypeStruct((t_rows, d), hh_dtype))
    if router:
        out_specs.append(pl.BlockSpec((te, LANE), lambda i: (i, 0)))
        out_shape.append(jax.ShapeDtypeStruct((t_rows, LANE), F32))
    body = functools.partial(_post_body, npt=npt, reps=te // bs, res=res, nxt=nxt, router=bool(router))
    return pl.pallas_call(body, grid=(nt,), in_specs=in_specs, out_specs=out_specs, out_shape=out_shape,
                          compiler_params=_params(("arbitrary",)), name=name)(*args)


def _row_copy(src_hbm, row, dst_ref, r, sem):
    return pltpu.make_async_copy(src_hbm.at[pl.ds(row, 1), :], dst_ref.at[pl.ds(r, 1), :], sem)


def _gather_rows(idx_ref, base, src_hbm, dst_ref, sem, n):
    def start(r, c):
        _row_copy(src_hbm, idx_ref[base + r], dst_ref, r, sem).start()
        return c

    def wait(r, c):
        _row_copy(src_hbm, 0, dst_ref, r, sem).wait()
        return c

    lax.fori_loop(0, n, start, 0)
    lax.fori_loop(0, n, wait, 0)


def _dispatch_body(src_ref, nused_ref, x_hbm, o_ref, buf_ref, sem, *, tm):
    i = pl.program_id(0)

    @pl.when(i < nused_ref[0])
    def _():
        _gather_rows(src_ref, i * tm, x_hbm, buf_ref, sem, tm)
        o_ref[...] = buf_ref[...].astype(o_ref.dtype)

    @pl.when(i >= nused_ref[0])
    def _():
        o_ref[...] = jnp.zeros_like(o_ref)


def _dispatch(hh, src, nused, *, tm, name):
    cap = src.shape[0]
    d = hh.shape[1]
    return pl.pallas_call(
        functools.partial(_dispatch_body, tm=tm),
        grid_spec=pltpu.PrefetchScalarGridSpec(
            num_scalar_prefetch=2, grid=(cap // tm,),
            in_specs=[pl.BlockSpec(memory_space=pl.ANY)],
            out_specs=pl.BlockSpec((tm, d), lambda i, s, n: (i, 0)),
            scratch_shapes=[pltpu.VMEM((tm, d), hh.dtype), pltpu.SemaphoreType.DMA(())]),
        out_shape=jax.ShapeDtypeStruct((cap, d), BF16),
        compiler_params=_params(("arbitrary",)), name=name)(src, nused, hh)


def _combine_body(d0_ref, d1_ref, y_hbm, p_ref, o_ref, a_ref, b_ref, sem_a, sem_b, *, te):
    i = pl.program_id(0)

    def start(r, c):
        _row_copy(y_hbm, d0_ref[i * te + r], a_ref, r, sem_a).start()
        _row_copy(y_hbm, d1_ref[i * te + r], b_ref, r, sem_b).start()
        return c

    def wait(r, c):
        _row_copy(y_hbm, 0, a_ref, r, sem_a).wait()
        _row_copy(y_hbm, 0, b_ref, r, sem_b).wait()
        return c

    lax.fori_loop(0, te, start, 0)
    lax.fori_loop(0, te, wait, 0)
    p = p_ref[...]
    o_ref[...] = p[:, 0:1] * a_ref[...] + p[:, 1:2] * b_ref[...]


def _combine(ys, dest0, dest1, route, *, name):
    t_rows = route.shape[0]
    d = ys.shape[1]
    te = TE
    return pl.pallas_call(
        functools.partial(_combine_body, te=te),
        grid_spec=pltpu.PrefetchScalarGridSpec(
            num_scalar_prefetch=2, grid=(t_rows // te,),
            in_specs=[pl.BlockSpec(memory_space=pl.ANY),
                      pl.BlockSpec((te, LANE), lambda i, a, b: (i, 0))],
            out_specs=pl.BlockSpec((te, d), lambda i, a, b: (i, 0)),
            scratch_shapes=[pltpu.VMEM((te, d), F32), pltpu.VMEM((te, d), F32),
                            pltpu.SemaphoreType.DMA(()), pltpu.SemaphoreType.DMA(())]),
        out_shape=jax.ShapeDtypeStruct((t_rows, d), F32),
        compiler_params=_params(("arbitrary",)), name=name)(dest0, dest1, ys, route)


def _dn_post(y, j):
    y = y * jax.nn.sigmoid(y)
    outs = []
    for h in range(y.shape[1] // DN_HEAD_DIM):
        yh = y[:, h * DN_HEAD_DIM:(h + 1) * DN_HEAD_DIM]
        n = lax.rsqrt(jnp.sum(yh * yh, axis=-1, keepdims=True) + L2_EPS)
        s = jnp.where(j == 0, n * (DN_HEAD_DIM ** -0.5), jnp.where(j == 1, n, 1.0))
        outs.append(yh * s)
    return jnp.concatenate(outs, axis=1)


def _conv_rows(x_ref, prev_ref, buf_ref, w, first):
    te = x_ref.shape[0]
    x = x_ref[...]
    buf_ref[0:SUBLANE, :] = jnp.where(first, 0.0, prev_ref[...])
    buf_ref[SUBLANE:, :] = x
    y = w[CONV_W - 1:CONV_W, :] * x
    for dlt in range(1, CONV_W):
        y = y + w[CONV_W - 1 - dlt:CONV_W - dlt, :] * buf_ref[SUBLANE - dlt:SUBLANE - dlt + te, :]
    return y


def _conv_slabs(x_ref, hist_ref, w, bs):
    xp = jnp.concatenate([hist_ref[...], x_ref[...]], axis=0)
    steps = x_ref.shape[0] // bs
    ys = []
    for t in range(steps):
        y = w[0:1, :] * xp[t * bs:(t + 1) * bs]
        for j in range(1, CONV_W):
            y = y + w[j:j + 1, :] * xp[(t + j) * bs:(t + j + 1) * bs]
        ys.append(y)
    return ys


def _dn_prep_body(x_ref, prev_ref, hist_ref, w_ref, o_ref, buf_ref, *, npt, tps, bs):
    i, j = pl.program_id(0), pl.program_id(1)

    @pl.when(i < npt)
    def _():
        y = _conv_rows(x_ref, prev_ref, buf_ref, w_ref[...], i % tps == 0)
        o_ref[...] = _dn_post(y, j)

    @pl.when(i >= npt)
    def _():
        ys = _conv_slabs(x_ref, hist_ref, w_ref[...], bs)
        o_ref[...] = _dn_post(jnp.concatenate(ys, axis=0), j)


def _dn_prep(qkv, hist_s, conv_w, layer, cfg):
    t_rows, tl, bs = cfg["tp"] + cfg["ts"], cfg["ts"], cfg["bs"]
    w3 = qkv.shape[1]
    cb = w3 // 3
    rpb = tl // SUBLANE
    return pl.pallas_call(
        functools.partial(_dn_prep_body, npt=cfg["tp"] // tl, tps=cfg["seq"] // tl, bs=bs),
        grid=(t_rows // tl, 3),
        in_specs=[pl.BlockSpec((tl, cb), lambda i, j: (i, j)),
                  pl.BlockSpec((SUBLANE, cb), lambda i, j: (jnp.maximum(i * rpb - 1, 0), j)),
                  pl.BlockSpec(((CONV_W - 1) * bs, cb), lambda i, j: (0, j)),
                  pl.BlockSpec((None, CONV_W, cb), lambda i, j: (layer, 0, j))],
        out_specs=pl.BlockSpec((tl, cb), lambda i, j: (i, j)),
        out_shape=jax.ShapeDtypeStruct((t_rows, w3), F32),
        scratch_shapes=[pltpu.VMEM((tl + SUBLANE, cb), F32)],
        compiler_params=_params(("arbitrary", "arbitrary")), name="dn_prep")(qkv, qkv, hist_s, conv_w)


def _dotf(a, b):
    return jnp.dot(a, b, precision=HI, preferred_element_type=F32)


def _dotb(a, b):
    return jnp.dot(a.astype(BF16), b.astype(BF16), preferred_element_type=F32)


def _dot_nt(a, b):
    return lax.dot_general(a, b, (((1,), (1,)), ((), ())), preferred_element_type=F32)


def _dot_tn(a, b):
    return lax.dot_general(a, b, (((0,), (0,)), ((), ())), preferred_element_type=F32)


def _inv_unit_lower(lm, rpg):
    c = lm.shape[0]
    row = lax.broadcasted_iota(I32, (c, c), 0)
    col = lax.broadcasted_iota(I32, (c, c), 1)
    eye = (row == col).astype(F32)
    blk = (row // SUBLANE) == (col // SUBLANE)
    ld = jnp.where(blk, lm, 0.0)
    l2 = _dotb(ld, ld)
    l4 = _dotb(l2, l2)
    t1 = _dotb(_dotb(eye - ld, eye + l2), eye + l4)
    if rpg <= SUBLANE:
        return t1
    n1 = _dotb(t1, lm - ld)
    n2 = _dotb(n1, n1)
    n4 = _dotb(n2, n2)
    minv = _dotb(_dotb(eye - n1, eye + n2), eye + n4)
    return _dotb(minv, t1)


def _dn_gates(ba, al, dt, heads, valid):
    lanel = lax.broadcasted_iota(I32, ba.shape, 1)
    beta = jnp.where(valid & (lanel < heads), jax.nn.sigmoid(ba), 0.0)
    g = jnp.where(valid, -jnp.exp(al) * jax.nn.softplus(ba + dt), 0.0)
    return beta, g


def _dn_wy_body(q_ref, k_ref, v_ref, ba_ref, al_ref, dt_ref,
                w_ref, u_ref, qg_ref, kd_ref, sc_ref, eg_ref, m_ref, r_ref):
    c = DN_CHUNK
    rows_n = q_ref.shape[0]
    heads = q_ref.shape[1] // DN_HEAD_DIM
    p2 = 2 * c
    beta_all, g_all = _dn_gates(ba_ref[...], al_ref[...], dt_ref[...], heads, True)
    rr = lax.broadcasted_iota(I32, (rows_n, rows_n), 0)
    cc = lax.broadcasted_iota(I32, (rows_n, rows_n), 1)
    gcum_all = _dotf(((rr >= cc) & (rr // c == cc // c)).astype(F32), g_all)
    row = lax.broadcasted_iota(I32, (p2, p2), 0)
    col = lax.broadcasted_iota(I32, (p2, p2), 1)
    same = (row // c) == (col // c)
    tri = (row >= col) & same
    strict = (row > col) & same

    eye = (row == col).astype(F32)
    blk = (row // SUBLANE) == (col // SUBLANE)

    def stack(ref, rs, h):
        return jnp.concatenate([ref[rs, (h + a) * DN_HEAD_DIM:(h + a + 1) * DN_HEAD_DIM] for a in (0, 1)], axis=0)

    def cols(arr, rs, lane0):
        return jnp.concatenate([arr[rs, lane0 + a:lane0 + a + 1] for a in (0, 1)], axis=0)

    probs = [(ci, h) for ci in range(rows_n // c) for h in range(0, heads, 2)]
    n_p = len(probs)

    for p, (ci, h) in enumerate(probs):
        rs = slice(ci * c, (ci + 1) * c)
        q, k = stack(q_ref, rs, h), stack(k_ref, rs, h)
        beta = cols(beta_all, rs, h)
        gcol = cols(gcum_all, rs, heads + h)
        glast = jnp.concatenate(
            [jnp.broadcast_to(gcum_all[(ci + 1) * c - 1:(ci + 1) * c, heads + h + a:heads + h + a + 1], (c, 1))
             for a in (0, 1)], axis=0)
        gmat = jnp.broadcast_to(gcol, (p2, p2))
        decay = jnp.exp(jnp.where(tri, gmat - gmat.T, NEG_BIG))
        kb = k * beta
        k16 = k.astype(BF16)
        lm = jnp.where(strict, _dot_nt(kb.astype(BF16), k16) * decay, 0.0)
        m_ref[0, p] = lm
        m_ref[1, p] = jnp.where(blk, lm, 0.0)
        sc = _dot_nt(q.astype(BF16), k16) * decay
        egc = jnp.exp(gcol)
        r_ref[0, p] = (kb * egc).astype(BF16)
        r_ref[1, p] = (stack(v_ref, rs, h) * beta).astype(BF16)
        qg = q * egc
        kd = k * jnp.exp(glast - gcol)
        egl = jnp.exp(glast)
        for a in (0, 1):
            hs = slice(a * c, (a + 1) * c)
            qg_ref[ci, h + a] = qg[hs].astype(qg_ref.dtype)
            kd_ref[ci, h + a] = kd[hs].astype(kd_ref.dtype)
            sc_ref[ci, h + a] = sc[hs, hs].astype(sc_ref.dtype)
            eg_ref[ci, h + a] = jnp.broadcast_to(egl[a * c:a * c + 1], (1, LANE))

    def stage(dst, fa, fb):
        for p in range(n_p):
            m_ref[dst, p] = _dotb(fa(p), fb(p))

    ld = lambda p: m_ref[1, p]
    stage(2, ld, ld)
    stage(3, lambda p: m_ref[2, p], lambda p: m_ref[2, p])
    stage(4, lambda p: eye - ld(p), lambda p: eye + m_ref[2, p])
    stage(5, lambda p: m_ref[4, p], lambda p: eye + m_ref[3, p])
    stage(2, lambda p: m_ref[5, p], lambda p: m_ref[0, p] - ld(p))
    stage(3, lambda p: m_ref[2, p], lambda p: m_ref[2, p])
    stage(4, lambda p: m_ref[3, p], lambda p: m_ref[3, p])
    stage(6, lambda p: eye - m_ref[2, p], lambda p: eye + m_ref[3, p])
    stage(7, lambda p: m_ref[6, p], lambda p: eye + m_ref[4, p])
    stage(2, lambda p: m_ref[7, p], lambda p: m_ref[5, p])
    for p, (ci, h) in enumerate(probs):
        t16 = m_ref[2, p].astype(BF16)
        wm = jnp.dot(t16, r_ref[0, p], preferred_element_type=F32)
        um = jnp.dot(t16, r_ref[1, p], preferred_element_type=F32)
        for a in (0, 1):
            hs = slice(a * c, (a + 1) * c)
            w_ref[ci, h + a] = wm[hs].astype(w_ref.dtype)
            u_ref[ci, h + a] = um[hs]


def _dn_seq_body(w_ref, u_ref, qg_ref, kd_ref, sc_ref, eg_ref, z_ref, ng_ref, o_ref, so_ref, s_ref):
    nck, heads, c = w_ref.shape[0], w_ref.shape[1], w_ref.shape[2]
    ci = pl.program_id(1)

    @pl.when(ci == 0)
    def _():
        s_ref[...] = jnp.zeros_like(s_ref)

    def chunk(n, carry):
        rs = pl.ds(pl.multiple_of(n * c, c), c)
        for h in range(heads):
            hs = slice(h * DN_HEAD_DIM, (h + 1) * DN_HEAD_DIM)
            s = s_ref[h]
            r = jnp.dot(jnp.concatenate([w_ref[n, h], qg_ref[n, h]], axis=0), s.astype(BF16),
                        preferred_element_type=F32)
            vn16 = (u_ref[n, h] - r[0:c]).astype(BF16)
            o = r[c:2 * c] + jnp.dot(sc_ref[n, h], vn16, preferred_element_type=F32)
            s_ref[h] = s * eg_ref[n, h] + _dot_tn(kd_ref[n, h], vn16)
            o_ref[rs, hs] = (_rms(o, ng_ref[...]) * z_ref[rs, hs]).astype(o_ref.dtype)
        return carry

    lax.fori_loop(0, nck, chunk, 0)

    @pl.when(ci == pl.num_programs(1) - 1)
    def _():
        so_ref[...] = s_ref[...]


DN_WY_CHUNKS = 4
DN_SEQ_CHUNKS = 8


def _dn_prompt(qkvp, z, ba, al, dt, ng, layer, cfg):
    tp, seq, bp = cfg["tp"], cfg["seq"], cfg["bp"]
    w = qkvp.shape[1] // 3
    heads = w // DN_HEAD_DIM
    c, d = DN_CHUNK, DN_HEAD_DIM
    nc = tp // c
    kw, ks = DN_WY_CHUNKS, DN_SEQ_CHUNKS
    assert nc % kw == 0 and (seq // c) % ks == 0 and heads % 2 == 0
    vec = pl.BlockSpec((None, 1, LANE), lambda *_: (layer, 0, 0))
    hm = lambda n, last: pl.BlockSpec((n, heads, c, last), lambda i: (i, 0, 0, 0))
    eg_spec = lambda n: pl.BlockSpec((n, heads, 1, LANE), lambda i: (i, 0, 0, 0))
    sds = jax.ShapeDtypeStruct
    wm, um, qg, kd, sc, eg = pl.pallas_call(
        _dn_wy_body, grid=(nc // kw,),
        in_specs=[pl.BlockSpec((kw * c, w), lambda i: (i, 0)), pl.BlockSpec((kw * c, w), lambda i: (i, 1)),
                  pl.BlockSpec((kw * c, w), lambda i: (i, 2)), pl.BlockSpec((kw * c, LANE), lambda i: (i, 0)),
                  vec, vec],
        out_specs=[hm(kw, d), hm(kw, d), hm(kw, d), hm(kw, d), hm(kw, c), eg_spec(kw)],
        out_shape=[sds((nc, heads, c, d), BF16), sds((nc, heads, c, d), F32), sds((nc, heads, c, d), BF16),
                   sds((nc, heads, c, d), BF16), sds((nc, heads, c, c), BF16), sds((nc, heads, 1, LANE), F32)],
        scratch_shapes=[pltpu.VMEM((8, kw * heads // 2, 2 * c, 2 * c), F32),
                        pltpu.VMEM((2, kw * heads // 2, 2 * c, d), BF16)],
        compiler_params=_params(("arbitrary",)), name="dn_wy")(qkvp, qkvp, qkvp, ba, al, dt)
    nps = seq // c // ks
    hm2 = lambda last: pl.BlockSpec((ks, heads, c, last), lambda b, n: (b * nps + n, 0, 0, 0))
    return pl.pallas_call(
        _dn_seq_body, grid=(bp, nps),
        in_specs=[hm2(d), hm2(d), hm2(d), hm2(d), hm2(c),
                  pl.BlockSpec((ks, heads, 1, LANE), lambda b, n: (b * nps + n, 0, 0, 0)),
                  pl.BlockSpec((ks * c, w), lambda b, n: (b * nps + n, 0)),
                  pl.BlockSpec((None, 1, LANE), lambda b, n: (layer, 0, 0))],
        out_specs=[pl.BlockSpec((ks * c, w), lambda b, n: (b * nps + n, 0)),
                   pl.BlockSpec((None, heads, d, d), lambda b, n: (b, 0, 0, 0))],
        out_shape=[sds((tp, w), BF16), sds((bp, heads, d, d), F32)],
        scratch_shapes=[pltpu.VMEM((heads, d, d), F32)],
        compiler_params=_params(("arbitrary", "arbitrary")), name="dn_seq")(wm, um, qg, kd, sc, eg, z, ng)


def _dn_chunk_body(q_ref, k_ref, v_ref, z_ref, ba_ref, s0_ref, al_ref, dt_ref, ng_ref, chain_ref,
                   o_ref, so_ref, s_ref, *, groups, n_valid):
    c = q_ref.shape[0]
    heads = q_ref.shape[1] // DN_HEAD_DIM
    rpg = c // groups
    s_ref[...] = s0_ref[...]

    row = lax.broadcasted_iota(I32, (c, c), 0)
    col = lax.broadcasted_iota(I32, (c, c), 1)
    same = (row // rpg) == (col // rpg)
    tri = (row >= col) & same
    strict = (row > col) & same
    rowl = lax.broadcasted_iota(I32, (c, LANE), 0)
    beta_all, g_all = _dn_gates(ba_ref[...], al_ref[...], dt_ref[...], heads, (rowl % rpg) < n_valid)
    gcum_all = _dotf(tri.astype(F32), g_all)
    rg = lax.broadcasted_iota(I32, (c, 1), 0) // rpg
    rg2 = jnp.concatenate([rg, rg], axis=0)

    for h in range(heads):
        hs = slice(h * DN_HEAD_DIM, (h + 1) * DN_HEAD_DIM)
        q, k, v = q_ref[:, hs], k_ref[:, hs], v_ref[:, hs]
        beta = beta_all[:, h:h + 1]
        gcol = gcum_all[:, heads + h:heads + h + 1]
        gmat = jnp.broadcast_to(
            jnp.concatenate([gcol, jnp.zeros((LANE - c, 1), F32)], axis=0), (LANE, LANE))
        decay = jnp.exp(jnp.where(tri, gmat[0:c, 0:c] - gmat.T[0:c, 0:c], NEG_BIG))
        kb = k * beta
        k16 = k.astype(BF16)
        tinv = _inv_unit_lower(jnp.where(strict, _dot_nt(kb.astype(BF16), k16) * decay, 0.0), rpg)
        scores = _dot_nt(q.astype(BF16), k16) * decay
        egc = jnp.exp(gcol)
        lhs = jnp.concatenate([kb * egc, q * egc], axis=0).astype(BF16)
        ws = jnp.zeros((2 * c, DN_HEAD_DIM), F32)
        glast_rows = jnp.zeros((c, 1), F32)
        glasts = []
        for gi in range(groups):
            r = jnp.dot(lhs, s_ref[gi, h].astype(BF16), preferred_element_type=F32)
            gl = gcol[(gi + 1) * rpg - 1:(gi + 1) * rpg, :]
            glasts.append(gl)
            ws = jnp.where(rg2 == gi, r, ws)
            glast_rows = jnp.where(rg == gi, gl, glast_rows)
        vn16 = _dotb(tinv, v * beta - ws[0:c]).astype(BF16)
        o = ws[c:2 * c] + jnp.dot(scores.astype(BF16), vn16, preferred_element_type=F32)
        kdec = k * jnp.exp(glast_rows - gcol)
        for gi in range(groups):
            upd = _dot_tn(jnp.where(rg == gi, kdec, 0.0).astype(BF16), vn16)
            s_ref[gi, h] = s_ref[gi, h] * jnp.exp(glasts[gi]) + upd
        o_ref[:, hs] = (_rms(o, ng_ref[...]) * z_ref[:, hs]).astype(o_ref.dtype)

    so_ref[...] = s_ref[...]


def _dn_sample(qkv, z, ba, state, chain, al, dt, ng, layer, *, n_valid):
    rows, w3 = qkv.shape
    w = w3 // 3
    heads = w // DN_HEAD_DIM
    c = DN_CHUNK
    groups = c // SUBLANE
    rspec = lambda col: pl.BlockSpec((c, w), lambda b: (b, col))
    vec = pl.BlockSpec((None, 1, LANE), lambda b: (layer, 0, 0))
    sspec = pl.BlockSpec((None, groups, heads, DN_HEAD_DIM, DN_HEAD_DIM), lambda b: (layer, b, 0, 0, 0))
    return pl.pallas_call(
        functools.partial(_dn_chunk_body, groups=groups, n_valid=n_valid),
        grid=(rows // c,),
        in_specs=[rspec(0), rspec(1), rspec(2), rspec(0), pl.BlockSpec((c, LANE), lambda b: (b, 0)),
                  sspec, vec, vec, vec, pl.BlockSpec(memory_space=pl.ANY)],
        out_specs=[rspec(0), sspec],
        out_shape=[jax.ShapeDtypeStruct((rows, w), BF16), jax.ShapeDtypeStruct(chain.shape, F32)],
        scratch_shapes=[pltpu.VMEM((groups, heads, DN_HEAD_DIM, DN_HEAD_DIM), F32)],
        input_output_aliases={9: 1},
        compiler_params=_params(("arbitrary",)), name="dn_sample")(
            qkv, qkv, qkv, z, ba, state, al, dt, ng, chain)


def _layer_norm(v, g, b):
    mu = jnp.mean(v, axis=-1, keepdims=True)
    vc = v - mu
    return vc * lax.rsqrt(jnp.mean(vc * vc, axis=-1, keepdims=True) + LN_EPS) * g + b


def _gmlp_body(uv_ref, g_ref, b_ref, ws_ref, bst_ref, cw_ref, cb_ref, o_ref, vn_ref, *, npt, bs):
    w = uv_ref.shape[1] // 2
    i = pl.program_id(0)
    vn = _layer_norm(uv_ref[:, w:], g_ref[...], b_ref[...])

    @pl.when(i < npt)
    def _():
        groups = ws_ref.shape[0]
        gw = w // groups
        v16 = vn.astype(BF16)
        row = lax.broadcasted_iota(I32, (GM_CHUNK, GM_CHUNK), 0)
        col = lax.broadcasted_iota(I32, (GM_CHUNK, GM_CHUNK), 1)
        wcs = [jnp.where(row >= col, ws_ref[g], 0.0).astype(BF16) for g in range(groups)]
        bst = bst_ref[...]
        for c in range(uv_ref.shape[0] // GM_CHUNK):
            rs = slice(c * GM_CHUNK, (c + 1) * GM_CHUNK)
            for g in range(groups):
                cs = slice(g * gw, (g + 1) * gw)
                s = jnp.dot(wcs[g], v16[rs, cs], preferred_element_type=F32) + bst[:, g:g + 1]
                o_ref[rs, cs] = (uv_ref[rs, cs] * s).astype(o_ref.dtype)

    @pl.when(i >= npt)
    def _():
        steps = uv_ref.shape[0] // bs
        vn_ref[...] = vn
        for t in range(steps):
            s = cb_ref[t:t + 1, :]
            for u in range(t + 1):
                s = s + cw_ref[t * steps + u:t * steps + u + 1, :] * vn[u * bs:(u + 1) * bs]
            o_ref[t * bs:(t + 1) * bs, :] = (uv_ref[t * bs:(t + 1) * bs, 0:w] * s).astype(o_ref.dtype)


def _gmlp(uv, ln_g, ln_b, ws, bst, cw, cb, layer, cfg):
    tl, bs = cfg["ts"], cfg["bs"]
    t_rows, w2 = uv.shape
    w = w2 // 2
    groups = ws.shape[1]
    steps = tl // bs
    vec = pl.BlockSpec((None, 1, w), lambda i: (layer, 0, 0))
    return pl.pallas_call(
        functools.partial(_gmlp_body, npt=cfg["tp"] // tl, bs=bs), grid=(t_rows // tl,),
        in_specs=[pl.BlockSpec((tl, w2), lambda i: (i, 0)), vec, vec,
                  pl.BlockSpec((None, groups, GM_CHUNK, GM_CHUNK), lambda i: (layer, 0, 0, 0)),
                  pl.BlockSpec((None, GM_CHUNK, LANE), lambda i: (layer, 0, 0)),
                  pl.BlockSpec((None, steps * steps, w), lambda i: (layer, 0, 0)),
                  pl.BlockSpec((None, SUBLANE, w), lambda i: (layer, 0, 0))],
        out_specs=[pl.BlockSpec((tl, w), lambda i: (i, 0)), pl.BlockSpec((tl, w), lambda i: (0, 0))],
        out_shape=[jax.ShapeDtypeStruct((t_rows, w), BF16), jax.ShapeDtypeStruct((tl, w), F32)],
        compiler_params=_params(("arbitrary",)), name="gmlp")(uv, ln_g, ln_b, ws, bst, cw, cb)


def _lru_gates(xc, wa_ref, ba, wx_ref, bx, lam):
    nb = wa_ref.shape[0]
    bw = xc.shape[1] // nb
    x16 = xc.astype(BF16)
    rs, gs = [], []
    for h in range(nb):
        xs = x16[:, h * bw:(h + 1) * bw]
        rs.append(jnp.dot(xs, wa_ref[h].astype(BF16), preferred_element_type=F32))
        gs.append(jnp.dot(xs, wx_ref[h].astype(BF16), preferred_element_type=F32))
    r = jax.nn.sigmoid(jnp.concatenate(rs, axis=1) + ba)
    ig = jax.nn.sigmoid(jnp.concatenate(gs, axis=1) + bx)
    log_a = -LRU_C * r * jax.nn.softplus(-lam)
    th = jnp.tanh(log_a)
    return jnp.exp(log_a), jnp.sqrt(-2.0 * th / (1.0 - th)), ig


def _lru_body(x_ref, prev_ref, y_ref, hist_ref, h0_ref, cw_ref, cb_ref, wa_ref, ba_ref, wx_ref, bx_ref,
              lam_ref, o_ref, hlp_ref, hls_ref, buf_ref, a_ref, b_ref, h_ref, *, npt, tps, bs, pos0):
    te, w = x_ref.shape
    i = pl.program_id(0)

    def gates(xc):
        return _lru_gates(xc, wa_ref, ba_ref[...], wx_ref, bx_ref[...], lam_ref[...])

    @pl.when(i < npt)
    def _():
        first = i % tps == 0
        xc = _conv_rows(x_ref, prev_ref, buf_ref, cw_ref[...], first) + cb_ref[...]
        a, mult, ig = gates(xc)
        reset = first & (lax.broadcasted_iota(I32, (te, w), 0) == 0)
        a_ref[...] = jnp.where(reset, 0.0, a)
        b_ref[...] = jnp.where(reset, 1.0, mult) * ig * xc

        @pl.when(first)
        def _():
            h_ref[...] = jnp.zeros_like(h_ref)

        sub = lax.broadcasted_iota(I32, (SUBLANE, w), 0)

        def slab(s, h):
            rs = pl.ds(pl.multiple_of(s * SUBLANE, SUBLANE), SUBLANE)
            av, bv = a_ref[rs, :], b_ref[rs, :]
            for dlt in (1, 2, 4):
                ash, bsh = pltpu.roll(av, dlt, 0), pltpu.roll(bv, dlt, 0)
                m = sub >= dlt
                bv = jnp.where(m, av * bsh + bv, bv)
                av = jnp.where(m, av * ash, av)
            hs = av * h + bv
            b_ref[rs, :] = hs
            return hs[SUBLANE - 1:SUBLANE, :]

        h_last = lax.fori_loop(0, te // SUBLANE, slab, h_ref[...])
        h_ref[...] = h_last
        hlp_ref[...] = h_last
        o_ref[...] = (y_ref[...] * b_ref[...]).astype(o_ref.dtype)

    @pl.when(i >= npt)
    def _():
        ys = _conv_slabs(x_ref, hist_ref, cw_ref[...], bs)
        h = h0_ref[...]
        for t, y in enumerate(ys):
            xc = y + cb_ref[...]
            a, mult, ig = gates(xc)
            if pos0 + t == 0:
                a, mult = jnp.zeros_like(a), jnp.ones_like(mult)
            h = a * h + mult * ig * xc
            o_ref[t * bs:(t + 1) * bs, :] = (y_ref[t * bs:(t + 1) * bs, :] * h).astype(o_ref.dtype)
        hls_ref[...] = h


def _lru(lx, ly, hist_s, h0_s, p, layer, cfg):
    tl, bs, bp = cfg["ts"], cfg["bs"], cfg["bp"]
    t_rows, w = lx.shape
    nb = p["lru_wa"].shape[1]
    bw = w // nb
    npt, tps = cfg["tp"] // tl, cfg["seq"] // tl
    rpb = tl // SUBLANE
    vec = pl.BlockSpec((None, 1, w), lambda i: (layer, 0, 0))
    blk = pl.BlockSpec((None, nb, bw, bw), lambda i: (layer, 0, 0, 0))
    row = pl.BlockSpec((tl, w), lambda i: (i, 0))
    oc, hl_p, hl_s = pl.pallas_call(
        functools.partial(_lru_body, npt=npt, tps=tps, bs=bs, pos0=cfg["pos0_s"]), grid=(t_rows // tl,),
        in_specs=[row, pl.BlockSpec((SUBLANE, w), lambda i: (jnp.maximum(i * rpb - 1, 0), 0)), row,
                  pl.BlockSpec(((CONV_W - 1) * bs, w), lambda i: (0, 0)),
                  pl.BlockSpec((None, bs, w), lambda i: (layer, 0, 0)),
                  pl.BlockSpec((None, CONV_W, w), lambda i: (layer, 0, 0)), vec, blk, vec, blk, vec, vec],
        out_specs=[row, pl.BlockSpec((None, 1, w), lambda i: (jnp.minimum(i // tps, bp - 1), 0, 0)),
                   pl.BlockSpec((bs, w), lambda i: (0, 0))],
        out_shape=[jax.ShapeDtypeStruct((t_rows, w), BF16), jax.ShapeDtypeStruct((bp, 1, w), F32),
                   jax.ShapeDtypeStruct((bs, w), F32)],
        scratch_shapes=[pltpu.VMEM((tl + SUBLANE, w), F32), pltpu.VMEM((tl, w), F32),
                        pltpu.VMEM((tl, w), F32), pltpu.VMEM((1, w), F32)],
        compiler_params=_params(("arbitrary",)), name="lru")(
            lx, lx, ly, hist_s, h0_s, p["lru_conv_w"], p["lru_conv_b"], p["lru_wa"], p["lru_ba"],
            p["lru_wx"], p["lru_bx"], p["lru_lambda"])
    return oc, hl_p.reshape(bp, w), hl_s


def _time_major(a):
    return jnp.swapaxes(a, 0, 1).reshape(a.shape[0] * a.shape[1], a.shape[2])


def _batch_major(a, bs):
    return jnp.swapaxes(a.reshape(a.shape[0] // bs, bs, a.shape[1]), 0, 1)


def _moe_plan(route, n_exp, tm):
    t_rows = route.shape[0]
    flat_e = route[:, 2:2 + TOP_K].astype(I32).reshape(-1)
    oh = (flat_e[:, None] == jnp.arange(n_exp, dtype=I32)[None, :]).astype(I32)
    csum = jnp.cumsum(oh, axis=0)
    rank = jnp.take_along_axis(csum, flat_e[:, None], axis=1)[:, 0] - 1
    counts = csum[-1]
    tiles_e = (counts + tm - 1) // tm
    tile_end = jnp.cumsum(tiles_e)
    dest = ((tile_end - tiles_e) * tm)[flat_e] + rank
    n_tiles = pl.cdiv(TOP_K * t_rows, tm) + n_exp
    src = jnp.zeros((n_tiles * tm,), I32).at[dest].set(jnp.arange(TOP_K * t_rows, dtype=I32) // TOP_K)
    tile_e = jnp.sum((jnp.arange(n_tiles, dtype=I32)[:, None] >= tile_end[None, :]).astype(I32), axis=1)
    tile_e = jnp.minimum(tile_e, n_exp - 1)
    dest = dest.reshape(t_rows, TOP_K)
    return dest[:, 0], dest[:, 1], src, tile_e, tile_end[-1].astype(I32)


def kernel(x_prompt, x_sample, state_dn_S, state_dn_conv, state_lru_h, state_lru_conv, c_prompt, c_sample,
           w_mod, b_mod, norm_g, w_in, dn_conv_w, dn_a_log, dn_dt_bias, dn_norm_g, gm_ln_g, gm_ln_b, gm_ws,
           gm_bs, lru_conv_w, lru_conv_b, lru_wa, lru_ba, lru_wx, lru_bx, lru_lambda, w_branch, w_out,
           ffn_w_gate, ffn_w_up, ffn_w_down, router_w, router_b, moe_w_gate, moe_w_up, moe_w_down):
    bp, seq, d = x_prompt.shape
    bs, dec = x_sample.shape[0], x_sample.shape[1]
    depth = w_mod.shape[0]
    w = d // 2
    heads = w // DN_HEAD_DIM
    n_exp = moe_w_gate.shape[1]
    tp, ts = bp * seq, bs * dec
    t_rows = tp + ts
    past_len = 16384
    assert dec <= SUBLANE and dec >= CONV_W - 1 and ts % GM_CHUNK == 0 and seq % ts == 0
    assert tp % ts == 0 and ts % TE == 0 and bs % SUBLANE == 0 and t_rows % (16 * 2 * DENSE_ROW_TILES) == 0
    cfg = dict(bp=bp, bs=bs, seq=seq, tp=tp, ts=ts, pos0_s=past_len)
    tm_d = t_rows // DENSE_ROW_TILES

    def dense_meta(slab, ni=DENSE_ROW_TILES):
        return jnp.concatenate([jnp.full((ni,), slab, I32), jnp.array([ni], I32)])

    n_c = bp + bs
    n_c_pad = -(-n_c // SUBLANE) * SUBLANE
    c_all = jnp.concatenate([c_prompt, c_sample, jnp.zeros((n_c_pad - n_c, d), F32)], axis=0)
    b_mod3 = b_mod.reshape(depth, 1, N_MOD * d)
    mods = []
    for l in range(depth):
        meta = jnp.array([l, 1], I32)
        mods.append(_mm(c_all, w_mod, meta, name="mod", tm=n_c_pad, pre_act="silu", bias=b_mod3))
    mod = jnp.stack(mods).reshape(depth, n_c_pad, N_MOD, d)
    cfg["mod_p"] = mod[:, :bp].reshape(depth * bp, N_MOD, d)
    cfg["mod_s"] = jnp.swapaxes(mod[:, bp:n_c], 1, 2).reshape(depth * N_MOD, bs, d)
    cfg["norm_g"] = norm_g.reshape(depth * 4, 1, d)

    n_qkv, n_z = 3 * w, w
    c_ba = n_qkv + n_z
    c_rest = c_ba + 2 * heads
    assert c_rest % SUBLANE == 0
    w_in_t = jnp.swapaxes(w_in, 1, 2)
    lane_pad = lambda a, off: jnp.pad(a, ((0, 0), (off, LANE - off - a.shape[1]))).reshape(depth, 1, LANE)
    al_pad, dt_pad = lane_pad(dn_a_log, heads), lane_pad(dn_dt_bias, heads)
    dn_ng = dn_norm_g.reshape(depth, 1, DN_HEAD_DIM)
    vec3 = lambda a: a.reshape(depth, 1, a.shape[-1])
    lru_p = dict(lru_conv_w=lru_conv_w, lru_conv_b=vec3(lru_conv_b), lru_wa=lru_wa, lru_ba=vec3(lru_ba),
                 lru_wx=lru_wx, lru_bx=vec3(lru_bx), lru_lambda=vec3(lru_lambda))
    gm_g3, gm_b3 = vec3(gm_ln_g), vec3(gm_ln_b)
    gm_bst = jnp.pad(jnp.swapaxes(gm_bs, 1, 2), ((0, 0), (0, 0), (0, LANE - gm_bs.shape[1])))
    gw = w // gm_ws.shape[1]
    gm_cw = jnp.repeat(jnp.swapaxes(gm_ws[:, :, :dec, :dec], 1, 3).reshape(depth, dec, dec, -1), gw, axis=-1)
    gm_cw = jnp.swapaxes(gm_cw, 1, 2).reshape(depth, dec * dec, w)
    gm_cb = jnp.repeat(jnp.swapaxes(gm_bs[:, :, :dec], 1, 2), gw, axis=-1)
    gm_cb = jnp.pad(gm_cb, ((0, 0), (0, SUBLANE - dec), (0, 0)))
    wb3 = w_branch.reshape(depth * 3, w, d)
    n_moe = router_w.shape[0]
    rw_pad = jnp.pad(router_w, ((0, 0), (0, 0), (0, LANE - n_exp)))
    rb_pad = jnp.pad(router_b, ((0, 0), (0, LANE - n_exp)), constant_values=NEG_BIG).reshape(n_moe, 1, LANE)
    d_ff = ffn_w_gate.shape[2]
    moe_wg = moe_w_gate.reshape(n_moe * n_exp, d, d_ff)
    moe_wu = moe_w_up.reshape(n_moe * n_exp, d, d_ff)
    moe_wd = moe_w_down.reshape(n_moe * n_exp, d_ff, d)

    x = jnp.concatenate([x_prompt.reshape(tp, d), _time_major(x_sample)], axis=0)
    s_chain = jnp.zeros(state_dn_S.shape, F32)
    outs = {k: [] for k in ("S_p", "dc_p", "h_p", "lc_p", "dc_s", "h_s", "lc_s", "v_s")}

    def last_rows(a):
        return jnp.stack([a[(b + 1) * seq - (CONV_W - 1):(b + 1) * seq] for b in range(bp)])

    def to_blocks(a):
        a = jnp.pad(_batch_major(a, bs), ((0, 0), (0, SUBLANE - dec), (0, 0)))
        return a.reshape(bs * SUBLANE, a.shape[2])

    (hh,) = _post(x, None, cfg, res=None, nxt=(0, 0, 1, 0), name="pre0")
    for l in range(depth):
        meta = dense_meta(l)
        proj = functools.partial(_mm, hh, w_in_t, meta, w_nk=True, tm=tm_d, tn=TN_WIDE)
        qkv = proj(name="in_qkv", n=n_qkv)
        zs = proj(name="in_z", n_off=n_qkv, n=n_z, act="silu")
        ba = proj(name="in_ba", n_off=c_ba, n=LANE, tn=LANE)
        uv = proj(name="in_uv", n_off=c_rest, n=2 * w, act="gelu")
        lx = proj(name="in_lx", n_off=c_rest + 2 * w, n=w)
        ly = proj(name="in_ly", n_off=c_rest + 3 * w, n=w, act="gelu")
        gates = proj(name="in_merge", n_off=c_rest + 4 * w, n=3 * d, act="sigmoid", out_dtype=BF16)

        dn_hist = _time_major(state_dn_conv[l])
        qkvp = _dn_prep(qkv, dn_hist, dn_conv_w, l, cfg)
        oa_p, s_p = _dn_prompt(qkvp, zs, ba, al_pad, dt_pad, dn_ng, l, cfg)
        oa_sb, s_chain = _dn_sample(to_blocks(qkvp[tp:]), to_blocks(zs[tp:]), to_blocks(ba[tp:]),
                                    state_dn_S, s_chain, al_pad, dt_pad, dn_ng, l, n_valid=dec)
        oa = jnp.concatenate([oa_p, _time_major(oa_sb.reshape(bs, SUBLANE, w)[:, :dec])], axis=0)
        dc_hist = jnp.concatenate([dn_hist, qkv[tp:]], axis=0)

        ob, vn_s = _gmlp(uv, gm_g3, gm_b3, gm_ws, gm_bst, gm_cw, gm_cb, l, cfg)
        lru_hist = _time_major(state_lru_conv[l])
        oc, hl_p, hl_s = _lru(lx, ly, lru_hist, state_lru_h, lru_p, l, cfg)
        lc_hist = jnp.concatenate([lru_hist, lx[tp:]], axis=0)

        outs["S_p"].append(s_p)
        outs["dc_p"].append(last_rows(qkv))
        outs["h_p"].append(hl_p)
        outs["lc_p"].append(last_rows(lx))
        outs["dc_s"].append(_batch_major(dc_hist[dec * bs:], bs))
        outs["h_s"].append(hl_s)
        outs["lc_s"].append(_batch_major(lc_hist[dec * bs:], bs))
        outs["v_s"].append(_batch_major(vn_s, bs))

        m = _branch_mm(oa, ob, oc, gates, wb3, meta, name="branch", tm=tm_d)
        y = _mm(m, w_out, meta, name="w_out", tm=tm_d, tn=TN_WIDE)
        j = l // 2
        if l % 2 == 0:
            x, hh = _post(x, y, cfg, res=(l, 1, 2), nxt=(l, 2, 4, 3), name="post_mix")
            hid = _ffn_up(hh, ffn_w_gate, ffn_w_up, dense_meta(j), name="ffn_up", tm=tm_d)
            y = _mm(hid, ffn_w_down, dense_meta(j, 2 * DENSE_ROW_TILES), name="ffn_down", tm=tm_d // 2)
        else:
            x, hh32, route = _post(x, y, cfg, res=(l, 1, 2), nxt=(l, 2, 4, 3), router=(rw_pad, rb_pad, j),
                                   hh_dtype=F32, name="post_mix_r")
            d0, d1, src, tile_e, nused = _moe_plan(route, n_exp, TM)
            xs = _dispatch(hh32, src, nused.reshape(1), tm=TM, name="moe_dispatch")
            emeta = jnp.concatenate([j * n_exp + tile_e, nused.reshape(1)])
            hid = _ffn_up(xs, moe_wg, moe_wu, emeta, name="moe_up")
            ys = _mm(hid, moe_wd, emeta, name="moe_down")
            y = _combine(ys, d0, d1, route, name="moe_combine")
        if l + 1 < depth:
            x, hh = _post(x, y, cfg, res=(l, 3, 5), nxt=(l + 1, 0, 1, 0), name="post_ffn")
        else:
            (x,) = _post(x, y, cfg, res=(l, 3, 5), nxt=None, name="post_last")

    st = lambda k: jnp.stack(outs[k])
    return (x[:tp].reshape(bp, seq, d), _batch_major(x[tp:], bs),
            st("S_p"), st("dc_p"), st("h_p"), st("lc_p"),
            s_chain, st("dc_s"), st("h_s"), st("lc_s"), st("v_s"))
```

```python
import functools

import jax
import jax.numpy as jnp
from jax import lax
from jax.experimental import pallas as pl
from jax.experimental.pallas import tpu as pltpu

F32, BF16, I32 = jnp.float32, jnp.bfloat16, jnp.int32
HI = lax.Precision.HIGHEST

LANE = 128
SUBLANE = 8
VMEM_LIMIT = 56 << 20
DN_HEAD_DIM = 128
DN_CHUNK = 64
CONV_W = 4
GM_CHUNK = 128
LRU_BLOCKS = 8
LRU_C = 8.0
N_MOD = 6
TOP_K = 2
RMS_EPS = 1e-6
LN_EPS = 1e-5
L2_EPS = 1e-6
NEG_BIG = -1e30

TM = 512
TN = 512
TN_WIDE = 1024
DENSE_ROW_TILES = 8
TE = 256
DMA_ISSUE_UNROLL = 8


def _params(sem):
    return pltpu.CompilerParams(dimension_semantics=sem, vmem_limit_bytes=VMEM_LIMIT)


def _act(name, x):
    if name is None:
        return x
    if name == "silu":
        return x * jax.nn.sigmoid(x)
    if name == "gelu":
        return jax.nn.gelu(x)
    if name == "sigmoid":
        return jax.nn.sigmoid(x)
    raise ValueError(name)


def _rms(x, g):
    return x * lax.rsqrt(jnp.mean(x * x, axis=-1, keepdims=True) + RMS_EPS) * g


def _cast_once(meta_ref, i, pairs):
    prev = meta_ref[jnp.maximum(i - 1, 0)]

    @pl.when((i == 0) | (meta_ref[i] != prev))
    def _():
        for src, dst in pairs:
            dst[...] = src[...].astype(BF16)


def _mm_body(meta_ref, x_ref, w_ref, *rest, ni, act, pre_act, cast_w, has_bias, w_nk):
    rest = list(rest)
    b_ref = rest.pop(0) if has_bias else None
    o_ref = rest.pop(0)
    wbf_ref = rest.pop(0) if cast_w else None
    i = pl.program_id(1)
    nused = meta_ref[ni]

    @pl.when(i < nused)
    def _():
        if cast_w:
            _cast_once(meta_ref, i, [(w_ref, wbf_ref)])
            w = wbf_ref[...]
        else:
            w = w_ref[...]
        w = w[0] if w_nk else w
        x = x_ref[...]
        if pre_act is not None:
            x = _act(pre_act, x.astype(F32))
        dims = (((1,), (1,)), ((), ())) if w_nk else (((1,), (0,)), ((), ()))
        acc = lax.dot_general(x.astype(BF16), w, dims, preferred_element_type=F32)
        if has_bias:
            acc = acc + b_ref[...]
        o_ref[...] = _act(act, acc).astype(o_ref.dtype)

    @pl.when(i >= nused)
    def _():
        o_ref[...] = jnp.zeros_like(o_ref)


def _mm(x, w, meta, *, name, n_off=0, n=None, tm=TM, tn=TN, act=None, pre_act=None,
        bias=None, out_dtype=F32, w_nk=False):
    m_rows, k = x.shape
    n = w.shape[1 if w_nk else 2] if n is None else n
    tm = min(tm, m_rows)
    ni, nj = pl.cdiv(m_rows, tm), n // tn
    assert n % tn == 0 and meta.shape[0] == ni + 1
    cast_w = w.dtype != BF16
    body = functools.partial(_mm_body, ni=ni, act=act, pre_act=pre_act, cast_w=cast_w,
                             has_bias=bias is not None, w_nk=w_nk)
    if w_nk:
        assert n_off % SUBLANE == 0 and tn % SUBLANE == 0
        wspec = pl.BlockSpec((pl.Element(1), pl.Element(tn), pl.Element(k)),
                             lambda j, i, m: (m[i], (n_off // SUBLANE + j * (tn // SUBLANE)) * SUBLANE, 0))
    else:
        wspec = pl.BlockSpec((None, k, tn), lambda j, i, m: (m[i], 0, j + n_off))
    in_specs = [pl.BlockSpec((tm, k), lambda j, i, m: (jnp.minimum(i, m[ni] - 1), 0)), wspec]
    args = [x, w]
    if bias is not None:
        in_specs.append(pl.BlockSpec((None, 1, tn), lambda j, i, m: (m[i], 0, j + n_off)))
        args.append(bias)
    scratch = [pltpu.VMEM((1, tn, k) if w_nk else (k, tn), BF16)] if cast_w else []
    return pl.pallas_call(
        body,
        grid_spec=pltpu.PrefetchScalarGridSpec(
            num_scalar_prefetch=1, grid=(nj, ni), in_specs=in_specs,
            out_specs=pl.BlockSpec((tm, tn), lambda j, i, m: (i, j)), scratch_shapes=scratch),
        out_shape=jax.ShapeDtypeStruct((m_rows, n), out_dtype),
        compiler_params=_params(("arbitrary", "arbitrary")), name=name)(meta, *args)


def _ffn_up_body(meta_ref, x_ref, wg_ref, wu_ref, o_ref, wgb_ref, wub_ref, *, ni):
    i = pl.program_id(1)
    nused = meta_ref[ni]

    @pl.when(i < nused)
    def _():
        _cast_once(meta_ref, i, [(wg_ref, wgb_ref), (wu_ref, wub_ref)])
        x = x_ref[...]
        g = jnp.dot(x, wgb_ref[...], preferred_element_type=F32)
        u = jnp.dot(x, wub_ref[...], preferred_element_type=F32)
        o_ref[...] = (g * jax.nn.sigmoid(g) * u).astype(o_ref.dtype)

    @pl.when(i >= nused)
    def _():
        o_ref[...] = jnp.zeros_like(o_ref)


def _ffn_up(x, wg, wu, meta, *, name, tm=TM, tn=TN):
    m_rows, k = x.shape
    n = wg.shape[2]
    ni, nj = m_rows // tm, n // tn
    assert m_rows % tm == 0 and n % tn == 0 and meta.shape[0] == ni + 1
    wspec = pl.BlockSpec((None, k, tn), lambda j, i, m: (m[i], 0, j))
    return pl.pallas_call(
        functools.partial(_ffn_up_body, ni=ni),
        grid_spec=pltpu.PrefetchScalarGridSpec(
            num_scalar_prefetch=1, grid=(nj, ni),
            in_specs=[pl.BlockSpec((tm, k), lambda j, i, m: (jnp.minimum(i, m[ni] - 1), 0)),
                      wspec, wspec],
            out_specs=pl.BlockSpec((tm, tn), lambda j, i, m: (i, j)),
            scratch_shapes=[pltpu.VMEM((k, tn), BF16), pltpu.VMEM((k, tn), BF16)]),
        out_shape=jax.ShapeDtypeStruct((m_rows, n), BF16),
        compiler_params=_params(("arbitrary", "arbitrary")), name=name)(meta, x, wg, wu)


def _branch_body(meta_ref, xa_ref, xb_ref, xc_ref, ga_ref, gb_ref, gc_ref,
                 w0_ref, w1_ref, w2_ref, o_ref, s0_ref, s1_ref, s2_ref):
    i = pl.program_id(1)
    _cast_once(meta_ref, i, [(w0_ref, s0_ref), (w1_ref, s1_ref), (w2_ref, s2_ref)])
    acc = ga_ref[...] * jnp.dot(xa_ref[...], s0_ref[...], preferred_element_type=F32)
    acc = acc + gb_ref[...] * jnp.dot(xb_ref[...], s1_ref[...], preferred_element_type=F32)
    acc = acc + gc_ref[...] * jnp.dot(xc_ref[...], s2_ref[...], preferred_element_type=F32)
    o_ref[...] = acc.astype(o_ref.dtype)


def _branch_mm(oa, ob, oc, gates, wb, meta, *, name, tm=TM, tn=TN):
    m_rows, k = oa.shape
    n = wb.shape[2]
    ni, nj = m_rows // tm, n // tn
    xspec = pl.BlockSpec((tm, k), lambda j, i, m: (i, 0))

    def gspec(b):
        return pl.BlockSpec((tm, tn), lambda j, i, m: (i, j + b * nj))

    def wspec(b):
        return pl.BlockSpec((None, k, tn), lambda j, i, m: (m[i] * 3 + b, 0, j))

    return pl.pallas_call(
        _branch_body,
        grid_spec=pltpu.PrefetchScalarGridSpec(
            num_scalar_prefetch=1, grid=(nj, ni),
            in_specs=[xspec, xspec, xspec, gspec(0), gspec(1), gspec(2), wspec(0), wspec(1), wspec(2)],
            out_specs=pl.BlockSpec((tm, tn), lambda j, i, m: (i, j)),
            scratch_shapes=[pltpu.VMEM((k, tn), BF16)] * 3),
        out_shape=jax.ShapeDtypeStruct((m_rows, n), BF16),
        compiler_params=_params(("arbitrary", "arbitrary")), name=name)(
            meta, oa, ob, oc, gates, gates, gates, wb, wb, wb)


def _route(hh, rw_ref, rb_ref):
    lg = jnp.dot(hh, rw_ref[...], precision=HI, preferred_element_type=F32) + rb_ref[...]
    lane = lax.broadcasted_iota(I32, lg.shape, 1)
    m1 = jnp.max(lg, axis=1, keepdims=True)
    i1 = jnp.min(jnp.where(lg == m1, lane, LANE), axis=1, keepdims=True)
    lg2 = jnp.where(lane == i1, NEG_BIG * 2, lg)
    m2 = jnp.max(lg2, axis=1, keepdims=True)
    i2 = jnp.min(jnp.where(lg2 == m2, lane, LANE), axis=1, keepdims=True)
    e = jnp.exp(m2 - m1)
    p1 = 1.0 / (1.0 + e)
    p2 = e / (1.0 + e)
    return jnp.where(lane == 0, p1, jnp.where(lane == 1, p2, jnp.where(
        lane == 2, i1.astype(F32), jnp.where(lane == 3, i2.astype(F32), 0.0))))


def _post_body(*refs, npt, reps, res, nxt, router):
    it = iter(refs)
    x_ref = next(it)
    y_ref = next(it) if res else None
    gres_ref = next(it) if res else None
    gnx_ref = next(it) if nxt else None
    mpr_ref = next(it) if res else None
    mpn_ref = next(it) if nxt else None
    msg_ref = next(it) if res else None
    msc_ref = next(it) if nxt else None
    msh_ref = next(it) if nxt else None
    rw_ref = next(it) if router else None
    rb_ref = next(it) if router else None
    xo_ref = next(it) if res else None
    hh_ref = next(it) if nxt else None
    rt_ref = next(it) if router else None
    i = pl.program_id(0)

    def compute(gate, sc, sh):
        x = x_ref[...]
        if res:
            x = x + gate * _rms(y_ref[...], gres_ref[...])
            xo_ref[...] = x
        if nxt:
            hh = _rms(x, gnx_ref[...]) * (1.0 + sc) + sh
            hh_ref[...] = hh.astype(hh_ref.dtype)
            if router:
                rt_ref[...] = _route(hh, rw_ref, rb_ref)

    def rep(r):
        return jnp.concatenate([r[...]] * reps, axis=0) if reps > 1 else r[...]

    @pl.when(i < npt)
    def _():
        compute(mpr_ref[res[2]:res[2] + 1, :] if res else None,
                mpn_ref[nxt[2]:nxt[2] + 1, :] if nxt else None,
                mpn_ref[nxt[3]:nxt[3] + 1, :] if nxt else None)

    @pl.when(i >= npt)
    def _():
        compute(rep(msg_ref) if res else None, rep(msc_ref) if nxt else None,
                rep(msh_ref) if nxt else None)


def _post(x, y, cfg, *, res, nxt, router=None, hh_dtype=BF16, name):
    t_rows, d = x.shape
    te, bp, bs = TE, cfg["bp"], cfg["bs"]
    npt, nt, tpb = cfg["tp"] // te, t_rows // te, cfg["seq"] // te
    assert te % bs == 0 and cfg["seq"] % te == 0 and t_rows % te == 0
    row = pl.BlockSpec((te, d), lambda i: (i, 0))
    in_specs, args = [row], [x]
    if res:
        in_specs.append(row)
        args.append(y)

    def gspec(layer, r):
        return pl.BlockSpec((None, 1, d), lambda i: (layer * 4 + r, 0, 0))

    def mpspec(layer):
        return pl.BlockSpec((None, N_MOD, d), lambda i: (layer * bp + jnp.minimum(i // tpb, bp - 1), 0, 0))

    def msspec(layer, plane):
        return pl.BlockSpec((None, bs, d), lambda i: (layer * N_MOD + plane, 0, 0))

    if res:
        in_specs.append(gspec(res[0], res[1])); args.append(cfg["norm_g"])
    if nxt:
        in_specs.append(gspec(nxt[0], nxt[1])); args.append(cfg["norm_g"])
    if res:
        in_specs.append(mpspec(res[0])); args.append(cfg["mod_p"])
    if nxt:
        in_specs.append(mpspec(nxt[0])); args.append(cfg["mod_p"])
    if res:
        in_specs.append(msspec(res[0], res[2])); args.append(cfg["mod_s"])
    if nxt:
        in_specs.append(msspec(nxt[0], nxt[2])); args.append(cfg["mod_s"])
        in_specs.append(msspec(nxt[0], nxt[3])); args.append(cfg["mod_s"])
    if router:
        rw, rb, slab = router
        in_specs.append(pl.BlockSpec((None, d, LANE), lambda i: (slab, 0, 0))); args.append(rw)
        in_specs.append(pl.BlockSpec((None, 1, LANE), lambda i: (slab, 0, 0))); args.append(rb)
    out_specs, out_shape = [], []
    if res:
        out_specs.append(row); out_shape.append(jax.ShapeDtypeStruct((t_rows, d), F32))
    if nxt:
        out_specs.append(row); out_shape.append(jax.ShapeDtypeStruct((t_rows, d), hh_dtype))
    if router:
        out_specs.append(pl.BlockSpec((te, LANE), lambda i: (i, 0)))
        out_shape.append(jax.ShapeDtypeStruct((t_rows, LANE), F32))
    body = functools.partial(_post_body, npt=npt, reps=te // bs, res=res, nxt=nxt, router=bool(router))
    return pl.pallas_call(body, grid=(nt,), in_specs=in_specs, out_specs=out_specs, out_shape=out_shape,
                          compiler_params=_params(("arbitrary",)), name=name)(*args)


def _row_copy(src_hbm, row, dst_ref, r, sem):
    return pltpu.make_async_copy(src_hbm.at[pl.ds(row, 1), :], dst_ref.at[pl.ds(r, 1), :], sem)


def _gather_rows(idx_ref, base, src_hbm, dst_ref, sem, n):
    def start(r, c):
        _row_copy(src_hbm, idx_ref[base + r], dst_ref, r, sem).start()
        return c

    def wait(r, c):
        _row_copy(src_hbm, 0, dst_ref, r, sem).wait()
        return c

    lax.fori_loop(0, n, start, 0, unroll=DMA_ISSUE_UNROLL)
    lax.fori_loop(0, n, wait, 0, unroll=DMA_ISSUE_UNROLL)


def _dispatch_body(src_ref, nused_ref, x_hbm, o_ref, buf_ref, sem, *, tm):
    i = pl.program_id(0)

    @pl.when(i < nused_ref[0])
    def _():
        _gather_rows(src_ref, i * tm, x_hbm, buf_ref, sem, tm)
        o_ref[...] = buf_ref[...].astype(o_ref.dtype)

    @pl.when(i >= nused_ref[0])
    def _():
        o_ref[...] = jnp.zeros_like(o_ref)


def _dispatch(hh, src, nused, *, tm, name):
    cap = src.shape[0]
    d = hh.shape[1]
    return pl.pallas_call(
        functools.partial(_dispatch_body, tm=tm),
        grid_spec=pltpu.PrefetchScalarGridSpec(
            num_scalar_prefetch=2, grid=(cap // tm,),
            in_specs=[pl.BlockSpec(memory_space=pl.ANY)],
            out_specs=pl.BlockSpec((tm, d), lambda i, s, n: (i, 0)),
            scratch_shapes=[pltpu.VMEM((tm, d), hh.dtype), pltpu.SemaphoreType.DMA(())]),
        out_shape=jax.ShapeDtypeStruct((cap, d), BF16),
        compiler_params=_params(("arbitrary",)), name=name)(src, nused, hh)


def _combine_body(d0_ref, d1_ref, y_hbm, p_ref, o_ref, a_ref, b_ref, sem_a, sem_b, *, te):
    i = pl.program_id(0)

    def start(r, c):
        _row_copy(y_hbm, d0_ref[i * te + r], a_ref, r, sem_a).start()
        _row_copy(y_hbm, d1_ref[i * te + r], b_ref, r, sem_b).start()
        return c

    def wait(r, c):
        _row_copy(y_hbm, 0, a_ref, r, sem_a).wait()
        _row_copy(y_hbm, 0, b_ref, r, sem_b).wait()
        return c

    lax.fori_loop(0, te, start, 0, unroll=DMA_ISSUE_UNROLL)
    lax.fori_loop(0, te, wait, 0, unroll=DMA_ISSUE_UNROLL)
    p = p_ref[...]
    o_ref[...] = p[:, 0:1] * a_ref[...] + p[:, 1:2] * b_ref[...]


def _combine(ys, dest0, dest1, route, *, name):
    t_rows = route.shape[0]
    d = ys.shape[1]
    te = TE
    return pl.pallas_call(
        functools.partial(_combine_body, te=te),
        grid_spec=pltpu.PrefetchScalarGridSpec(
            num_scalar_prefetch=2, grid=(t_rows // te,),
            in_specs=[pl.BlockSpec(memory_space=pl.ANY),
                      pl.BlockSpec((te, LANE), lambda i, a, b: (i, 0))],
            out_specs=pl.BlockSpec((te, d), lambda i, a, b: (i, 0)),
            scratch_shapes=[pltpu.VMEM((te, d), F32), pltpu.VMEM((te, d), F32),
                            pltpu.SemaphoreType.DMA(()), pltpu.SemaphoreType.DMA(())]),
        out_shape=jax.ShapeDtypeStruct((t_rows, d), F32),
        compiler_params=_params(("arbitrary",)), name=name)(dest0, dest1, ys, route)


def _dn_post(y, j):
    y = y * jax.nn.sigmoid(y)
    outs = []
    for h in range(y.shape[1] // DN_HEAD_DIM):
        yh = y[:, h * DN_HEAD_DIM:(h + 1) * DN_HEAD_DIM]
        n = lax.rsqrt(jnp.sum(yh * yh, axis=-1, keepdims=True) + L2_EPS)
        s = jnp.where(j == 0, n * (DN_HEAD_DIM ** -0.5), jnp.where(j == 1, n, 1.0))
        outs.append(yh * s)
    return jnp.concatenate(outs, axis=1)


def _conv_rows(x_ref, prev_ref, buf_ref, w, first):
    te = x_ref.shape[0]
    x = x_ref[...]
    buf_ref[0:SUBLANE, :] = jnp.where(first, 0.0, prev_ref[...])
    buf_ref[SUBLANE:, :] = x
    y = w[CONV_W - 1:CONV_W, :] * x
    for dlt in range(1, CONV_W):
        y = y + w[CONV_W - 1 - dlt:CONV_W - dlt, :] * buf_ref[SUBLANE - dlt:SUBLANE - dlt + te, :]
    return y


def _conv_slabs(x_ref, hist_ref, w, bs):
    xp = jnp.concatenate([hist_ref[...], x_ref[...]], axis=0)
    steps = x_ref.shape[0] // bs
    ys = []
    for t in range(steps):
        y = w[0:1, :] * xp[t * bs:(t + 1) * bs]
        for j in range(1, CONV_W):
            y = y + w[j:j + 1, :] * xp[(t + j) * bs:(t + j + 1) * bs]
        ys.append(y)
    return ys


def _dn_prep_body(x_ref, prev_ref, hist_ref, w_ref, o_ref, buf_ref, *, npt, tps, bs):
    i, j = pl.program_id(0), pl.program_id(1)

    @pl.when(i < npt)
    def _():
        y = _conv_rows(x_ref, prev_ref, buf_ref, w_ref[...], i % tps == 0)
        o_ref[...] = _dn_post(y, j)

    @pl.when(i >= npt)
    def _():
        ys = _conv_slabs(x_ref, hist_ref, w_ref[...], bs)
        o_ref[...] = _dn_post(jnp.concatenate(ys, axis=0), j)


def _dn_prep(qkv, hist_s, conv_w, layer, cfg):
    t_rows, tl, bs = cfg["tp"] + cfg["ts"], cfg["ts"], cfg["bs"]
    w3 = qkv.shape[1]
    cb = w3 // 3
    rpb = tl // SUBLANE
    return pl.pallas_call(
        functools.partial(_dn_prep_body, npt=cfg["tp"] // tl, tps=cfg["seq"] // tl, bs=bs),
        grid=(t_rows // tl, 3),
        in_specs=[pl.BlockSpec((tl, cb), lambda i, j: (i, j)),
                  pl.BlockSpec((SUBLANE, cb), lambda i, j: (jnp.maximum(i * rpb - 1, 0), j)),
                  pl.BlockSpec(((CONV_W - 1) * bs, cb), lambda i, j: (0, j)),
                  pl.BlockSpec((None, CONV_W, cb), lambda i, j: (layer, 0, j))],
        out_specs=pl.BlockSpec((tl, cb), lambda i, j: (i, j)),
        out_shape=jax.ShapeDtypeStruct((t_rows, w3), F32),
        scratch_shapes=[pltpu.VMEM((tl + SUBLANE, cb), F32)],
        compiler_params=_params(("arbitrary", "arbitrary")), name="dn_prep")(qkv, qkv, hist_s, conv_w)


def _dotf(a, b):
    return jnp.dot(a, b, precision=HI, preferred_element_type=F32)


def _dotb(a, b):
    return jnp.dot(a.astype(BF16), b.astype(BF16), preferred_element_type=F32)


def _dot_nt(a, b):
    return lax.dot_general(a, b, (((1,), (1,)), ((), ())), preferred_element_type=F32)


def _dot_tn(a, b):
    return lax.dot_general(a, b, (((0,), (0,)), ((), ())), preferred_element_type=F32)


def _dn_gates(ba, al, dt, heads, valid):
    lanel = lax.broadcasted_iota(I32, ba.shape, 1)
    beta = jnp.where(valid & (lanel < heads), jax.nn.sigmoid(ba), 0.0)
    g = jnp.where(valid, -jnp.exp(al) * jax.nn.softplus(ba + dt), 0.0)
    return beta, g


def _dn_wy_body(q_ref, k_ref, v_ref, ba_ref, al_ref, dt_ref,
                w_ref, u_ref, qg_ref, kd_ref, sc_ref, eg_ref, m_ref, r_ref):
    c = DN_CHUNK
    rows_n = q_ref.shape[0]
    heads = q_ref.shape[1] // DN_HEAD_DIM
    p2 = 2 * c
    beta_all, g_all = _dn_gates(ba_ref[...], al_ref[...], dt_ref[...], heads, True)
    rr = lax.broadcasted_iota(I32, (rows_n, rows_n), 0)
    cc = lax.broadcasted_iota(I32, (rows_n, rows_n), 1)
    gcum_all = _dotf(((rr >= cc) & (rr // c == cc // c)).astype(F32), g_all)
    row = lax.broadcasted_iota(I32, (p2, p2), 0)
    col = lax.broadcasted_iota(I32, (p2, p2), 1)
    same = (row // c) == (col // c)
    tri = (row >= col) & same
    strict = (row > col) & same

    eye = (row == col).astype(F32)
    blk = (row // SUBLANE) == (col // SUBLANE)

    def stack(ref, rs, h):
        return jnp.concatenate([ref[rs, (h + a) * DN_HEAD_DIM:(h + a + 1) * DN_HEAD_DIM] for a in (0, 1)], axis=0)

    def cols(arr, rs, lane0):
        return jnp.concatenate([arr[rs, lane0 + a:lane0 + a + 1] for a in (0, 1)], axis=0)

    probs = [(ci, h) for ci in range(rows_n // c) for h in range(0, heads, 2)]
    n_p = len(probs)

    for p, (ci, h) in enumerate(probs):
        rs = slice(ci * c, (ci + 1) * c)
        q, k = stack(q_ref, rs, h), stack(k_ref, rs, h)
        beta = cols(beta_all, rs, h)
        gcol = cols(gcum_all, rs, heads + h)
        glast = jnp.concatenate(
            [jnp.broadcast_to(gcum_all[(ci + 1) * c - 1:(ci + 1) * c, heads + h + a:heads + h + a + 1], (c, 1))
             for a in (0, 1)], axis=0)
        gmat = jnp.broadcast_to(gcol, (p2, p2))
        decay = jnp.exp(jnp.where(tri, gmat - gmat.T, NEG_BIG))
        kb = k * beta
        k16 = k.astype(BF16)
        lm = jnp.where(strict, _dot_nt(kb.astype(BF16), k16) * decay, 0.0)
        m_ref[0, p] = lm
        m_ref[1, p] = jnp.where(blk, lm, 0.0)
        sc = _dot_nt(q.astype(BF16), k16) * decay
        egc = jnp.exp(gcol)
        r_ref[0, p] = (kb * egc).astype(BF16)
        r_ref[1, p] = (stack(v_ref, rs, h) * beta).astype(BF16)
        qg = q * egc
        kd = k * jnp.exp(glast - gcol)
        egl = jnp.exp(glast)
        for a in (0, 1):
            hs = slice(a * c, (a + 1) * c)
            qg_ref[ci, h + a] = qg[hs].astype(qg_ref.dtype)
            kd_ref[ci, h + a] = kd[hs].astype(kd_ref.dtype)
            sc_ref[ci, h + a] = sc[hs, hs].astype(sc_ref.dtype)
            eg_ref[ci, h + a] = jnp.broadcast_to(egl[a * c:a * c + 1], (1, LANE))

    def stage(dst, fa, fb):
        for p in range(n_p):
            m_ref[dst, p] = _dotb(fa(p), fb(p))

    ld = lambda p: m_ref[1, p]
    stage(2, ld, ld)
    stage(3, lambda p: m_ref[2, p], lambda p: m_ref[2, p])
    stage(4, lambda p: eye - ld(p), lambda p: eye + m_ref[2, p])
    stage(5, lambda p: m_ref[4, p], lambda p: eye + m_ref[3, p])
    stage(2, lambda p: m_ref[5, p], lambda p: m_ref[0, p] - ld(p))
    stage(3, lambda p: m_ref[2, p], lambda p: m_ref[2, p])
    stage(4, lambda p: m_ref[3, p], lambda p: m_ref[3, p])
    stage(6, lambda p: eye - m_ref[2, p], lambda p: eye + m_ref[3, p])
    stage(7, lambda p: m_ref[6, p], lambda p: eye + m_ref[4, p])
    stage(2, lambda p: m_ref[7, p], lambda p: m_ref[5, p])
    for p, (ci, h) in enumerate(probs):
        t16 = m_ref[2, p].astype(BF16)
        wm = jnp.dot(t16, r_ref[0, p], preferred_element_type=F32)
        um = jnp.dot(t16, r_ref[1, p], preferred_element_type=F32)
        for a in (0, 1):
            hs = slice(a * c, (a + 1) * c)
            w_ref[ci, h + a] = wm[hs].astype(w_ref.dtype)
            u_ref[ci, h + a] = um[hs]


def _dn_seq_body(w_ref, u_ref, qg_ref, kd_ref, sc_ref, eg_ref, z_ref, ng_ref, o_ref, so_ref, s_ref, r_ref):
    nck, heads, c = w_ref.shape[0], w_ref.shape[1], w_ref.shape[2]
    ci = pl.program_id(1)

    @pl.when(ci == 0)
    def _():
        s_ref[...] = jnp.zeros_like(s_ref)

    def chunk(n, carry):
        rs = pl.ds(pl.multiple_of(n * c, c), c)
        for h in range(heads):
            r_ref[h] = jnp.dot(jnp.concatenate([w_ref[n, h], qg_ref[n, h]], axis=0),
                               s_ref[h].astype(BF16), preferred_element_type=F32)
        for h in range(heads):
            hs = slice(h * DN_HEAD_DIM, (h + 1) * DN_HEAD_DIM)
            vn16 = (u_ref[n, h] - r_ref[h, 0:c]).astype(BF16)
            o = r_ref[h, c:2 * c] + jnp.dot(sc_ref[n, h], vn16, preferred_element_type=F32)
            s_ref[h] = s_ref[h] * eg_ref[n, h] + _dot_tn(kd_ref[n, h], vn16)
            o_ref[rs, hs] = (_rms(o, ng_ref[...]) * z_ref[rs, hs]).astype(o_ref.dtype)
        return carry

    lax.fori_loop(0, nck, chunk, 0)

    @pl.when(ci == pl.num_programs(1) - 1)
    def _():
        so_ref[...] = s_ref[...]


DN_WY_CHUNKS = 4
DN_SEQ_CHUNKS = 8


def _dn_prompt(qkvp, z, ba, al, dt, ng, layer, cfg):
    tp, seq, bp = cfg["tp"], cfg["seq"], cfg["bp"]
    w = qkvp.shape[1] // 3
    heads = w // DN_HEAD_DIM
    c, d = DN_CHUNK, DN_HEAD_DIM
    nc = tp // c
    kw, ks = DN_WY_CHUNKS, DN_SEQ_CHUNKS
    assert nc % kw == 0 and (seq // c) % ks == 0 and heads % 2 == 0
    vec = pl.BlockSpec((None, 1, LANE), lambda *_: (layer, 0, 0))
    hm = lambda n, last: pl.BlockSpec((n, heads, c, last), lambda i: (i, 0, 0, 0))
    eg_spec = lambda n: pl.BlockSpec((n, heads, 1, LANE), lambda i: (i, 0, 0, 0))
    sds = jax.ShapeDtypeStruct
    wm, um, qg, kd, sc, eg = pl.pallas_call(
        _dn_wy_body, grid=(nc // kw,),
        in_specs=[pl.BlockSpec((kw * c, w), lambda i: (i, 0)), pl.BlockSpec((kw * c, w), lambda i: (i, 1)),
                  pl.BlockSpec((kw * c, w), lambda i: (i, 2)), pl.BlockSpec((kw * c, LANE), lambda i: (i, 0)),
                  vec, vec],
        out_specs=[hm(kw, d), hm(kw, d), hm(kw, d), hm(kw, d), hm(kw, c), eg_spec(kw)],
        out_shape=[sds((nc, heads, c, d), BF16), sds((nc, heads, c, d), F32), sds((nc, heads, c, d), BF16),
                   sds((nc, heads, c, d), BF16), sds((nc, heads, c, c), BF16), sds((nc, heads, 1, LANE), F32)],
        scratch_shapes=[pltpu.VMEM((8, kw * heads // 2, 2 * c, 2 * c), F32),
                        pltpu.VMEM((2, kw * heads // 2, 2 * c, d), BF16)],
        compiler_params=_params(("arbitrary",)), name="dn_wy")(qkvp, qkvp, qkvp, ba, al, dt)
    nps = seq // c // ks
    hm2 = lambda last: pl.BlockSpec((ks, heads, c, last), lambda b, n: (b * nps + n, 0, 0, 0))
    return pl.pallas_call(
        _dn_seq_body, grid=(bp, nps),
        in_specs=[hm2(d), hm2(d), hm2(d), hm2(d), hm2(c),
                  pl.BlockSpec((ks, heads, 1, LANE), lambda b, n: (b * nps + n, 0, 0, 0)),
                  pl.BlockSpec((ks * c, w), lambda b, n: (b * nps + n, 0)),
                  pl.BlockSpec((None, 1, LANE), lambda b, n: (layer, 0, 0))],
        out_specs=[pl.BlockSpec((ks * c, w), lambda b, n: (b * nps + n, 0)),
                   pl.BlockSpec((None, heads, d, d), lambda b, n: (b, 0, 0, 0))],
        out_shape=[sds((tp, w), BF16), sds((bp, heads, d, d), F32)],
        scratch_shapes=[pltpu.VMEM((heads, d, d), F32), pltpu.VMEM((heads, 2 * c, d), F32)],
        compiler_params=_params(("arbitrary", "arbitrary")), name="dn_seq")(wm, um, qg, kd, sc, eg, z, ng)


def _dn_chunk_body(q_ref, k_ref, v_ref, z_ref, ba_ref, s0_ref, al_ref, dt_ref, ng_ref, chain_ref,
                   o_ref, so_ref, m_ref, sc_ref, ws_ref, kd_ref, vn_ref, *, groups, n_valid):
    c = q_ref.shape[0]
    heads = q_ref.shape[1] // DN_HEAD_DIM
    rpg = c // groups
    assert rpg == SUBLANE

    row = lax.broadcasted_iota(I32, (c, c), 0)
    col = lax.broadcasted_iota(I32, (c, c), 1)
    same = (row // rpg) == (col // rpg)
    tri = (row >= col) & same
    strict = (row > col) & same
    rowl = lax.broadcasted_iota(I32, (c, LANE), 0)
    beta_all, g_all = _dn_gates(ba_ref[...], al_ref[...], dt_ref[...], heads, (rowl % rpg) < n_valid)
    gcum_all = _dotf(tri.astype(F32), g_all)
    rg = lax.broadcasted_iota(I32, (c, 1), 0) // rpg
    rg2 = jnp.concatenate([rg, rg], axis=0)

    eye = (row == col).astype(F32)
    glast = lambda h, gi: gcum_all[(gi + 1) * rpg - 1:(gi + 1) * rpg, heads + h:heads + h + 1]

    for h in range(heads):
        hs = slice(h * DN_HEAD_DIM, (h + 1) * DN_HEAD_DIM)
        q, k = q_ref[:, hs], k_ref[:, hs]
        beta = beta_all[:, h:h + 1]
        gcol = gcum_all[:, heads + h:heads + h + 1]
        gmat = jnp.broadcast_to(
            jnp.concatenate([gcol, jnp.zeros((LANE - c, 1), F32)], axis=0), (LANE, LANE))
        decay = jnp.exp(jnp.where(tri, gmat[0:c, 0:c] - gmat.T[0:c, 0:c], NEG_BIG))
        kb = k * beta
        k16 = k.astype(BF16)
        m_ref[0, h] = jnp.where(strict, _dot_nt(kb.astype(BF16), k16) * decay, 0.0)
        sc_ref[h] = (_dot_nt(q.astype(BF16), k16) * decay).astype(BF16)
        egc = jnp.exp(gcol)
        lhs = jnp.concatenate([kb * egc, q * egc], axis=0).astype(BF16)
        ws = jnp.zeros((2 * c, DN_HEAD_DIM), F32)
        glast_rows = jnp.zeros((c, 1), F32)
        for gi in range(groups):
            r = jnp.dot(lhs, s0_ref[gi, h].astype(BF16), preferred_element_type=F32)
            ws = jnp.where(rg2 == gi, r, ws)
            glast_rows = jnp.where(rg == gi, glast(h, gi), glast_rows)
        ws_ref[h] = ws
        kd_ref[h] = (k * jnp.exp(glast_rows - gcol)).astype(BF16)

    def stage(dst, fa, fb):
        for h in range(heads):
            m_ref[dst, h] = _dotb(fa(h), fb(h))

    stage(1, lambda h: m_ref[0, h], lambda h: m_ref[0, h])
    stage(2, lambda h: m_ref[1, h], lambda h: m_ref[1, h])
    stage(3, lambda h: eye - m_ref[0, h], lambda h: eye + m_ref[1, h])
    stage(1, lambda h: m_ref[3, h], lambda h: eye + m_ref[2, h])
    for h in range(heads):
        hs = slice(h * DN_HEAD_DIM, (h + 1) * DN_HEAD_DIM)
        vn_ref[h] = _dotb(m_ref[1, h], v_ref[:, hs] * beta_all[:, h:h + 1] - ws_ref[h, 0:c]).astype(BF16)
    for h in range(heads):
        hs = slice(h * DN_HEAD_DIM, (h + 1) * DN_HEAD_DIM)
        vn16 = vn_ref[h]
        o = ws_ref[h, c:2 * c] + jnp.dot(sc_ref[h], vn16, preferred_element_type=F32)
        kdec = kd_ref[h]
        for gi in range(groups):
            upd = _dot_tn(jnp.where(rg == gi, kdec, jnp.zeros_like(kdec)), vn16)
            so_ref[gi, h] = s0_ref[gi, h] * jnp.exp(glast(h, gi)) + upd
        o_ref[:, hs] = (_rms(o, ng_ref[...]) * z_ref[:, hs]).astype(o_ref.dtype)


def _dn_sample(qkv, z, ba, state, chain, al, dt, ng, layer, *, n_valid):
    rows, w3 = qkv.shape
    w = w3 // 3
    heads = w // DN_HEAD_DIM
    c = DN_CHUNK
    groups = c // SUBLANE
    rspec = lambda col: pl.BlockSpec((c, w), lambda b: (b, col))
    vec = pl.BlockSpec((None, 1, LANE), lambda b: (layer, 0, 0))
    sspec = pl.BlockSpec((None, groups, heads, DN_HEAD_DIM, DN_HEAD_DIM), lambda b: (layer, b, 0, 0, 0))
    return pl.pallas_call(
        functools.partial(_dn_chunk_body, groups=groups, n_valid=n_valid),
        grid=(rows // c,),
        in_specs=[rspec(0), rspec(1), rspec(2), rspec(0), pl.BlockSpec((c, LANE), lambda b: (b, 0)),
                  sspec, vec, vec, vec, pl.BlockSpec(memory_space=pl.ANY)],
        out_specs=[rspec(0), sspec],
        out_shape=[jax.ShapeDtypeStruct((rows, w), BF16), jax.ShapeDtypeStruct(chain.shape, F32)],
        scratch_shapes=[pltpu.VMEM((4, heads, c, c), F32), pltpu.VMEM((heads, c, c), BF16),
                        pltpu.VMEM((heads, 2 * c, DN_HEAD_DIM), F32), pltpu.VMEM((heads, c, DN_HEAD_DIM), BF16),
                        pltpu.VMEM((heads, c, DN_HEAD_DIM), BF16)],
        input_output_aliases={9: 1},
        compiler_params=_params(("arbitrary",)), name="dn_sample")(
            qkv, qkv, qkv, z, ba, state, al, dt, ng, chain)


def _layer_norm(v, g, b):
    mu = jnp.mean(v, axis=-1, keepdims=True)
    vc = v - mu
    return vc * lax.rsqrt(jnp.mean(vc * vc, axis=-1, keepdims=True) + LN_EPS) * g + b


def _gmlp_body(uv_ref, g_ref, b_ref, ws_ref, bst_ref, cw_ref, cb_ref, o_ref, vn_ref, *, npt, bs):
    w = uv_ref.shape[1] // 2
    i = pl.program_id(0)
    vn = _layer_norm(uv_ref[:, w:].astype(F32), g_ref[...], b_ref[...])

    @pl.when(i < npt)
    def _():
        groups = ws_ref.shape[0]
        gw = w // groups
        v16 = vn.astype(BF16)
        row = lax.broadcasted_iota(I32, (GM_CHUNK, GM_CHUNK), 0)
        col = lax.broadcasted_iota(I32, (GM_CHUNK, GM_CHUNK), 1)
        wcs = [jnp.where(row >= col, ws_ref[g], 0.0).astype(BF16) for g in range(groups)]
        bst = bst_ref[...]
        for c in range(uv_ref.shape[0] // GM_CHUNK):
            rs = slice(c * GM_CHUNK, (c + 1) * GM_CHUNK)
            for g in range(groups):
                cs = slice(g * gw, (g + 1) * gw)
                s = jnp.dot(wcs[g], v16[rs, cs], preferred_element_type=F32) + bst[:, g:g + 1]
                o_ref[rs, cs] = (uv_ref[rs, cs] * s).astype(o_ref.dtype)

    @pl.when(i >= npt)
    def _():
        steps = uv_ref.shape[0] // bs
        vn_ref[...] = vn
        for t in range(steps):
            s = cb_ref[t:t + 1, :]
            for u in range(t + 1):
                s = s + cw_ref[t * steps + u:t * steps + u + 1, :] * vn[u * bs:(u + 1) * bs]
            o_ref[t * bs:(t + 1) * bs, :] = (uv_ref[t * bs:(t + 1) * bs, 0:w] * s).astype(o_ref.dtype)


def _gmlp(uv, ln_g, ln_b, ws, bst, cw, cb, layer, cfg):
    tl, bs = cfg["ts"], cfg["bs"]
    t_rows, w2 = uv.shape
    w = w2 // 2
    groups = ws.shape[1]
    steps = tl // bs
    vec = pl.BlockSpec((None, 1, w), lambda i: (layer, 0, 0))
    return pl.pallas_call(
        functools.partial(_gmlp_body, npt=cfg["tp"] // tl, bs=bs), grid=(t_rows // tl,),
        in_specs=[pl.BlockSpec((tl, w2), lambda i: (i, 0)), vec, vec,
                  pl.BlockSpec((None, groups, GM_CHUNK, GM_CHUNK), lambda i: (layer, 0, 0, 0)),
                  pl.BlockSpec((None, GM_CHUNK, LANE), lambda i: (layer, 0, 0)),
                  pl.BlockSpec((None, steps * steps, w), lambda i: (layer, 0, 0)),
                  pl.BlockSpec((None, SUBLANE, w), lambda i: (layer, 0, 0))],
        out_specs=[pl.BlockSpec((tl, w), lambda i: (i, 0)), pl.BlockSpec((tl, w), lambda i: (0, 0))],
        out_shape=[jax.ShapeDtypeStruct((t_rows, w), BF16), jax.ShapeDtypeStruct((tl, w), F32)],
        compiler_params=_params(("arbitrary",)), name="gmlp")(uv, ln_g, ln_b, ws, bst, cw, cb)


def _lru_gates(xc, wa_ref, ba, wx_ref, bx, lam):
    nb = wa_ref.shape[0]
    bw = xc.shape[1] // nb
    x16 = xc.astype(BF16)
    rs, gs = [], []
    for h in range(nb):
        xs = x16[:, h * bw:(h + 1) * bw]
        rs.append(jnp.dot(xs, wa_ref[h].astype(BF16), preferred_element_type=F32))
        gs.append(jnp.dot(xs, wx_ref[h].astype(BF16), preferred_element_type=F32))
    r = jax.nn.sigmoid(jnp.concatenate(rs, axis=1) + ba)
    ig = jax.nn.sigmoid(jnp.concatenate(gs, axis=1) + bx)
    log_a = -LRU_C * r * jax.nn.softplus(-lam)
    th = jnp.tanh(log_a)
    return jnp.exp(log_a), jnp.sqrt(-2.0 * th / (1.0 - th)), ig


def _lru_body(x_ref, prev_ref, y_ref, hist_ref, h0_ref, cw_ref, cb_ref, wa_ref, ba_ref, wx_ref, bx_ref,
              lam_ref, o_ref, hlp_ref, hls_ref, buf_ref, a_ref, b_ref, h_ref, *, npt, tps, bs, pos0):
    te, w = x_ref.shape
    i = pl.program_id(0)

    def gates(xc):
        return _lru_gates(xc, wa_ref, ba_ref[...], wx_ref, bx_ref[...], lam_ref[...])

    @pl.when(i < npt)
    def _():
        first = i % tps == 0
        xc = _conv_rows(x_ref, prev_ref, buf_ref, cw_ref[...], first) + cb_ref[...]
        a, mult, ig = gates(xc)
        reset = first & (lax.broadcasted_iota(I32, (te, w), 0) == 0)
        a_ref[...] = jnp.where(reset, 0.0, a)
        b_ref[...] = jnp.where(reset, 1.0, mult) * ig * xc

        @pl.when(first)
        def _():
            h_ref[...] = jnp.zeros_like(h_ref)

        sub = lax.broadcasted_iota(I32, (SUBLANE, w), 0)

        def slab(s, h):
            rs = pl.ds(pl.multiple_of(s * SUBLANE, SUBLANE), SUBLANE)
            av, bv = a_ref[rs, :], b_ref[rs, :]
            for dlt in (1, 2, 4):
                ash, bsh = pltpu.roll(av, dlt, 0), pltpu.roll(bv, dlt, 0)
                m = sub >= dlt
                bv = jnp.where(m, av * bsh + bv, bv)
                av = jnp.where(m, av * ash, av)
            hs = av * h + bv
            b_ref[rs, :] = hs
            return hs[SUBLANE - 1:SUBLANE, :]

        h_last = lax.fori_loop(0, te // SUBLANE, slab, h_ref[...])
        h_ref[...] = h_last
        hlp_ref[...] = h_last
        o_ref[...] = (y_ref[...] * b_ref[...]).astype(o_ref.dtype)

    @pl.when(i >= npt)
    def _():
        ys = _conv_slabs(x_ref, hist_ref, cw_ref[...], bs)
        h = h0_ref[...]
        for t, y in enumerate(ys):
            xc = y + cb_ref[...]
            a, mult, ig = gates(xc)
            if pos0 + t == 0:
                a, mult = jnp.zeros_like(a), jnp.ones_like(mult)
            h = a * h + mult * ig * xc
            o_ref[t * bs:(t + 1) * bs, :] = (y_ref[t * bs:(t + 1) * bs, :] * h).astype(o_ref.dtype)
        hls_ref[...] = h


def _lru(lx, ly, hist_s, h0_s, p, layer, cfg):
    tl, bs, bp = cfg["ts"], cfg["bs"], cfg["bp"]
    t_rows, w = lx.shape
    nb = p["lru_wa"].shape[1]
    bw = w // nb
    npt, tps = cfg["tp"] // tl, cfg["seq"] // tl
    rpb = tl // SUBLANE
    vec = pl.BlockSpec((None, 1, w), lambda i: (layer, 0, 0))
    blk = pl.BlockSpec((None, nb, bw, bw), lambda i: (layer, 0, 0, 0))
    row = pl.BlockSpec((tl, w), lambda i: (i, 0))
    oc, hl_p, hl_s = pl.pallas_call(
        functools.partial(_lru_body, npt=npt, tps=tps, bs=bs, pos0=cfg["pos0_s"]), grid=(t_rows // tl,),
        in_specs=[row, pl.BlockSpec((SUBLANE, w), lambda i: (jnp.maximum(i * rpb - 1, 0), 0)), row,
                  pl.BlockSpec(((CONV_W - 1) * bs, w), lambda i: (0, 0)),
                  pl.BlockSpec((None, bs, w), lambda i: (layer, 0, 0)),
                  pl.BlockSpec((None, CONV_W, w), lambda i: (layer, 0, 0)), vec, blk, vec, blk, vec, vec],
        out_specs=[row, pl.BlockSpec((None, 1, w), lambda i: (jnp.minimum(i // tps, bp - 1), 0, 0)),
                   pl.BlockSpec((bs, w), lambda i: (0, 0))],
        out_shape=[jax.ShapeDtypeStruct((t_rows, w), BF16), jax.ShapeDtypeStruct((bp, 1, w), F32),
                   jax.ShapeDtypeStruct((bs, w), F32)],
        scratch_shapes=[pltpu.VMEM((tl + SUBLANE, w), F32), pltpu.VMEM((tl, w), F32),
                        pltpu.VMEM((tl, w), F32), pltpu.VMEM((1, w), F32)],
        compiler_params=_params(("arbitrary",)), name="lru")(
            lx, lx, ly, hist_s, h0_s, p["lru_conv_w"], p["lru_conv_b"], p["lru_wa"], p["lru_ba"],
            p["lru_wx"], p["lru_bx"], p["lru_lambda"])
    return oc, hl_p.reshape(bp, w), hl_s


def _time_major(a):
    return jnp.swapaxes(a, 0, 1).reshape(a.shape[0] * a.shape[1], a.shape[2])


def _batch_major(a, bs):
    return jnp.swapaxes(a.reshape(a.shape[0] // bs, bs, a.shape[1]), 0, 1)


def _moe_plan(route, n_exp, tm):
    t_rows = route.shape[0]
    flat_e = route[:, 2:2 + TOP_K].astype(I32).reshape(-1)
    oh = (flat_e[:, None] == jnp.arange(n_exp, dtype=I32)[None, :]).astype(I32)
    csum = jnp.cumsum(oh, axis=0)
    rank = jnp.take_along_axis(csum, flat_e[:, None], axis=1)[:, 0] - 1
    counts = csum[-1]
    tiles_e = (counts + tm - 1) // tm
    tile_end = jnp.cumsum(tiles_e)
    dest = ((tile_end - tiles_e) * tm)[flat_e] + rank
    n_tiles = pl.cdiv(TOP_K * t_rows, tm) + n_exp
    src = jnp.zeros((n_tiles * tm,), I32).at[dest].set(jnp.arange(TOP_K * t_rows, dtype=I32) // TOP_K)
    tile_e = jnp.sum((jnp.arange(n_tiles, dtype=I32)[:, None] >= tile_end[None, :]).astype(I32), axis=1)
    tile_e = jnp.minimum(tile_e, n_exp - 1)
    dest = dest.reshape(t_rows, TOP_K)
    return dest[:, 0], dest[:, 1], src, tile_e, tile_end[-1].astype(I32)


def kernel(x_prompt, x_sample, state_dn_S, state_dn_conv, state_lru_h, state_lru_conv, c_prompt, c_sample,
           w_mod, b_mod, norm_g, w_in, dn_conv_w, dn_a_log, dn_dt_bias, dn_norm_g, gm_ln_g, gm_ln_b, gm_ws,
           gm_bs, lru_conv_w, lru_conv_b, lru_wa, lru_ba, lru_wx, lru_bx, lru_lambda, w_branch, w_out,
           ffn_w_gate, ffn_w_up, ffn_w_down, router_w, router_b, moe_w_gate, moe_w_up, moe_w_down):
    bp, seq, d = x_prompt.shape
    bs, dec = x_sample.shape[0], x_sample.shape[1]
    depth = w_mod.shape[0]
    w = d // 2
    heads = w // DN_HEAD_DIM
    n_exp = moe_w_gate.shape[1]
    tp, ts = bp * seq, bs * dec
    t_rows = tp + ts
    past_len = 16384
    assert dec <= SUBLANE and dec >= CONV_W - 1 and ts % GM_CHUNK == 0 and seq % ts == 0
    assert tp % ts == 0 and ts % TE == 0 and bs % SUBLANE == 0 and t_rows % (16 * 2 * DENSE_ROW_TILES) == 0
    cfg = dict(bp=bp, bs=bs, seq=seq, tp=tp, ts=ts, pos0_s=past_len)
    tm_d = t_rows // DENSE_ROW_TILES

    def dense_meta(slab, ni=DENSE_ROW_TILES):
        return jnp.concatenate([jnp.full((ni,), slab, I32), jnp.array([ni], I32)])

    n_c = bp + bs
    n_c_pad = -(-n_c // SUBLANE) * SUBLANE
    c_all = jnp.concatenate([c_prompt, c_sample, jnp.zeros((n_c_pad - n_c, d), F32)], axis=0)
    b_mod3 = b_mod.reshape(depth, 1, N_MOD * d)
    mods = []
    for l in range(depth):
        meta = jnp.array([l, 1], I32)
        mods.append(_mm(c_all, w_mod, meta, name="mod", tm=n_c_pad, pre_act="silu", bias=b_mod3))
    mod = jnp.stack(mods).reshape(depth, n_c_pad, N_MOD, d)
    cfg["mod_p"] = mod[:, :bp].reshape(depth * bp, N_MOD, d)
    cfg["mod_s"] = jnp.swapaxes(mod[:, bp:n_c], 1, 2).reshape(depth * N_MOD, bs, d)
    cfg["norm_g"] = norm_g.reshape(depth * 4, 1, d)

    n_qkv, n_z = 3 * w, w
    c_ba = n_qkv + n_z
    c_rest = c_ba + 2 * heads
    assert c_rest % SUBLANE == 0
    w_in_t = jnp.swapaxes(w_in, 1, 2)
    lane_pad = lambda a, off: jnp.pad(a, ((0, 0), (off, LANE - off - a.shape[1]))).reshape(depth, 1, LANE)
    al_pad, dt_pad = lane_pad(dn_a_log, heads), lane_pad(dn_dt_bias, heads)
    dn_ng = dn_norm_g.reshape(depth, 1, DN_HEAD_DIM)
    vec3 = lambda a: a.reshape(depth, 1, a.shape[-1])
    lru_p = dict(lru_conv_w=lru_conv_w, lru_conv_b=vec3(lru_conv_b), lru_wa=lru_wa, lru_ba=vec3(lru_ba),
                 lru_wx=lru_wx, lru_bx=vec3(lru_bx), lru_lambda=vec3(lru_lambda))
    gm_g3, gm_b3 = vec3(gm_ln_g), vec3(gm_ln_b)
    gm_bst = jnp.pad(jnp.swapaxes(gm_bs, 1, 2), ((0, 0), (0, 0), (0, LANE - gm_bs.shape[1])))
    gw = w // gm_ws.shape[1]
    gm_cw = jnp.repeat(jnp.swapaxes(gm_ws[:, :, :dec, :dec], 1, 3).reshape(depth, dec, dec, -1), gw, axis=-1)
    gm_cw = jnp.swapaxes(gm_cw, 1, 2).reshape(depth, dec * dec, w)
    gm_cb = jnp.repeat(jnp.swapaxes(gm_bs[:, :, :dec], 1, 2), gw, axis=-1)
    gm_cb = jnp.pad(gm_cb, ((0, 0), (0, SUBLANE - dec), (0, 0)))
    wb3 = w_branch.reshape(depth * 3, w, d)
    n_moe = router_w.shape[0]
    rw_pad = jnp.pad(router_w, ((0, 0), (0, 0), (0, LANE - n_exp)))
    rb_pad = jnp.pad(router_b, ((0, 0), (0, LANE - n_exp)), constant_values=NEG_BIG).reshape(n_moe, 1, LANE)
    d_ff = ffn_w_gate.shape[2]
    moe_wg = moe_w_gate.reshape(n_moe * n_exp, d, d_ff)
    moe_wu = moe_w_up.reshape(n_moe * n_exp, d, d_ff)
    moe_wd = moe_w_down.reshape(n_moe * n_exp, d_ff, d)

    x = jnp.concatenate([x_prompt.reshape(tp, d), _time_major(x_sample)], axis=0)
    s_chain = jnp.zeros(state_dn_S.shape, F32)
    outs = {k: [] for k in ("S_p", "dc_p", "h_p", "lc_p", "dc_s", "h_s", "lc_s", "v_s")}

    def last_rows(a):
        return jnp.stack([a[(b + 1) * seq - (CONV_W - 1):(b + 1) * seq] for b in range(bp)])

    def to_blocks(a):
        a = jnp.pad(_batch_major(a, bs), ((0, 0), (0, SUBLANE - dec), (0, 0)))
        return a.reshape(bs * SUBLANE, a.shape[2])

    (hh,) = _post(x, None, cfg, res=None, nxt=(0, 0, 1, 0), name="pre0")
    for l in range(depth):
        meta = dense_meta(l)
        proj = functools.partial(_mm, hh, w_in_t, meta, w_nk=True, tm=tm_d, tn=TN_WIDE)
        qkv = proj(name="in_qkv", n=n_qkv)
        zs = proj(name="in_z", n_off=n_qkv, n=n_z, act="silu", out_dtype=BF16)
        ba = proj(name="in_ba", n_off=c_ba, n=LANE, tn=LANE)
        uv = proj(name="in_uv", n_off=c_rest, n=2 * w, act="gelu", out_dtype=BF16)
        lx = proj(name="in_lx", n_off=c_rest + 2 * w, n=w)
        ly = proj(name="in_ly", n_off=c_rest + 3 * w, n=w, act="gelu", out_dtype=BF16)
        gates = proj(name="in_merge", n_off=c_rest + 4 * w, n=3 * d, act="sigmoid", out_dtype=BF16)

        dn_hist = _time_major(state_dn_conv[l])
        qkvp = _dn_prep(qkv, dn_hist, dn_conv_w, l, cfg)
        oa_p, s_p = _dn_prompt(qkvp, zs, ba, al_pad, dt_pad, dn_ng, l, cfg)
        oa_sb, s_chain = _dn_sample(to_blocks(qkvp[tp:]), to_blocks(zs[tp:]), to_blocks(ba[tp:]),
                                    state_dn_S, s_chain, al_pad, dt_pad, dn_ng, l, n_valid=dec)
        oa = jnp.concatenate([oa_p, _time_major(oa_sb.reshape(bs, SUBLANE, w)[:, :dec])], axis=0)
        dc_hist = jnp.concatenate([dn_hist, qkv[tp:]], axis=0)

        ob, vn_s = _gmlp(uv, gm_g3, gm_b3, gm_ws, gm_bst, gm_cw, gm_cb, l, cfg)
        lru_hist = _time_major(state_lru_conv[l])
        oc, hl_p, hl_s = _lru(lx, ly, lru_hist, state_lru_h, lru_p, l, cfg)
        lc_hist = jnp.concatenate([lru_hist, lx[tp:]], axis=0)

        outs["S_p"].append(s_p)
        outs["dc_p"].append(last_rows(qkv))
        outs["h_p"].append(hl_p)
        outs["lc_p"].append(last_rows(lx))
        outs["dc_s"].append(_batch_major(dc_hist[dec * bs:], bs))
        outs["h_s"].append(hl_s)
        outs["lc_s"].append(_batch_major(lc_hist[dec * bs:], bs))
        outs["v_s"].append(_batch_major(vn_s, bs))

        m = _branch_mm(oa, ob, oc, gates, wb3, meta, name="branch", tm=tm_d)
        y = _mm(m, w_out, meta, name="w_out", tm=tm_d, tn=TN_WIDE)
        j = l // 2
        if l % 2 == 0:
            x, hh = _post(x, y, cfg, res=(l, 1, 2), nxt=(l, 2, 4, 3), name="post_mix")
            hid = _ffn_up(hh, ffn_w_gate, ffn_w_up, dense_meta(j), name="ffn_up", tm=tm_d)
            y = _mm(hid, ffn_w_down, dense_meta(j, 2 * DENSE_ROW_TILES), name="ffn_down", tm=tm_d // 2)
        else:
            x, hh32, route = _post(x, y, cfg, res=(l, 1, 2), nxt=(l, 2, 4, 3), router=(rw_pad, rb_pad, j),
                                   hh_dtype=F32, name="post_mix_r")
            d0, d1, src, tile_e, nused = _moe_plan(route, n_exp, TM)
            xs = _dispatch(hh32, src, nused.reshape(1), tm=TM, name="moe_dispatch")
            emeta = jnp.concatenate([j * n_exp + tile_e, nused.reshape(1)])
            hid = _ffn_up(xs, moe_wg, moe_wu, emeta, name="moe_up")
            ys = _mm(hid, moe_wd, emeta, name="moe_down")
            y = _combine(ys, d0, d1, route, name="moe_combine")
        if l + 1 < depth:
            x, hh = _post(x, y, cfg, res=(l, 3, 5), nxt=(l + 1, 0, 1, 0), name="post_ffn")
        else:
            (x,) = _post(x, y, cfg, res=(l, 3, 5), nxt=None, name="post_last")

    st = lambda k: jnp.stack(outs[k])
    return (x[:tp].reshape(bp, seq, d), _batch_major(x[tp:], bs),
            st("S_p"), st("dc_p"), st("h_p"), st("lc_p"),
            s_chain, st("dc_s"), st("h_s"), st("lc_s"), st("v_s"))
```

```python
import functools

import jax
import jax.numpy as jnp
from jax import lax
from jax.experimental import pallas as pl
from jax.experimental.pallas import tpu as pltpu

F32, BF16, I32 = jnp.float32, jnp.bfloat16, jnp.int32
HI = lax.Precision.HIGHEST

LANE = 128
SUBLANE = 8
VMEM_LIMIT = 56 << 20
DN_HEAD_DIM = 128
DN_CHUNK = 64
CONV_W = 4
GM_CHUNK = 128
LRU_BLOCKS = 8
LRU_C = 8.0
N_MOD = 6
TOP_K = 2
RMS_EPS = 1e-6
LN_EPS = 1e-5
L2_EPS = 1e-6
NEG_BIG = -1e30

TM = 512
TN = 512
TN_WIDE = 1024
DENSE_ROW_TILES = 8
TE = 256
DMA_ISSUE_UNROLL = 8


def _params(sem):
    return pltpu.CompilerParams(dimension_semantics=sem, vmem_limit_bytes=VMEM_LIMIT)


def _act(name, x):
    if name is None:
        return x
    if name == "silu":
        return x * jax.nn.sigmoid(x)
    if name == "gelu":
        return jax.nn.gelu(x)
    if name == "sigmoid":
        return jax.nn.sigmoid(x)
    raise ValueError(name)


def _rms(x, g):
    return x * lax.rsqrt(jnp.mean(x * x, axis=-1, keepdims=True) + RMS_EPS) * g


def _cast_once(meta_ref, i, pairs):
    prev = meta_ref[jnp.maximum(i - 1, 0)]

    @pl.when((i == 0) | (meta_ref[i] != prev))
    def _():
        for src, dst in pairs:
            dst[...] = src[...].astype(BF16)


def _mm_body(meta_ref, x_ref, w_ref, *rest, ni, act, pre_act, cast_w, has_bias, w_nk):
    rest = list(rest)
    b_ref = rest.pop(0) if has_bias else None
    o_ref = rest.pop(0)
    wbf_ref = rest.pop(0) if cast_w else None
    i = pl.program_id(1)
    nused = meta_ref[ni]

    @pl.when(i < nused)
    def _():
        if cast_w:
            _cast_once(meta_ref, i, [(w_ref, wbf_ref)])
            w = wbf_ref[...]
        else:
            w = w_ref[...]
        w = w[0] if w_nk else w
        x = x_ref[...]
        if pre_act is not None:
            x = _act(pre_act, x.astype(F32))
        dims = (((1,), (1,)), ((), ())) if w_nk else (((1,), (0,)), ((), ()))
        acc = lax.dot_general(x.astype(BF16), w, dims, preferred_element_type=F32)
        if has_bias:
            acc = acc + b_ref[...]
        o_ref[...] = _act(act, acc).astype(o_ref.dtype)

    @pl.when(i >= nused)
    def _():
        o_ref[...] = jnp.zeros_like(o_ref)


def _mm(x, w, meta, *, name, n_off=0, n=None, tm=TM, tn=TN, act=None, pre_act=None,
        bias=None, out_dtype=F32, w_nk=False):
    m_rows, k = x.shape
    n = w.shape[1 if w_nk else 2] if n is None else n
    tm = min(tm, m_rows)
    ni, nj = pl.cdiv(m_rows, tm), n // tn
    assert n % tn == 0 and meta.shape[0] == ni + 1
    cast_w = w.dtype != BF16
    body = functools.partial(_mm_body, ni=ni, act=act, pre_act=pre_act, cast_w=cast_w,
                             has_bias=bias is not None, w_nk=w_nk)
    if w_nk:
        assert n_off % SUBLANE == 0 and tn % SUBLANE == 0
        wspec = pl.BlockSpec((pl.Element(1), pl.Element(tn), pl.Element(k)),
                             lambda j, i, m: (m[i], (n_off // SUBLANE + j * (tn // SUBLANE)) * SUBLANE, 0))
    else:
        wspec = pl.BlockSpec((None, k, tn), lambda j, i, m: (m[i], 0, j + n_off))
    in_specs = [pl.BlockSpec((tm, k), lambda j, i, m: (jnp.minimum(i, m[ni] - 1), 0)), wspec]
    args = [x, w]
    if bias is not None:
        in_specs.append(pl.BlockSpec((None, 1, tn), lambda j, i, m: (m[i], 0, j + n_off)))
        args.append(bias)
    scratch = [pltpu.VMEM((1, tn, k) if w_nk else (k, tn), BF16)] if cast_w else []
    return pl.pallas_call(
        body,
        grid_spec=pltpu.PrefetchScalarGridSpec(
            num_scalar_prefetch=1, grid=(nj, ni), in_specs=in_specs,
            out_specs=pl.BlockSpec((tm, tn), lambda j, i, m: (i, j)), scratch_shapes=scratch),
        out_shape=jax.ShapeDtypeStruct((m_rows, n), out_dtype),
        compiler_params=_params(("arbitrary", "arbitrary")), name=name)(meta, *args)


def _ffn_up_body(meta_ref, x_ref, wg_ref, wu_ref, o_ref, wgb_ref, wub_ref, *, ni):
    i = pl.program_id(1)
    nused = meta_ref[ni]

    @pl.when(i < nused)
    def _():
        _cast_once(meta_ref, i, [(wg_ref, wgb_ref), (wu_ref, wub_ref)])
        x = x_ref[...]
        g = jnp.dot(x, wgb_ref[...], preferred_element_type=F32)
        u = jnp.dot(x, wub_ref[...], preferred_element_type=F32)
        o_ref[...] = (g * jax.nn.sigmoid(g) * u).astype(o_ref.dtype)

    @pl.when(i >= nused)
    def _():
        o_ref[...] = jnp.zeros_like(o_ref)


def _ffn_up(x, wg, wu, meta, *, name, tm=TM, tn=TN):
    m_rows, k = x.shape
    n = wg.shape[2]
    ni, nj = m_rows // tm, n // tn
    assert m_rows % tm == 0 and n % tn == 0 and meta.shape[0] == ni + 1
    wspec = pl.BlockSpec((None, k, tn), lambda j, i, m: (m[i], 0, j))
    return pl.pallas_call(
        functools.partial(_ffn_up_body, ni=ni),
        grid_spec=pltpu.PrefetchScalarGridSpec(
            num_scalar_prefetch=1, grid=(nj, ni),
            in_specs=[pl.BlockSpec((tm, k), lambda j, i, m: (jnp.minimum(i, m[ni] - 1), 0)),
                      wspec, wspec],
            out_specs=pl.BlockSpec((tm, tn), lambda j, i, m: (i, j)),
            scratch_shapes=[pltpu.VMEM((k, tn), BF16), pltpu.VMEM((k, tn), BF16)]),
        out_shape=jax.ShapeDtypeStruct((m_rows, n), BF16),
        compiler_params=_params(("arbitrary", "arbitrary")), name=name)(meta, x, wg, wu)


def _branch_body(meta_ref, xa_ref, xb_ref, xc_ref, ga_ref, gb_ref, gc_ref,
                 w0_ref, w1_ref, w2_ref, o_ref, s0_ref, s1_ref, s2_ref):
    i = pl.program_id(1)
    _cast_once(meta_ref, i, [(w0_ref, s0_ref), (w1_ref, s1_ref), (w2_ref, s2_ref)])
    acc = ga_ref[...] * jnp.dot(xa_ref[...], s0_ref[...], preferred_element_type=F32)
    acc = acc + gb_ref[...] * jnp.dot(xb_ref[...], s1_ref[...], preferred_element_type=F32)
    acc = acc + gc_ref[...] * jnp.dot(xc_ref[...], s2_ref[...], preferred_element_type=F32)
    o_ref[...] = acc.astype(o_ref.dtype)


def _branch_mm(oa, ob, oc, gates, wb, meta, *, name, tm=TM, tn=TN):
    m_rows, k = oa.shape
    n = wb.shape[2]
    ni, nj = m_rows // tm, n // tn
    xspec = pl.BlockSpec((tm, k), lambda j, i, m: (i, 0))

    def gspec(b):
        return pl.BlockSpec((tm, tn), lambda j, i, m: (i, j + b * nj))

    def wspec(b):
        return pl.BlockSpec((None, k, tn), lambda j, i, m: (m[i] * 3 + b, 0, j))

    return pl.pallas_call(
        _branch_body,
        grid_spec=pltpu.PrefetchScalarGridSpec(
            num_scalar_prefetch=1, grid=(nj, ni),
            in_specs=[xspec, xspec, xspec, gspec(0), gspec(1), gspec(2), wspec(0), wspec(1), wspec(2)],
            out_specs=pl.BlockSpec((tm, tn), lambda j, i, m: (i, j)),
            scratch_shapes=[pltpu.VMEM((k, tn), BF16)] * 3),
        out_shape=jax.ShapeDtypeStruct((m_rows, n), BF16),
        compiler_params=_params(("arbitrary", "arbitrary")), name=name)(
            meta, oa, ob, oc, gates, gates, gates, wb, wb, wb)


def _route(hh, rw_ref, rb_ref):
    lg = jnp.dot(hh, rw_ref[...], precision=HI, preferred_element_type=F32) + rb_ref[...]
    lane = lax.broadcasted_iota(I32, lg.shape, 1)
    m1 = jnp.max(lg, axis=1, keepdims=True)
    i1 = jnp.min(jnp.where(lg == m1, lane, LANE), axis=1, keepdims=True)
    lg2 = jnp.where(lane == i1, NEG_BIG * 2, lg)
    m2 = jnp.max(lg2, axis=1, keepdims=True)
    i2 = jnp.min(jnp.where(lg2 == m2, lane, LANE), axis=1, keepdims=True)
    e = jnp.exp(m2 - m1)
    p1 = 1.0 / (1.0 + e)
    p2 = e / (1.0 + e)
    return jnp.where(lane == 0, p1, jnp.where(lane == 1, p2, jnp.where(
        lane == 2, i1.astype(F32), jnp.where(lane == 3, i2.astype(F32), 0.0))))


def _post_body(*refs, npt, reps, res, nxt, router):
    it = iter(refs)
    x_ref = next(it)
    y_ref = next(it) if res else None
    gres_ref = next(it) if res else None
    gnx_ref = next(it) if nxt else None
    mpr_ref = next(it) if res else None
    mpn_ref = next(it) if nxt else None
    msg_ref = next(it) if res else None
    msc_ref = next(it) if nxt else None
    msh_ref = next(it) if nxt else None
    rw_ref = next(it) if router else None
    rb_ref = next(it) if router else None
    xo_ref = next(it) if res else None
    hh_ref = next(it) if nxt else None
    rt_ref = next(it) if router else None
    i = pl.program_id(0)

    def compute(gate, sc, sh):
        x = x_ref[...]
        if res:
            x = x + gate * _rms(y_ref[...].astype(F32), gres_ref[...])
            xo_ref[...] = x
        if nxt:
            hh = _rms(x, gnx_ref[...]) * (1.0 + sc) + sh
            hh_ref[...] = hh.astype(hh_ref.dtype)
            if router:
                rt_ref[...] = _route(hh, rw_ref, rb_ref)

    def rep(r):
        return jnp.concatenate([r[...]] * reps, axis=0) if reps > 1 else r[...]

    @pl.when(i < npt)
    def _():
        compute(mpr_ref[res[2]:res[2] + 1, :] if res else None,
                mpn_ref[nxt[2]:nxt[2] + 1, :] if nxt else None,
                mpn_ref[nxt[3]:nxt[3] + 1, :] if nxt else None)

    @pl.when(i >= npt)
    def _():
        compute(rep(msg_ref) if res else None, rep(msc_ref) if nxt else None,
                rep(msh_ref) if nxt else None)


def _post(x, y, cfg, *, res, nxt, router=None, hh_dtype=BF16, name):
    t_rows, d = x.shape
    te, bp, bs = TE, cfg["bp"], cfg["bs"]
    npt, nt, tpb = cfg["tp"] // te, t_rows // te, cfg["seq"] // te
    assert te % bs == 0 and cfg["seq"] % te == 0 and t_rows % te == 0
    row = pl.BlockSpec((te, d), lambda i: (i, 0))
    in_specs, args = [row], [x]
    if res:
        in_specs.append(row)
        args.append(y)

    def gspec(layer, r):
        return pl.BlockSpec((None, 1, d), lambda i: (layer * 4 + r, 0, 0))

    def mpspec(layer):
        return pl.BlockSpec((None, N_MOD, d), lambda i: (layer * bp + jnp.minimum(i // tpb, bp - 1), 0, 0))

    def msspec(layer, plane):
        return pl.BlockSpec((None, bs, d), lambda i: (layer * N_MOD + plane, 0, 0))

    if res:
        in_specs.append(gspec(res[0], res[1])); args.append(cfg["norm_g"])
    if nxt:
        in_specs.append(gspec(nxt[0], nxt[1])); args.append(cfg["norm_g"])
    if res:
        in_specs.append(mpspec(res[0])); args.append(cfg["mod_p"])
    if nxt:
        in_specs.append(mpspec(nxt[0])); args.append(cfg["mod_p"])
    if res:
        in_specs.append(msspec(res[0], res[2])); args.append(cfg["mod_s"])
    if nxt:
        in_specs.append(msspec(nxt[0], nxt[2])); args.append(cfg["mod_s"])
        in_specs.append(msspec(nxt[0], nxt[3])); args.append(cfg["mod_s"])
    if router:
        rw, rb, slab = router
        in_specs.append(pl.BlockSpec((None, d, LANE), lambda i: (slab, 0, 0))); args.append(rw)
        in_specs.append(pl.BlockSpec((None, 1, LANE), lambda i: (slab, 0, 0))); args.append(rb)
    out_specs, out_shape = [], []
    if res:
        out_specs.append(row); out_shape.append(jax.ShapeDtypeStruct((t_rows, d), F32))
    if nxt:
        out_specs.append(row); out_shape.append(jax.ShapeDtypeStruct((t_rows, d), hh_dtype))
    if router:
        out_specs.append(pl.BlockSpec((te, LANE), lambda i: (i, 0)))
        out_shape.append(jax.ShapeDtypeStruct((t_rows, LANE), F32))
    body = functools.partial(_post_body, npt=npt, reps=te // bs, res=res, nxt=nxt, router=bool(router))
    return pl.pallas_call(body, grid=(nt,), in_specs=in_specs, out_specs=out_specs, out_shape=out_shape,
                          compiler_params=_params(("arbitrary",)), name=name)(*args)


def _row_copy(src_hbm, row, dst_ref, r, sem):
    return pltpu.make_async_copy(src_hbm.at[pl.ds(row, 1), :], dst_ref.at[pl.ds(r, 1), :], sem)


def _gather_rows(idx_ref, base, src_hbm, dst_ref, sem, n):
    def start(r, c):
        _row_copy(src_hbm, idx_ref[base + r], dst_ref, r, sem).start()
        return c

    def wait(r, c):
        _row_copy(src_hbm, 0, dst_ref, r, sem).wait()
        return c

    lax.fori_loop(0, n, start, 0, unroll=DMA_ISSUE_UNROLL)
    lax.fori_loop(0, n, wait, 0, unroll=DMA_ISSUE_UNROLL)


def _dispatch_body(src_ref, nused_ref, x_hbm, o_ref, buf_ref, sem, *, tm):
    i = pl.program_id(0)

    @pl.when(i < nused_ref[0])
    def _():
        _gather_rows(src_ref, i * tm, x_hbm, buf_ref, sem, tm)
        o_ref[...] = buf_ref[...].astype(o_ref.dtype)

    @pl.when(i >= nused_ref[0])
    def _():
        o_ref[...] = jnp.zeros_like(o_ref)


def _dispatch(hh, src, nused, *, tm, name):
    cap = src.shape[0]
    d = hh.shape[1]
    return pl.pallas_call(
        functools.partial(_dispatch_body, tm=tm),
        grid_spec=pltpu.PrefetchScalarGridSpec(
            num_scalar_prefetch=2, grid=(cap // tm,),
            in_specs=[pl.BlockSpec(memory_space=pl.ANY)],
            out_specs=pl.BlockSpec((tm, d), lambda i, s, n: (i, 0)),
            scratch_shapes=[pltpu.VMEM((tm, d), hh.dtype), pltpu.SemaphoreType.DMA(())]),
        out_shape=jax.ShapeDtypeStruct((cap, d), BF16),
        compiler_params=_params(("arbitrary",)), name=name)(src, nused, hh)


def _combine_body(d0_ref, d1_ref, y_hbm, p_ref, o_ref, a_ref, b_ref, sem_a, sem_b, *, te):
    i = pl.program_id(0)

    def start(r, c):
        _row_copy(y_hbm, d0_ref[i * te + r], a_ref, r, sem_a).start()
        _row_copy(y_hbm, d1_ref[i * te + r], b_ref, r, sem_b).start()
        return c

    def wait(r, c):
        _row_copy(y_hbm, 0, a_ref, r, sem_a).wait()
        _row_copy(y_hbm, 0, b_ref, r, sem_b).wait()
        return c

    lax.fori_loop(0, te, start, 0, unroll=DMA_ISSUE_UNROLL)
    lax.fori_loop(0, te, wait, 0, unroll=DMA_ISSUE_UNROLL)
    p = p_ref[...]
    o_ref[...] = (p[:, 0:1] * a_ref[...] + p[:, 1:2] * b_ref[...]).astype(o_ref.dtype)


def _combine(ys, dest0, dest1, route, *, name):
    t_rows = route.shape[0]
    d = ys.shape[1]
    te = TE
    return pl.pallas_call(
        functools.partial(_combine_body, te=te),
        grid_spec=pltpu.PrefetchScalarGridSpec(
            num_scalar_prefetch=2, grid=(t_rows // te,),
            in_specs=[pl.BlockSpec(memory_space=pl.ANY),
                      pl.BlockSpec((te, LANE), lambda i, a, b: (i, 0))],
            out_specs=pl.BlockSpec((te, d), lambda i, a, b: (i, 0)),
            scratch_shapes=[pltpu.VMEM((te, d), F32), pltpu.VMEM((te, d), F32),
                            pltpu.SemaphoreType.DMA(()), pltpu.SemaphoreType.DMA(())]),
        out_shape=jax.ShapeDtypeStruct((t_rows, d), BF16),
        compiler_params=_params(("arbitrary",)), name=name)(dest0, dest1, ys, route)


def _dn_post(y, j):
    y = y * jax.nn.sigmoid(y)
    outs = []
    for h in range(y.shape[1] // DN_HEAD_DIM):
        yh = y[:, h * DN_HEAD_DIM:(h + 1) * DN_HEAD_DIM]
        n = lax.rsqrt(jnp.sum(yh * yh, axis=-1, keepdims=True) + L2_EPS)
        s = jnp.where(j == 0, n * (DN_HEAD_DIM ** -0.5), jnp.where(j == 1, n, 1.0))
        outs.append(yh * s)
    return jnp.concatenate(outs, axis=1)


def _conv_rows(x_ref, prev_ref, buf_ref, w, first):
    te = x_ref.shape[0]
    x = x_ref[...]
    buf_ref[0:SUBLANE, :] = jnp.where(first, 0.0, prev_ref[...])
    buf_ref[SUBLANE:, :] = x
    y = w[CONV_W - 1:CONV_W, :] * x
    for dlt in range(1, CONV_W):
        y = y + w[CONV_W - 1 - dlt:CONV_W - dlt, :] * buf_ref[SUBLANE - dlt:SUBLANE - dlt + te, :]
    return y


def _conv_slabs(x_ref, hist_ref, w, bs):
    xp = jnp.concatenate([hist_ref[...], x_ref[...]], axis=0)
    steps = x_ref.shape[0] // bs
    ys = []
    for t in range(steps):
        y = w[0:1, :] * xp[t * bs:(t + 1) * bs]
        for j in range(1, CONV_W):
            y = y + w[j:j + 1, :] * xp[(t + j) * bs:(t + j + 1) * bs]
        ys.append(y)
    return ys


def _dn_prep_s_body(x_ref, hist_ref, w_ref, o_ref, *, bs):
    ys = _conv_slabs(x_ref, hist_ref, w_ref[...], bs)
    o_ref[...] = _dn_post(jnp.concatenate(ys, axis=0), pl.program_id(0))


def _dn_prep_sample(qkv, hist_s, conv_w, layer, cfg):
    tp, ts, bs = cfg["tp"], cfg["ts"], cfg["bs"]
    w3 = qkv.shape[1]
    cb = w3 // 3
    return pl.pallas_call(
        functools.partial(_dn_prep_s_body, bs=bs), grid=(3,),
        in_specs=[pl.BlockSpec((ts, cb), lambda j: (tp // ts, j)),
                  pl.BlockSpec(((CONV_W - 1) * bs, cb), lambda j: (0, j)),
                  pl.BlockSpec((None, CONV_W, cb), lambda j: (layer, 0, j))],
        out_specs=pl.BlockSpec((ts, cb), lambda j: (0, j)),
        out_shape=jax.ShapeDtypeStruct((ts, w3), F32),
        compiler_params=_params(("arbitrary",)), name="dn_prep_s")(qkv, hist_s, conv_w)


def _dotf(a, b):
    return jnp.dot(a, b, precision=HI, preferred_element_type=F32)


def _dotb(a, b):
    return jnp.dot(a.astype(BF16), b.astype(BF16), preferred_element_type=F32)


def _dot_nt(a, b):
    return lax.dot_general(a, b, (((1,), (1,)), ((), ())), preferred_element_type=F32)


def _dot_tn(a, b):
    return lax.dot_general(a, b, (((0,), (0,)), ((), ())), preferred_element_type=F32)


def _dn_gates(ba, al, dt, heads, valid):
    lanel = lax.broadcasted_iota(I32, ba.shape, 1)
    beta = jnp.where(valid & (lanel < heads), jax.nn.sigmoid(ba), 0.0)
    g = jnp.where(valid, -jnp.exp(al) * jax.nn.softplus(ba + dt), 0.0)
    return beta, g


def _dn_wy_body(x_ref, prev_ref, cw_ref, ba_ref, al_ref, dt_ref,
                w_ref, u_ref, qg_ref, kd_ref, sc_ref, eg_ref, m_ref, r_ref, p_ref, buf_ref, *, steps_per_seq):
    c = DN_CHUNK
    rows_n = x_ref.shape[0]
    wq = x_ref.shape[1] // 3
    heads = wq // DN_HEAD_DIM
    p2 = 2 * c
    y = _conv_rows(x_ref, prev_ref, buf_ref, cw_ref[...], pl.program_id(0) % steps_per_seq == 0)
    for j in range(3):
        p_ref[j] = _dn_post(y[:, j * wq:(j + 1) * wq], j)
    q_ref, k_ref, v_ref = p_ref.at[0], p_ref.at[1], p_ref.at[2]
    beta_all, g_all = _dn_gates(ba_ref[...], al_ref[...], dt_ref[...], heads, True)
    rr = lax.broadcasted_iota(I32, (rows_n, rows_n), 0)
    cc = lax.broadcasted_iota(I32, (rows_n, rows_n), 1)
    gcum_all = _dotf(((rr >= cc) & (rr // c == cc // c)).astype(F32), g_all)
    row = lax.broadcasted_iota(I32, (p2, p2), 0)
    col = lax.broadcasted_iota(I32, (p2, p2), 1)
    same = (row // c) == (col // c)
    tri = (row >= col) & same
    strict = (row > col) & same

    eye = (row == col).astype(F32)
    blk = (row // SUBLANE) == (col // SUBLANE)

    def stack(ref, rs, h):
        return jnp.concatenate([ref[rs, (h + a) * DN_HEAD_DIM:(h + a + 1) * DN_HEAD_DIM] for a in (0, 1)], axis=0)

    def cols(arr, rs, lane0):
        return jnp.concatenate([arr[rs, lane0 + a:lane0 + a + 1] for a in (0, 1)], axis=0)

    probs = [(ci, h) for ci in range(rows_n // c) for h in range(0, heads, 2)]
    n_p = len(probs)

    for p, (ci, h) in enumerate(probs):
        rs = slice(ci * c, (ci + 1) * c)
        q, k = stack(q_ref, rs, h), stack(k_ref, rs, h)
        beta = cols(beta_all, rs, h)
        gcol = cols(gcum_all, rs, heads + h)
        glast = jnp.concatenate(
            [jnp.broadcast_to(gcum_all[(ci + 1) * c - 1:(ci + 1) * c, heads + h + a:heads + h + a + 1], (c, 1))
             for a in (0, 1)], axis=0)
        gmat = jnp.broadcast_to(gcol, (p2, p2))
        decay = jnp.exp(jnp.where(tri, gmat - gmat.T, NEG_BIG))
        kb = k * beta
        k16 = k.astype(BF16)
        lm = jnp.where(strict, _dot_nt(kb.astype(BF16), k16) * decay, 0.0)
        m_ref[0, p] = lm
        m_ref[1, p] = jnp.where(blk, lm, 0.0)
        sc = _dot_nt(q.astype(BF16), k16) * decay
        egc = jnp.exp(gcol)
        r_ref[0, p] = (kb * egc).astype(BF16)
        r_ref[1, p] = (stack(v_ref, rs, h) * beta).astype(BF16)
        qg = q * egc
        kd = k * jnp.exp(glast - gcol)
        egl = jnp.exp(glast)
        for a in (0, 1):
            hs = slice(a * c, (a + 1) * c)
            qg_ref[ci, h + a] = qg[hs].astype(qg_ref.dtype)
            kd_ref[ci, h + a] = kd[hs].astype(kd_ref.dtype)
            sc_ref[ci, h + a] = sc[hs, hs].astype(sc_ref.dtype)
            eg_ref[ci, h + a] = jnp.broadcast_to(egl[a * c:a * c + 1], (1, LANE))

    def stage(dst, fa, fb):
        for p in range(n_p):
            m_ref[dst, p] = _dotb(fa(p), fb(p))

    ld = lambda p: m_ref[1, p]
    stage(2, ld, ld)
    stage(3, lambda p: m_ref[2, p], lambda p: m_ref[2, p])
    stage(4, lambda p: eye - ld(p), lambda p: eye + m_ref[2, p])
    stage(5, lambda p: m_ref[4, p], lambda p: eye + m_ref[3, p])
    stage(2, lambda p: m_ref[5, p], lambda p: m_ref[0, p] - ld(p))
    stage(3, lambda p: m_ref[2, p], lambda p: m_ref[2, p])
    stage(4, lambda p: m_ref[3, p], lambda p: m_ref[3, p])
    stage(6, lambda p: eye - m_ref[2, p], lambda p: eye + m_ref[3, p])
    stage(7, lambda p: m_ref[6, p], lambda p: eye + m_ref[4, p])
    stage(2, lambda p: m_ref[7, p], lambda p: m_ref[5, p])
    for p, (ci, h) in enumerate(probs):
        t16 = m_ref[2, p].astype(BF16)
        wm = jnp.dot(t16, r_ref[0, p], preferred_element_type=F32)
        um = jnp.dot(t16, r_ref[1, p], preferred_element_type=F32)
        for a in (0, 1):
            hs = slice(a * c, (a + 1) * c)
            w_ref[ci, h + a] = wm[hs].astype(w_ref.dtype)
            u_ref[ci, h + a] = um[hs]


def _dn_seq_body(w_ref, u_ref, qg_ref, kd_ref, sc_ref, eg_ref, z_ref, ng_ref, o_ref, so_ref, s_ref, r_ref):
    nck, heads, c = w_ref.shape[0], w_ref.shape[1], w_ref.shape[2]
    ci = pl.program_id(1)

    @pl.when(ci == 0)
    def _():
        s_ref[...] = jnp.zeros_like(s_ref)

    def chunk(n, carry):
        rs = pl.ds(pl.multiple_of(n * c, c), c)
        for h in range(heads):
            r_ref[h] = jnp.dot(jnp.concatenate([w_ref[n, h], qg_ref[n, h]], axis=0),
                               s_ref[h].astype(BF16), preferred_element_type=F32)
        for h in range(heads):
            hs = slice(h * DN_HEAD_DIM, (h + 1) * DN_HEAD_DIM)
            vn16 = (u_ref[n, h] - r_ref[h, 0:c]).astype(BF16)
            o = r_ref[h, c:2 * c] + jnp.dot(sc_ref[n, h], vn16, preferred_element_type=F32)
            s_ref[h] = s_ref[h] * eg_ref[n, h] + _dot_tn(kd_ref[n, h], vn16)
            o_ref[rs, hs] = (_rms(o, ng_ref[...]) * z_ref[rs, hs]).astype(o_ref.dtype)
        return carry

    lax.fori_loop(0, nck, chunk, 0)

    @pl.when(ci == pl.num_programs(1) - 1)
    def _():
        so_ref[...] = s_ref[...]


DN_WY_CHUNKS = 4
DN_SEQ_CHUNKS = 8


def _dn_prompt(qkv, conv_w, z, ba, al, dt, ng, layer, cfg):
    tp, seq, bp = cfg["tp"], cfg["seq"], cfg["bp"]
    w3 = qkv.shape[1]
    w = w3 // 3
    heads = w // DN_HEAD_DIM
    c, d = DN_CHUNK, DN_HEAD_DIM
    nc = tp // c
    kw, ks = DN_WY_CHUNKS, DN_SEQ_CHUNKS
    assert nc % kw == 0 and (seq // c) % ks == 0 and heads % 2 == 0
    vec = pl.BlockSpec((None, 1, LANE), lambda *_: (layer, 0, 0))
    hm = lambda n, last: pl.BlockSpec((n, heads, c, last), lambda i: (i, 0, 0, 0))
    eg_spec = lambda n: pl.BlockSpec((n, heads, 1, LANE), lambda i: (i, 0, 0, 0))
    sds = jax.ShapeDtypeStruct
    rpb = kw * c // SUBLANE
    wm, um, qg, kd, sc, eg = pl.pallas_call(
        functools.partial(_dn_wy_body, steps_per_seq=seq // (kw * c)), grid=(nc // kw,),
        in_specs=[pl.BlockSpec((kw * c, w3), lambda i: (i, 0)),
                  pl.BlockSpec((SUBLANE, w3), lambda i: (jnp.maximum(i * rpb - 1, 0), 0)),
                  pl.BlockSpec((None, CONV_W, w3), lambda i: (layer, 0, 0)),
                  pl.BlockSpec((kw * c, LANE), lambda i: (i, 0)), vec, vec],
        out_specs=[hm(kw, d), hm(kw, d), hm(kw, d), hm(kw, d), hm(kw, c), eg_spec(kw)],
        out_shape=[sds((nc, heads, c, d), BF16), sds((nc, heads, c, d), F32), sds((nc, heads, c, d), BF16),
                   sds((nc, heads, c, d), BF16), sds((nc, heads, c, c), BF16), sds((nc, heads, 1, LANE), F32)],
        scratch_shapes=[pltpu.VMEM((8, kw * heads // 2, 2 * c, 2 * c), F32),
                        pltpu.VMEM((2, kw * heads // 2, 2 * c, d), BF16),
                        pltpu.VMEM((3, kw * c, w), F32), pltpu.VMEM((kw * c + SUBLANE, w3), F32)],
        compiler_params=_params(("arbitrary",)), name="dn_wy")(qkv, qkv, conv_w, ba, al, dt)
    nps = seq // c // ks
    hm2 = lambda last: pl.BlockSpec((ks, heads, c, last), lambda b, n: (b * nps + n, 0, 0, 0))
    return pl.pallas_call(
        _dn_seq_body, grid=(bp, nps),
        in_specs=[hm2(d), hm2(d), hm2(d), hm2(d), hm2(c),
                  pl.BlockSpec((ks, heads, 1, LANE), lambda b, n: (b * nps + n, 0, 0, 0)),
                  pl.BlockSpec((ks * c, w), lambda b, n: (b * nps + n, 0)),
                  pl.BlockSpec((None, 1, LANE), lambda b, n: (layer, 0, 0))],
        out_specs=[pl.BlockSpec((ks * c, w), lambda b, n: (b * nps + n, 0)),
                   pl.BlockSpec((None, heads, d, d), lambda b, n: (b, 0, 0, 0))],
        out_shape=[sds((tp, w), BF16), sds((bp, heads, d, d), F32)],
        scratch_shapes=[pltpu.VMEM((heads, d, d), F32), pltpu.VMEM((heads, 2 * c, d), F32)],
        compiler_params=_params(("arbitrary", "arbitrary")), name="dn_seq")(wm, um, qg, kd, sc, eg, z, ng)


def _dn_chunk_body(q_ref, k_ref, v_ref, z_ref, ba_ref, s0_ref, al_ref, dt_ref, ng_ref, chain_ref,
                   o_ref, so_ref, m_ref, sc_ref, ws_ref, kd_ref, vn_ref, *, groups, n_valid):
    c = q_ref.shape[0]
    heads = q_ref.shape[1] // DN_HEAD_DIM
    rpg = c // groups
    assert rpg == SUBLANE

    row = lax.broadcasted_iota(I32, (c, c), 0)
    col = lax.broadcasted_iota(I32, (c, c), 1)
    same = (row // rpg) == (col // rpg)
    tri = (row >= col) & same
    strict = (row > col) & same
    rowl = lax.broadcasted_iota(I32, (c, LANE), 0)
    beta_all, g_all = _dn_gates(ba_ref[...], al_ref[...], dt_ref[...], heads, (rowl % rpg) < n_valid)
    gcum_all = _dotf(tri.astype(F32), g_all)
    rg = lax.broadcasted_iota(I32, (c, 1), 0) // rpg
    rg2 = jnp.concatenate([rg, rg], axis=0)

    eye = (row == col).astype(F32)
    glast = lambda h, gi: gcum_all[(gi + 1) * rpg - 1:(gi + 1) * rpg, heads + h:heads + h + 1]

    for h in range(heads):
        hs = slice(h * DN_HEAD_DIM, (h + 1) * DN_HEAD_DIM)
        q, k = q_ref[:, hs], k_ref[:, hs]
        beta = beta_all[:, h:h + 1]
        gcol = gcum_all[:, heads + h:heads + h + 1]
        gmat = jnp.broadcast_to(
            jnp.concatenate([gcol, jnp.zeros((LANE - c, 1), F32)], axis=0), (LANE, LANE))
        decay = jnp.exp(jnp.where(tri, gmat[0:c, 0:c] - gmat.T[0:c, 0:c], NEG_BIG))
        kb = k * beta
        k16 = k.astype(BF16)
        m_ref[0, h] = jnp.where(strict, _dot_nt(kb.astype(BF16), k16) * decay, 0.0)
        sc_ref[h] = (_dot_nt(q.astype(BF16), k16) * decay).astype(BF16)
        egc = jnp.exp(gcol)
        lhs = jnp.concatenate([kb * egc, q * egc], axis=0).astype(BF16)
        ws = jnp.zeros((2 * c, DN_HEAD_DIM), F32)
        glast_rows = jnp.zeros((c, 1), F32)
        for gi in range(groups):
            r = jnp.dot(lhs, s0_ref[gi, h].astype(BF16), preferred_element_type=F32)
            ws = jnp.where(rg2 == gi, r, ws)
            glast_rows = jnp.where(rg == gi, glast(h, gi), glast_rows)
        ws_ref[h] = ws
        kd_ref[h] = (k * jnp.exp(glast_rows - gcol)).astype(BF16)

    def stage(dst, fa, fb):
        for h in range(heads):
            m_ref[dst, h] = _dotb(fa(h), fb(h))

    stage(1, lambda h: m_ref[0, h], lambda h: m_ref[0, h])
    stage(2, lambda h: m_ref[1, h], lambda h: m_ref[1, h])
    stage(3, lambda h: eye - m_ref[0, h], lambda h: eye + m_ref[1, h])
    stage(1, lambda h: m_ref[3, h], lambda h: eye + m_ref[2, h])
    for h in range(heads):
        hs = slice(h * DN_HEAD_DIM, (h + 1) * DN_HEAD_DIM)
        vn_ref[h] = _dotb(m_ref[1, h], v_ref[:, hs] * beta_all[:, h:h + 1] - ws_ref[h, 0:c]).astype(BF16)
    for h in range(heads):
        hs = slice(h * DN_HEAD_DIM, (h + 1) * DN_HEAD_DIM)
        vn16 = vn_ref[h]
        o = ws_ref[h, c:2 * c] + jnp.dot(sc_ref[h], vn16, preferred_element_type=F32)
        kdec = kd_ref[h]
        for gi in range(groups):
            upd = _dot_tn(jnp.where(rg == gi, kdec, jnp.zeros_like(kdec)), vn16)
            so_ref[gi, h] = s0_ref[gi, h] * jnp.exp(glast(h, gi)) + upd
        o_ref[:, hs] = (_rms(o, ng_ref[...]) * z_ref[:, hs]).astype(o_ref.dtype)


def _dn_sample(qkv, z, ba, state, chain, al, dt, ng, layer, *, n_valid):
    rows, w3 = qkv.shape
    w = w3 // 3
    heads = w // DN_HEAD_DIM
    c = DN_CHUNK
    groups = c // SUBLANE
    rspec = lambda col: pl.BlockSpec((c, w), lambda b: (b, col))
    vec = pl.BlockSpec((None, 1, LANE), lambda b: (layer, 0, 0))
    sspec = pl.BlockSpec((None, groups, heads, DN_HEAD_DIM, DN_HEAD_DIM), lambda b: (layer, b, 0, 0, 0))
    return pl.pallas_call(
        functools.partial(_dn_chunk_body, groups=groups, n_valid=n_valid),
        grid=(rows // c,),
        in_specs=[rspec(0), rspec(1), rspec(2), rspec(0), pl.BlockSpec((c, LANE), lambda b: (b, 0)),
                  sspec, vec, vec, vec, pl.BlockSpec(memory_space=pl.ANY)],
        out_specs=[rspec(0), sspec],
        out_shape=[jax.ShapeDtypeStruct((rows, w), BF16), jax.ShapeDtypeStruct(chain.shape, F32)],
        scratch_shapes=[pltpu.VMEM((4, heads, c, c), F32), pltpu.VMEM((heads, c, c), BF16),
                        pltpu.VMEM((heads, 2 * c, DN_HEAD_DIM), F32), pltpu.VMEM((heads, c, DN_HEAD_DIM), BF16),
                        pltpu.VMEM((heads, c, DN_HEAD_DIM), BF16)],
        input_output_aliases={9: 1},
        compiler_params=_params(("arbitrary",)), name="dn_sample")(
            qkv, qkv, qkv, z, ba, state, al, dt, ng, chain)


def _layer_norm(v, g, b):
    mu = jnp.mean(v, axis=-1, keepdims=True)
    vc = v - mu
    return vc * lax.rsqrt(jnp.mean(vc * vc, axis=-1, keepdims=True) + LN_EPS) * g + b


def _gmlp_body(uv_ref, g_ref, b_ref, ws_ref, bst_ref, cw_ref, cb_ref, o_ref, vn_ref, *, npt, bs):
    w = uv_ref.shape[1] // 2
    i = pl.program_id(0)
    vn = _layer_norm(uv_ref[:, w:].astype(F32), g_ref[...], b_ref[...])

    @pl.when(i < npt)
    def _():
        groups = ws_ref.shape[0]
        gw = w // groups
        v16 = vn.astype(BF16)
        row = lax.broadcasted_iota(I32, (GM_CHUNK, GM_CHUNK), 0)
        col = lax.broadcasted_iota(I32, (GM_CHUNK, GM_CHUNK), 1)
        wcs = [jnp.where(row >= col, ws_ref[g], 0.0).astype(BF16) for g in range(groups)]
        bst = bst_ref[...]
        for c in range(uv_ref.shape[0] // GM_CHUNK):
            rs = slice(c * GM_CHUNK, (c + 1) * GM_CHUNK)
            for g in range(groups):
                cs = slice(g * gw, (g + 1) * gw)
                s = jnp.dot(wcs[g], v16[rs, cs], preferred_element_type=F32) + bst[:, g:g + 1]
                o_ref[rs, cs] = (uv_ref[rs, cs] * s).astype(o_ref.dtype)

    @pl.when(i >= npt)
    def _():
        steps = uv_ref.shape[0] // bs
        vn_ref[...] = vn
        for t in range(steps):
            s = cb_ref[t:t + 1, :]
            for u in range(t + 1):
                s = s + cw_ref[t * steps + u:t * steps + u + 1, :] * vn[u * bs:(u + 1) * bs]
            o_ref[t * bs:(t + 1) * bs, :] = (uv_ref[t * bs:(t + 1) * bs, 0:w] * s).astype(o_ref.dtype)


def _gmlp(uv, ln_g, ln_b, ws, bst, cw, cb, layer, cfg):
    tl, bs = cfg["ts"], cfg["bs"]
    t_rows, w2 = uv.shape
    w = w2 // 2
    groups = ws.shape[1]
    steps = tl // bs
    vec = pl.BlockSpec((None, 1, w), lambda i: (layer, 0, 0))
    return pl.pallas_call(
        functools.partial(_gmlp_body, npt=cfg["tp"] // tl, bs=bs), grid=(t_rows // tl,),
        in_specs=[pl.BlockSpec((tl, w2), lambda i: (i, 0)), vec, vec,
                  pl.BlockSpec((None, groups, GM_CHUNK, GM_CHUNK), lambda i: (layer, 0, 0, 0)),
                  pl.BlockSpec((None, GM_CHUNK, LANE), lambda i: (layer, 0, 0)),
                  pl.BlockSpec((None, steps * steps, w), lambda i: (layer, 0, 0)),
                  pl.BlockSpec((None, SUBLANE, w), lambda i: (layer, 0, 0))],
        out_specs=[pl.BlockSpec((tl, w), lambda i: (i, 0)), pl.BlockSpec((tl, w), lambda i: (0, 0))],
        out_shape=[jax.ShapeDtypeStruct((t_rows, w), BF16), jax.ShapeDtypeStruct((tl, w), F32)],
        compiler_params=_params(("arbitrary",)), name="gmlp")(uv, ln_g, ln_b, ws, bst, cw, cb)


def _lru_gates(xc, wa_ref, ba, wx_ref, bx, lam):
    nb = wa_ref.shape[0]
    bw = xc.shape[1] // nb
    x16 = xc.astype(BF16)
    rs, gs = [], []
    for h in range(nb):
        xs = x16[:, h * bw:(h + 1) * bw]
        rs.append(jnp.dot(xs, wa_ref[h].astype(BF16), preferred_element_type=F32))
        gs.append(jnp.dot(xs, wx_ref[h].astype(BF16), preferred_element_type=F32))
    r = jax.nn.sigmoid(jnp.concatenate(rs, axis=1) + ba)
    ig = jax.nn.sigmoid(jnp.concatenate(gs, axis=1) + bx)
    log_a = -LRU_C * r * jax.nn.softplus(-lam)
    th = jnp.tanh(log_a)
    return jnp.exp(log_a), jnp.sqrt(-2.0 * th / (1.0 - th)), ig


def _lru_body(x_ref, prev_ref, y_ref, hist_ref, h0_ref, cw_ref, cb_ref, wa_ref, ba_ref, wx_ref, bx_ref,
              lam_ref, o_ref, hlp_ref, hls_ref, buf_ref, a_ref, b_ref, h_ref, *, npt, tps, bs, pos0):
    te, w = x_ref.shape
    i = pl.program_id(0)

    def gates(xc):
        return _lru_gates(xc, wa_ref, ba_ref[...], wx_ref, bx_ref[...], lam_ref[...])

    @pl.when(i < npt)
    def _():
        first = i % tps == 0
        xc = _conv_rows(x_ref, prev_ref, buf_ref, cw_ref[...], first) + cb_ref[...]
        a, mult, ig = gates(xc)
        reset = first & (lax.broadcasted_iota(I32, (te, w), 0) == 0)
        a_ref[...] = jnp.where(reset, 0.0, a)
        b_ref[...] = jnp.where(reset, 1.0, mult) * ig * xc

        @pl.when(first)
        def _():
            h_ref[...] = jnp.zeros_like(h_ref)

        sub = lax.broadcasted_iota(I32, (SUBLANE, w), 0)

        def slab(s, h):
            rs = pl.ds(pl.multiple_of(s * SUBLANE, SUBLANE), SUBLANE)
            av, bv = a_ref[rs, :], b_ref[rs, :]
            for dlt in (1, 2, 4):
                ash, bsh = pltpu.roll(av, dlt, 0), pltpu.roll(bv, dlt, 0)
                m = sub >= dlt
                bv = jnp.where(m, av * bsh + bv, bv)
                av = jnp.where(m, av * ash, av)
            hs = av * h + bv
            b_ref[rs, :] = hs
            return hs[SUBLANE - 1:SUBLANE, :]

        h_last = lax.fori_loop(0, te // SUBLANE, slab, h_ref[...])
        h_ref[...] = h_last
        hlp_ref[...] = h_last
        o_ref[...] = (y_ref[...] * b_ref[...]).astype(o_ref.dtype)

    @pl.when(i >= npt)
    def _():
        ys = _conv_slabs(x_ref, hist_ref, cw_ref[...], bs)
        h = h0_ref[...]
        for t, y in enumerate(ys):
            xc = y + cb_ref[...]
            a, mult, ig = gates(xc)
            if pos0 + t == 0:
                a, mult = jnp.zeros_like(a), jnp.ones_like(mult)
            h = a * h + mult * ig * xc
            o_ref[t * bs:(t + 1) * bs, :] = (y_ref[t * bs:(t + 1) * bs, :] * h).astype(o_ref.dtype)
        hls_ref[...] = h


def _lru(lx, ly, hist_s, h0_s, p, layer, cfg):
    tl, bs, bp = cfg["ts"], cfg["bs"], cfg["bp"]
    t_rows, w = lx.shape
    nb = p["lru_wa"].shape[1]
    bw = w // nb
    npt, tps = cfg["tp"] // tl, cfg["seq"] // tl
    rpb = tl // SUBLANE
    vec = pl.BlockSpec((None, 1, w), lambda i: (layer, 0, 0))
    blk = pl.BlockSpec((None, nb, bw, bw), lambda i: (layer, 0, 0, 0))
    row = pl.BlockSpec((tl, w), lambda i: (i, 0))
    oc, hl_p, hl_s = pl.pallas_call(
        functools.partial(_lru_body, npt=npt, tps=tps, bs=bs, pos0=cfg["pos0_s"]), grid=(t_rows // tl,),
        in_specs=[row, pl.BlockSpec((SUBLANE, w), lambda i: (jnp.maximum(i * rpb - 1, 0), 0)), row,
                  pl.BlockSpec(((CONV_W - 1) * bs, w), lambda i: (0, 0)),
                  pl.BlockSpec((None, bs, w), lambda i: (layer, 0, 0)),
                  pl.BlockSpec((None, CONV_W, w), lambda i: (layer, 0, 0)), vec, blk, vec, blk, vec, vec],
        out_specs=[row, pl.BlockSpec((None, 1, w), lambda i: (jnp.minimum(i // tps, bp - 1), 0, 0)),
                   pl.BlockSpec((bs, w), lambda i: (0, 0))],
        out_shape=[jax.ShapeDtypeStruct((t_rows, w), BF16), jax.ShapeDtypeStruct((bp, 1, w), F32),
                   jax.ShapeDtypeStruct((bs, w), F32)],
        scratch_shapes=[pltpu.VMEM((tl + SUBLANE, w), F32), pltpu.VMEM((tl, w), F32),
                        pltpu.VMEM((tl, w), F32), pltpu.VMEM((1, w), F32)],
        compiler_params=_params(("arbitrary",)), name="lru")(
            lx, lx, ly, hist_s, h0_s, p["lru_conv_w"], p["lru_conv_b"], p["lru_wa"], p["lru_ba"],
            p["lru_wx"], p["lru_bx"], p["lru_lambda"])
    return oc, hl_p.reshape(bp, w), hl_s


def _time_major(a):
    return jnp.swapaxes(a, 0, 1).reshape(a.shape[0] * a.shape[1], a.shape[2])


def _batch_major(a, bs):
    return jnp.swapaxes(a.reshape(a.shape[0] // bs, bs, a.shape[1]), 0, 1)


def _moe_plan(route, n_exp, tm):
    t_rows = route.shape[0]
    flat_e = route[:, 2:2 + TOP_K].astype(I32).reshape(-1)
    oh = (flat_e[:, None] == jnp.arange(n_exp, dtype=I32)[None, :]).astype(I32)
    csum = jnp.cumsum(oh, axis=0)
    rank = jnp.take_along_axis(csum, flat_e[:, None], axis=1)[:, 0] - 1
    counts = csum[-1]
    tiles_e = (counts + tm - 1) // tm
    tile_end = jnp.cumsum(tiles_e)
    dest = ((tile_end - tiles_e) * tm)[flat_e] + rank
    n_tiles = pl.cdiv(TOP_K * t_rows, tm) + n_exp
    src = jnp.zeros((n_tiles * tm,), I32).at[dest].set(jnp.arange(TOP_K * t_rows, dtype=I32) // TOP_K)
    tile_e = jnp.sum((jnp.arange(n_tiles, dtype=I32)[:, None] >= tile_end[None, :]).astype(I32), axis=1)
    tile_e = jnp.minimum(tile_e, n_exp - 1)
    dest = dest.reshape(t_rows, TOP_K)
    return dest[:, 0], dest[:, 1], src, tile_e, tile_end[-1].astype(I32)


def kernel(x_prompt, x_sample, state_dn_S, state_dn_conv, state_lru_h, state_lru_conv, c_prompt, c_sample,
           w_mod, b_mod, norm_g, w_in, dn_conv_w, dn_a_log, dn_dt_bias, dn_norm_g, gm_ln_g, gm_ln_b, gm_ws,
           gm_bs, lru_conv_w, lru_conv_b, lru_wa, lru_ba, lru_wx, lru_bx, lru_lambda, w_branch, w_out,
           ffn_w_gate, ffn_w_up, ffn_w_down, router_w, router_b, moe_w_gate, moe_w_up, moe_w_down):
    bp, seq, d = x_prompt.shape
    bs, dec = x_sample.shape[0], x_sample.shape[1]
    depth = w_mod.shape[0]
    w = d // 2
    heads = w // DN_HEAD_DIM
    n_exp = moe_w_gate.shape[1]
    tp, ts = bp * seq, bs * dec
    t_rows = tp + ts
    past_len = 16384
    assert dec <= SUBLANE and dec >= CONV_W - 1 and ts % GM_CHUNK == 0 and seq % ts == 0
    assert tp % ts == 0 and ts % TE == 0 and bs % SUBLANE == 0 and t_rows % (16 * 2 * DENSE_ROW_TILES) == 0
    cfg = dict(bp=bp, bs=bs, seq=seq, tp=tp, ts=ts, pos0_s=past_len)
    tm_d = t_rows // DENSE_ROW_TILES

    def dense_meta(slab, ni=DENSE_ROW_TILES):
        return jnp.concatenate([jnp.full((ni,), slab, I32), jnp.array([ni], I32)])

    n_c = bp + bs
    n_c_pad = -(-n_c // SUBLANE) * SUBLANE
    c_all = jnp.concatenate([c_prompt, c_sample, jnp.zeros((n_c_pad - n_c, d), F32)], axis=0)
    b_mod3 = b_mod.reshape(depth, 1, N_MOD * d)
    mods = []
    for l in range(depth):
        meta = jnp.array([l, 1], I32)
        mods.append(_mm(c_all, w_mod, meta, name="mod", tm=n_c_pad, pre_act="silu", bias=b_mod3))
    mod = jnp.stack(mods).reshape(depth, n_c_pad, N_MOD, d)
    cfg["mod_p"] = mod[:, :bp].reshape(depth * bp, N_MOD, d)
    cfg["mod_s"] = jnp.swapaxes(mod[:, bp:n_c], 1, 2).reshape(depth * N_MOD, bs, d)
    cfg["norm_g"] = norm_g.reshape(depth * 4, 1, d)

    n_qkv, n_z = 3 * w, w
    c_ba = n_qkv + n_z
    c_rest = c_ba + 2 * heads
    assert c_rest % SUBLANE == 0
    w_in_t = jnp.swapaxes(w_in, 1, 2)
    lane_pad = lambda a, off: jnp.pad(a, ((0, 0), (off, LANE - off - a.shape[1]))).reshape(depth, 1, LANE)
    al_pad, dt_pad = lane_pad(dn_a_log, heads), lane_pad(dn_dt_bias, heads)
    dn_ng = dn_norm_g.reshape(depth, 1, DN_HEAD_DIM)
    vec3 = lambda a: a.reshape(depth, 1, a.shape[-1])
    lru_p = dict(lru_conv_w=lru_conv_w, lru_conv_b=vec3(lru_conv_b), lru_wa=lru_wa, lru_ba=vec3(lru_ba),
                 lru_wx=lru_wx, lru_bx=vec3(lru_bx), lru_lambda=vec3(lru_lambda))
    gm_g3, gm_b3 = vec3(gm_ln_g), vec3(gm_ln_b)
    gm_bst = jnp.pad(jnp.swapaxes(gm_bs, 1, 2), ((0, 0), (0, 0), (0, LANE - gm_bs.shape[1])))
    gw = w // gm_ws.shape[1]
    gm_cw = jnp.repeat(jnp.swapaxes(gm_ws[:, :, :dec, :dec], 1, 3).reshape(depth, dec, dec, -1), gw, axis=-1)
    gm_cw = jnp.swapaxes(gm_cw, 1, 2).reshape(depth, dec * dec, w)
    gm_cb = jnp.repeat(jnp.swapaxes(gm_bs[:, :, :dec], 1, 2), gw, axis=-1)
    gm_cb = jnp.pad(gm_cb, ((0, 0), (0, SUBLANE - dec), (0, 0)))
    wb3 = w_branch.reshape(depth * 3, w, d)
    n_moe = router_w.shape[0]
    rw_pad = jnp.pad(router_w, ((0, 0), (0, 0), (0, LANE - n_exp)))
    rb_pad = jnp.pad(router_b, ((0, 0), (0, LANE - n_exp)), constant_values=NEG_BIG).reshape(n_moe, 1, LANE)
    d_ff = ffn_w_gate.shape[2]
    moe_wg = moe_w_gate.reshape(n_moe * n_exp, d, d_ff)
    moe_wu = moe_w_up.reshape(n_moe * n_exp, d, d_ff)
    moe_wd = moe_w_down.reshape(n_moe * n_exp, d_ff, d)

    x = jnp.concatenate([x_prompt.reshape(tp, d), _time_major(x_sample)], axis=0)
    s_chain = jnp.zeros(state_dn_S.shape, F32)
    outs = {k: [] for k in ("S_p", "dc_p", "h_p", "lc_p", "dc_s", "h_s", "lc_s", "v_s")}

    def last_rows(a):
        return jnp.stack([a[(b + 1) * seq - (CONV_W - 1):(b + 1) * seq] for b in range(bp)])

    def to_blocks(a):
        a = jnp.pad(_batch_major(a, bs), ((0, 0), (0, SUBLANE - dec), (0, 0)))
        return a.reshape(bs * SUBLANE, a.shape[2])

    (hh,) = _post(x, None, cfg, res=None, nxt=(0, 0, 1, 0), name="pre0")
    for l in range(depth):
        meta = dense_meta(l)
        proj = functools.partial(_mm, hh, w_in_t, meta, w_nk=True, tm=tm_d, tn=TN_WIDE)
        proj16 = functools.partial(proj, out_dtype=BF16)
        qkv = proj(name="in_qkv", n=n_qkv)
        zs = proj16(name="in_z", n_off=n_qkv, n=n_z, act="silu")
        ba = proj(name="in_ba", n_off=c_ba, n=LANE, tn=LANE)
        uv = proj16(name="in_uv", n_off=c_rest, n=2 * w, act="gelu")
        lx = proj(name="in_lx", n_off=c_rest + 2 * w, n=w)
        ly = proj16(name="in_ly", n_off=c_rest + 3 * w, n=w, act="gelu")
        gates = proj16(name="in_merge", n_off=c_rest + 4 * w, n=3 * d, act="sigmoid")

        dn_hist = _time_major(state_dn_conv[l])
        qkv_s = _dn_prep_sample(qkv, dn_hist, dn_conv_w, l, cfg)
        oa_p, s_p = _dn_prompt(qkv, dn_conv_w, zs, ba, al_pad, dt_pad, dn_ng, l, cfg)
        oa_sb, s_chain = _dn_sample(to_blocks(qkv_s), to_blocks(zs[tp:]), to_blocks(ba[tp:]),
                                    state_dn_S, s_chain, al_pad, dt_pad, dn_ng, l, n_valid=dec)
        oa = jnp.concatenate([oa_p, _time_major(oa_sb.reshape(bs, SUBLANE, w)[:, :dec])], axis=0)
        dc_hist = jnp.concatenate([dn_hist, qkv[tp:]], axis=0)

        ob, vn_s = _gmlp(uv, gm_g3, gm_b3, gm_ws, gm_bst, gm_cw, gm_cb, l, cfg)
        lru_hist = _time_major(state_lru_conv[l])
        oc, hl_p, hl_s = _lru(lx, ly, lru_hist, state_lru_h, lru_p, l, cfg)
        lc_hist = jnp.concatenate([lru_hist, lx[tp:]], axis=0)

        outs["S_p"].append(s_p)
        outs["dc_p"].append(last_rows(qkv))
        outs["h_p"].append(hl_p)
        outs["lc_p"].append(last_rows(lx))
        outs["dc_s"].append(_batch_major(dc_hist[dec * bs:], bs))
        outs["h_s"].append(hl_s)
        outs["lc_s"].append(_batch_major(lc_hist[dec * bs:], bs))
        outs["v_s"].append(_batch_major(vn_s, bs))

        m = _branch_mm(oa, ob, oc, gates, wb3, meta, name="branch", tm=tm_d)
        y = _mm(m, w_out, meta, name="w_out", tm=tm_d, tn=TN_WIDE, out_dtype=BF16)
        j = l // 2
        if l % 2 == 0:
            x, hh = _post(x, y, cfg, res=(l, 1, 2), nxt=(l, 2, 4, 3), name="post_mix")
            hid = _ffn_up(hh, ffn_w_gate, ffn_w_up, dense_meta(j, DENSE_ROW_TILES // 2), name="ffn_up",
                          tm=2 * tm_d)
            y = _mm(hid, ffn_w_down, dense_meta(j, 2 * DENSE_ROW_TILES), name="ffn_down", tm=tm_d // 2,
                    out_dtype=BF16)
        else:
            x, hh32, route = _post(x, y, cfg, res=(l, 1, 2), nxt=(l, 2, 4, 3), router=(rw_pad, rb_pad, j),
                                   hh_dtype=F32, name="post_mix_r")
            d0, d1, src, tile_e, nused = _moe_plan(route, n_exp, TM)
            xs = _dispatch(hh32, src, nused.reshape(1), tm=TM, name="moe_dispatch")
            emeta = jnp.concatenate([j * n_exp + tile_e, nused.reshape(1)])
            hid = _ffn_up(xs, moe_wg, moe_wu, emeta, name="moe_up")
            ys = _mm(hid, moe_wd, emeta, name="moe_down")
            y = _combine(ys, d0, d1, route, name="moe_combine")
        if l + 1 < depth:
            x, hh = _post(x, y, cfg, res=(l, 3, 5), nxt=(l + 1, 0, 1, 0), name="post_ffn")
        else:
            (x,) = _post(x, y, cfg, res=(l, 3, 5), nxt=None, name="post_last")

    st = lambda k: jnp.stack(outs[k])
    return (x[:tp].reshape(bp, seq, d), _batch_major(x[tp:], bs),
            st("S_p"), st("dc_p"), st("h_p"), st("lc_p"),
            s_chain, st("dc_s"), st("h_s"), st("lc_s"), st("v_s"))
```

```python
import functools

import jax
import jax.numpy as jnp
from jax import lax
from jax.experimental import pallas as pl
from jax.experimental.pallas import tpu as pltpu

F32, BF16, I32 = jnp.float32, jnp.bfloat16, jnp.int32
HI = lax.Precision.HIGHEST

LANE = 128
SUBLANE = 8
VMEM_LIMIT = 56 << 20
DN_HEAD_DIM = 128
DN_CHUNK = 64
CONV_W = 4
GM_CHUNK = 128
LRU_BLOCKS = 8
LRU_C = 8.0
N_MOD = 6
TOP_K = 2
RMS_EPS = 1e-6
LN_EPS = 1e-5
L2_EPS = 1e-6
NEG_BIG = -1e30

TM = 512
TN = 512
TN_WIDE = 1024
DENSE_ROW_TILES = 8
TE = 512
DMA_ISSUE_UNROLL = 8


def _params(sem):
    return pltpu.CompilerParams(dimension_semantics=sem, vmem_limit_bytes=VMEM_LIMIT)


def _act(name, x):
    if name is None:
        return x
    if name == "silu":
        return x * jax.nn.sigmoid(x)
    if name == "gelu":
        return jax.nn.gelu(x)
    if name == "sigmoid":
        return jax.nn.sigmoid(x)
    raise ValueError(name)


def _rms(x, g):
    return x * lax.rsqrt(jnp.mean(x * x, axis=-1, keepdims=True) + RMS_EPS) * g


def _cast_once(meta_ref, i, pairs):
    prev = meta_ref[jnp.maximum(i - 1, 0)]

    @pl.when((i == 0) | (meta_ref[i] != prev))
    def _():
        for src, dst in pairs:
            dst[...] = src[...].astype(BF16)


def _mm_body(meta_ref, x_ref, w_ref, *rest, ni, act, pre_act, cast_w, has_bias, w_nk):
    rest = list(rest)
    b_ref = rest.pop(0) if has_bias else None
    o_ref = rest.pop(0)
    wbf_ref = rest.pop(0) if cast_w else None
    i = pl.program_id(1)
    nused = meta_ref[ni]

    @pl.when(i < nused)
    def _():
        if cast_w:
            _cast_once(meta_ref, i, [(w_ref, wbf_ref)])
            w = wbf_ref[...]
        else:
            w = w_ref[...]
        w = w[0] if w_nk else w
        x = x_ref[...]
        if pre_act is not None:
            x = _act(pre_act, x.astype(F32))
        dims = (((1,), (1,)), ((), ())) if w_nk else (((1,), (0,)), ((), ()))
        acc = lax.dot_general(x.astype(BF16), w, dims, preferred_element_type=F32)
        if has_bias:
            acc = acc + b_ref[...]
        o_ref[...] = _act(act, acc).astype(o_ref.dtype)

    @pl.when(i >= nused)
    def _():
        o_ref[...] = jnp.zeros_like(o_ref)


def _mm(x, w, meta, *, name, n_off=0, n=None, tm=TM, tn=TN, act=None, pre_act=None,
        bias=None, out_dtype=F32, w_nk=False):
    m_rows, k = x.shape
    n = w.shape[1 if w_nk else 2] if n is None else n
    tm = min(tm, m_rows)
    ni, nj = pl.cdiv(m_rows, tm), n // tn
    assert n % tn == 0 and meta.shape[0] == ni + 1
    cast_w = w.dtype != BF16
    body = functools.partial(_mm_body, ni=ni, act=act, pre_act=pre_act, cast_w=cast_w,
                             has_bias=bias is not None, w_nk=w_nk)
    if w_nk:
        assert n_off % SUBLANE == 0 and tn % SUBLANE == 0
        wspec = pl.BlockSpec((pl.Element(1), pl.Element(tn), pl.Element(k)),
                             lambda j, i, m: (m[i], (n_off // SUBLANE + j * (tn // SUBLANE)) * SUBLANE, 0))
    else:
        wspec = pl.BlockSpec((None, k, tn), lambda j, i, m: (m[i], 0, j + n_off))
    in_specs = [pl.BlockSpec((tm, k), lambda j, i, m: (jnp.minimum(i, m[ni] - 1), 0)), wspec]
    args = [x, w]
    if bias is not None:
        in_specs.append(pl.BlockSpec((None, 1, tn), lambda j, i, m: (m[i], 0, j + n_off)))
        args.append(bias)
    scratch = [pltpu.VMEM((1, tn, k) if w_nk else (k, tn), BF16)] if cast_w else []
    return pl.pallas_call(
        body,
        grid_spec=pltpu.PrefetchScalarGridSpec(
            num_scalar_prefetch=1, grid=(nj, ni), in_specs=in_specs,
            out_specs=pl.BlockSpec((tm, tn), lambda j, i, m: (i, j)), scratch_shapes=scratch),
        out_shape=jax.ShapeDtypeStruct((m_rows, n), out_dtype),
        compiler_params=_params(("arbitrary", "arbitrary")), name=name)(meta, *args)


def _ffn_up_body(meta_ref, x_ref, wg_ref, wu_ref, o_ref, wgb_ref, wub_ref, *, ni):
    i = pl.program_id(1)
    nused = meta_ref[ni]

    @pl.when(i < nused)
    def _():
        _cast_once(meta_ref, i, [(wg_ref, wgb_ref), (wu_ref, wub_ref)])
        x = x_ref[...]
        g = jnp.dot(x, wgb_ref[...], preferred_element_type=F32)
        u = jnp.dot(x, wub_ref[...], preferred_element_type=F32)
        o_ref[...] = (g * jax.nn.sigmoid(g) * u).astype(o_ref.dtype)

    @pl.when(i >= nused)
    def _():
        o_ref[...] = jnp.zeros_like(o_ref)


def _ffn_up(x, wg, wu, meta, *, name, tm=TM, tn=TN):
    m_rows, k = x.shape
    n = wg.shape[2]
    ni, nj = m_rows // tm, n // tn
    assert m_rows % tm == 0 and n % tn == 0 and meta.shape[0] == ni + 1
    wspec = pl.BlockSpec((None, k, tn), lambda j, i, m: (m[i], 0, j))
    return pl.pallas_call(
        functools.partial(_ffn_up_body, ni=ni),
        grid_spec=pltpu.PrefetchScalarGridSpec(
            num_scalar_prefetch=1, grid=(nj, ni),
            in_specs=[pl.BlockSpec((tm, k), lambda j, i, m: (jnp.minimum(i, m[ni] - 1), 0)),
                      wspec, wspec],
            out_specs=pl.BlockSpec((tm, tn), lambda j, i, m: (i, j)),
            scratch_shapes=[pltpu.VMEM((k, tn), BF16), pltpu.VMEM((k, tn), BF16)]),
        out_shape=jax.ShapeDtypeStruct((m_rows, n), BF16),
        compiler_params=_params(("arbitrary", "arbitrary")), name=name)(meta, x, wg, wu)


def _branch_body(meta_ref, xa_ref, xb_ref, xc_ref, ga_ref, gb_ref, gc_ref,
                 w0_ref, w1_ref, w2_ref, o_ref, s0_ref, s1_ref, s2_ref):
    i = pl.program_id(1)
    _cast_once(meta_ref, i, [(w0_ref, s0_ref), (w1_ref, s1_ref), (w2_ref, s2_ref)])
    acc = ga_ref[...] * jnp.dot(xa_ref[...], s0_ref[...], preferred_element_type=F32)
    acc = acc + gb_ref[...] * jnp.dot(xb_ref[...], s1_ref[...], preferred_element_type=F32)
    acc = acc + gc_ref[...] * jnp.dot(xc_ref[...], s2_ref[...], preferred_element_type=F32)
    o_ref[...] = acc.astype(o_ref.dtype)


def _branch_mm(oa, ob, oc, gates, wb, meta, *, name, tm=TM, tn=TN):
    m_rows, k = oa.shape
    n = wb.shape[2]
    ni, nj = m_rows // tm, n // tn
    xspec = pl.BlockSpec((tm, k), lambda j, i, m: (i, 0))

    def gspec(b):
        return pl.BlockSpec((tm, tn), lambda j, i, m: (i, j + b * nj))

    def wspec(b):
        return pl.BlockSpec((None, k, tn), lambda j, i, m: (m[i] * 3 + b, 0, j))

    return pl.pallas_call(
        _branch_body,
        grid_spec=pltpu.PrefetchScalarGridSpec(
            num_scalar_prefetch=1, grid=(nj, ni),
            in_specs=[xspec, xspec, xspec, gspec(0), gspec(1), gspec(2), wspec(0), wspec(1), wspec(2)],
            out_specs=pl.BlockSpec((tm, tn), lambda j, i, m: (i, j)),
            scratch_shapes=[pltpu.VMEM((k, tn), BF16)] * 3),
        out_shape=jax.ShapeDtypeStruct((m_rows, n), BF16),
        compiler_params=_params(("arbitrary", "arbitrary")), name=name)(
            meta, oa, ob, oc, gates, gates, gates, wb, wb, wb)


def _route(hh, rw_ref, rb_ref):
    lg = jnp.dot(hh, rw_ref[...], precision=HI, preferred_element_type=F32) + rb_ref[...]
    lane = lax.broadcasted_iota(I32, lg.shape, 1)
    m1 = jnp.max(lg, axis=1, keepdims=True)
    i1 = jnp.min(jnp.where(lg == m1, lane, LANE), axis=1, keepdims=True)
    lg2 = jnp.where(lane == i1, NEG_BIG * 2, lg)
    m2 = jnp.max(lg2, axis=1, keepdims=True)
    i2 = jnp.min(jnp.where(lg2 == m2, lane, LANE), axis=1, keepdims=True)
    e = jnp.exp(m2 - m1)
    p1 = 1.0 / (1.0 + e)
    p2 = e / (1.0 + e)
    return jnp.where(lane == 0, p1, jnp.where(lane == 1, p2, jnp.where(
        lane == 2, i1.astype(F32), jnp.where(lane == 3, i2.astype(F32), 0.0))))


def _post_body(*refs, npt, reps, res, nxt, router):
    it = iter(refs)
    x_ref = next(it)
    y_ref = next(it) if res else None
    gres_ref = next(it) if res else None
    gnx_ref = next(it) if nxt else None
    mpr_ref = next(it) if res else None
    mpn_ref = next(it) if nxt else None
    msg_ref = next(it) if res else None
    msc_ref = next(it) if nxt else None
    msh_ref = next(it) if nxt else None
    rw_ref = next(it) if router else None
    rb_ref = next(it) if router else None
    xo_ref = next(it) if res else None
    hh_ref = next(it) if nxt else None
    rt_ref = next(it) if router else None
    i = pl.program_id(0)

    def compute(gate, sc, sh):
        x = x_ref[...]
        if res:
            x = x + gate * _rms(y_ref[...].astype(F32), gres_ref[...])
            xo_ref[...] = x
        if nxt:
            hh = _rms(x, gnx_ref[...]) * (1.0 + sc) + sh
            hh_ref[...] = hh.astype(hh_ref.dtype)
            if router:
                rt_ref[...] = _route(hh, rw_ref, rb_ref)

    def rep(r):
        return jnp.concatenate([r[...]] * reps, axis=0) if reps > 1 else r[...]

    @pl.when(i < npt)
    def _():
        compute(mpr_ref[res[2]:res[2] + 1, :] if res else None,
                mpn_ref[nxt[2]:nxt[2] + 1, :] if nxt else None,
                mpn_ref[nxt[3]:nxt[3] + 1, :] if nxt else None)

    @pl.when(i >= npt)
    def _():
        compute(rep(msg_ref) if res else None, rep(msc_ref) if nxt else None,
                rep(msh_ref) if nxt else None)


def _post(x, y, cfg, *, res, nxt, router=None, hh_dtype=BF16, name):
    t_rows, d = x.shape
    te, bp, bs = TE, cfg["bp"], cfg["bs"]
    npt, nt, tpb = cfg["tp"] // te, t_rows // te, cfg["seq"] // te
    assert te % bs == 0 and cfg["seq"] % te == 0 and t_rows % te == 0
    row = pl.BlockSpec((te, d), lambda i: (i, 0))
    in_specs, args = [row], [x]
    if res:
        in_specs.append(row)
        args.append(y)

    def gspec(layer, r):
        return pl.BlockSpec((None, 1, d), lambda i: (layer * 4 + r, 0, 0))

    def mpspec(layer):
        return pl.BlockSpec((None, N_MOD, d), lambda i: (layer * bp + jnp.minimum(i // tpb, bp - 1), 0, 0))

    def msspec(layer, plane):
        return pl.BlockSpec((None, bs, d), lambda i: (layer * N_MOD + plane, 0, 0))

    if res:
        in_specs.append(gspec(res[0], res[1])); args.append(cfg["norm_g"])
    if nxt:
        in_specs.append(gspec(nxt[0], nxt[1])); args.append(cfg["norm_g"])
    if res:
        in_specs.append(mpspec(res[0])); args.append(cfg["mod_p"])
    if nxt:
        in_specs.append(mpspec(nxt[0])); args.append(cfg["mod_p"])
    if res:
        in_specs.append(msspec(res[0], res[2])); args.append(cfg["mod_s"])
    if nxt:
        in_specs.append(msspec(nxt[0], nxt[2])); args.append(cfg["mod_s"])
        in_specs.append(msspec(nxt[0], nxt[3])); args.append(cfg["mod_s"])
    if router:
        rw, rb, slab = router
        in_specs.append(pl.BlockSpec((None, d, LANE), lambda i: (slab, 0, 0))); args.append(rw)
        in_specs.append(pl.BlockSpec((None, 1, LANE), lambda i: (slab, 0, 0))); args.append(rb)
    out_specs, out_shape = [], []
    if res:
        out_specs.append(row); out_shape.append(jax.ShapeDtypeStruct((t_rows, d), F32))
    if nxt:
        out_specs.append(row); out_shape.append(jax.ShapeDtypeStruct((t_rows, d), hh_dtype))
    if router:
        out_specs.append(pl.BlockSpec((te, LANE), lambda i: (i, 0)))
        out_shape.append(jax.ShapeDtypeStruct((t_rows, LANE), F32))
    body = functools.partial(_post_body, npt=npt, reps=te // bs, res=res, nxt=nxt, router=bool(router))
    return pl.pallas_call(body, grid=(nt,), in_specs=in_specs, out_specs=out_specs, out_shape=out_shape,
                          compiler_params=_params(("arbitrary",)), name=name)(*args)


def _row_copy(src_hbm, row, dst_ref, r, sem):
    return pltpu.make_async_copy(src_hbm.at[pl.ds(row, 1), :], dst_ref.at[pl.ds(r, 1), :], sem)


def _gather_rows(idx_ref, base, src_hbm, dst_ref, sem, n):
    def start(g, c):
        for u in range(DMA_ISSUE_UNROLL):
            r = g * DMA_ISSUE_UNROLL + u
            _row_copy(src_hbm, idx_ref[base + r], dst_ref, r, sem).start(priority=u % 2)
        return c

    def wait(r, c):
        _row_copy(src_hbm, 0, dst_ref, r, sem).wait()
        return c

    lax.fori_loop(0, n // DMA_ISSUE_UNROLL, start, 0)
    lax.fori_loop(0, n, wait, 0, unroll=DMA_ISSUE_UNROLL)


def _dispatch_body(src_ref, nused_ref, x_hbm, o_ref, buf_ref, sem, *, tm):
    i = pl.program_id(0)

    @pl.when(i < nused_ref[0])
    def _():
        _gather_rows(src_ref, i * tm, x_hbm, buf_ref, sem, tm)
        o_ref[...] = buf_ref[...].astype(o_ref.dtype)

    @pl.when(i >= nused_ref[0])
    def _():
        o_ref[...] = jnp.zeros_like(o_ref)


def _dispatch(hh, src, nused, *, tm, name):
    cap = src.shape[0]
    d = hh.shape[1]
    return pl.pallas_call(
        functools.partial(_dispatch_body, tm=tm),
        grid_spec=pltpu.PrefetchScalarGridSpec(
            num_scalar_prefetch=2, grid=(cap // tm,),
            in_specs=[pl.BlockSpec(memory_space=pl.ANY)],
            out_specs=pl.BlockSpec((tm, d), lambda i, s, n: (i, 0)),
            scratch_shapes=[pltpu.VMEM((tm, d), hh.dtype), pltpu.SemaphoreType.DMA(())]),
        out_shape=jax.ShapeDtypeStruct((cap, d), BF16),
        compiler_params=_params(("arbitrary",)), name=name)(src, nused, hh)


def _combine_body(d0_ref, d1_ref, y_hbm, p_ref, o_ref, a_ref, b_ref, sem_a, sem_b, *, te):
    i = pl.program_id(0)

    def start(r, c):
        _row_copy(y_hbm, d0_ref[i * te + r], a_ref, r, sem_a).start(priority=0)
        _row_copy(y_hbm, d1_ref[i * te + r], b_ref, r, sem_b).start(priority=1)
        return c

    def wait(r, c):
        _row_copy(y_hbm, 0, a_ref, r, sem_a).wait()
        _row_copy(y_hbm, 0, b_ref, r, sem_b).wait()
        return c

    lax.fori_loop(0, te, start, 0, unroll=DMA_ISSUE_UNROLL)
    lax.fori_loop(0, te, wait, 0, unroll=DMA_ISSUE_UNROLL)
    p = p_ref[...]
    o_ref[...] = (p[:, 0:1] * a_ref[...] + p[:, 1:2] * b_ref[...]).astype(o_ref.dtype)


def _combine(ys, dest0, dest1, route, *, name):
    t_rows = route.shape[0]
    d = ys.shape[1]
    te = TE
    return pl.pallas_call(
        functools.partial(_combine_body, te=te),
        grid_spec=pltpu.PrefetchScalarGridSpec(
            num_scalar_prefetch=2, grid=(t_rows // te,),
            in_specs=[pl.BlockSpec(memory_space=pl.ANY),
                      pl.BlockSpec((te, LANE), lambda i, a, b: (i, 0))],
            out_specs=pl.BlockSpec((te, d), lambda i, a, b: (i, 0)),
            scratch_shapes=[pltpu.VMEM((te, d), F32), pltpu.VMEM((te, d), F32),
                            pltpu.SemaphoreType.DMA(()), pltpu.SemaphoreType.DMA(())]),
        out_shape=jax.ShapeDtypeStruct((t_rows, d), BF16),
        compiler_params=_params(("arbitrary",)), name=name)(dest0, dest1, ys, route)


def _dn_post(y, j):
    y = y * jax.nn.sigmoid(y)
    outs = []
    for h in range(y.shape[1] // DN_HEAD_DIM):
        yh = y[:, h * DN_HEAD_DIM:(h + 1) * DN_HEAD_DIM]
        n = lax.rsqrt(jnp.sum(yh * yh, axis=-1, keepdims=True) + L2_EPS)
        s = jnp.where(j == 0, n * (DN_HEAD_DIM ** -0.5), jnp.where(j == 1, n, 1.0))
        outs.append(yh * s)
    return jnp.concatenate(outs, axis=1)


def _conv_rows(x_ref, prev_ref, buf_ref, w, first):
    te = x_ref.shape[0]
    x = x_ref[...]
    buf_ref[0:SUBLANE, :] = jnp.where(first, 0.0, prev_ref[...])
    buf_ref[SUBLANE:, :] = x
    y = w[CONV_W - 1:CONV_W, :] * x
    for dlt in range(1, CONV_W):
        y = y + w[CONV_W - 1 - dlt:CONV_W - dlt, :] * buf_ref[SUBLANE - dlt:SUBLANE - dlt + te, :]
    return y


def _conv_slabs(x_ref, hist_ref, w, bs):
    xp = jnp.concatenate([hist_ref[...], x_ref[...]], axis=0)
    steps = x_ref.shape[0] // bs
    ys = []
    for t in range(steps):
        y = w[0:1, :] * xp[t * bs:(t + 1) * bs]
        for j in range(1, CONV_W):
            y = y + w[j:j + 1, :] * xp[(t + j) * bs:(t + j + 1) * bs]
        ys.append(y)
    return ys


def _dn_prep_s_body(x_ref, hist_ref, w_ref, o_ref, *, bs):
    ys = _conv_slabs(x_ref, hist_ref, w_ref[...], bs)
    o_ref[...] = _dn_post(jnp.concatenate(ys, axis=0), pl.program_id(0))


def _dn_prep_sample(qkv, hist_s, conv_w, layer, cfg):
    tp, ts, bs = cfg["tp"], cfg["ts"], cfg["bs"]
    w3 = qkv.shape[1]
    cb = w3 // 3
    return pl.pallas_call(
        functools.partial(_dn_prep_s_body, bs=bs), grid=(3,),
        in_specs=[pl.BlockSpec((ts, cb), lambda j: (tp // ts, j)),
                  pl.BlockSpec(((CONV_W - 1) * bs, cb), lambda j: (0, j)),
                  pl.BlockSpec((None, CONV_W, cb), lambda j: (layer, 0, j))],
        out_specs=pl.BlockSpec((ts, cb), lambda j: (0, j)),
        out_shape=jax.ShapeDtypeStruct((ts, w3), F32),
        compiler_params=_params(("arbitrary",)), name="dn_prep_s")(qkv, hist_s, conv_w)


def _dotf(a, b):
    return jnp.dot(a, b, precision=HI, preferred_element_type=F32)


def _dotb(a, b):
    return jnp.dot(a.astype(BF16), b.astype(BF16), preferred_element_type=F32)


def _dot_nt(a, b):
    return lax.dot_general(a, b, (((1,), (1,)), ((), ())), preferred_element_type=F32)


def _dot_tn(a, b):
    return lax.dot_general(a, b, (((0,), (0,)), ((), ())), preferred_element_type=F32)


def _dn_gates(ba, al, dt, heads, valid):
    lanel = lax.broadcasted_iota(I32, ba.shape, 1)
    beta = jnp.where(valid & (lanel < heads), jax.nn.sigmoid(ba), 0.0)
    g = jnp.where(valid, -jnp.exp(al) * jax.nn.softplus(ba + dt), 0.0)
    return beta, g


def _dn_wy_body(x_ref, prev_ref, cw_ref, ba_ref, al_ref, dt_ref,
                w_ref, u_ref, qg_ref, kd_ref, sc_ref, eg_ref, m_ref, r_ref, p_ref, buf_ref, *, steps_per_seq):
    c = DN_CHUNK
    rows_n = x_ref.shape[0]
    wq = x_ref.shape[1] // 3
    heads = wq // DN_HEAD_DIM
    p2 = 2 * c
    y = _conv_rows(x_ref, prev_ref, buf_ref, cw_ref[...], pl.program_id(0) % steps_per_seq == 0)
    for j in range(3):
        p_ref[j] = _dn_post(y[:, j * wq:(j + 1) * wq], j)
    q_ref, k_ref, v_ref = p_ref.at[0], p_ref.at[1], p_ref.at[2]
    beta_all, g_all = _dn_gates(ba_ref[...], al_ref[...], dt_ref[...], heads, True)
    rr = lax.broadcasted_iota(I32, (rows_n, rows_n), 0)
    cc = lax.broadcasted_iota(I32, (rows_n, rows_n), 1)
    gcum_all = _dotf(((rr >= cc) & (rr // c == cc // c)).astype(F32), g_all)
    row = lax.broadcasted_iota(I32, (p2, p2), 0)
    col = lax.broadcasted_iota(I32, (p2, p2), 1)
    same = (row // c) == (col // c)
    tri = (row >= col) & same
    strict = (row > col) & same

    eye = (row == col).astype(F32)
    blk = (row // SUBLANE) == (col // SUBLANE)

    def stack(ref, rs, h):
        return jnp.concatenate([ref[rs, (h + a) * DN_HEAD_DIM:(h + a + 1) * DN_HEAD_DIM] for a in (0, 1)], axis=0)

    def cols(arr, rs, lane0):
        return jnp.concatenate([arr[rs, lane0 + a:lane0 + a + 1] for a in (0, 1)], axis=0)

    probs = [(ci, h) for ci in range(rows_n // c) for h in range(0, heads, 2)]
    n_p = len(probs)

    for p, (ci, h) in enumerate(probs):
        rs = slice(ci * c, (ci + 1) * c)
        q, k = stack(q_ref, rs, h), stack(k_ref, rs, h)
        beta = cols(beta_all, rs, h)
        gcol = cols(gcum_all, rs, heads + h)
        glast = jnp.concatenate(
            [jnp.broadcast_to(gcum_all[(ci + 1) * c - 1:(ci + 1) * c, heads + h + a:heads + h + a + 1], (c, 1))
             for a in (0, 1)], axis=0)
        gmat = jnp.broadcast_to(gcol, (p2, p2))
        decay = jnp.exp(jnp.where(tri, gmat - gmat.T, NEG_BIG))
        kb = k * beta
        k16 = k.astype(BF16)
        lm = jnp.where(strict, _dot_nt(kb.astype(BF16), k16) * decay, 0.0)
        m_ref[0, p] = lm
        m_ref[1, p] = jnp.where(blk, lm, 0.0)
        sc = _dot_nt(q.astype(BF16), k16) * decay
        egc = jnp.exp(gcol)
        r_ref[0, p] = (kb * egc).astype(BF16)
        r_ref[1, p] = (stack(v_ref, rs, h) * beta).astype(BF16)
        qg = q * egc
        kd = k * jnp.exp(glast - gcol)
        egl = jnp.exp(glast)
        for a in (0, 1):
            hs = slice(a * c, (a + 1) * c)
            qg_ref[ci, h + a] = qg[hs].astype(qg_ref.dtype)
            kd_ref[ci, h + a] = kd[hs].astype(kd_ref.dtype)
            sc_ref[ci, h + a] = sc[hs, hs].astype(sc_ref.dtype)
            eg_ref[ci, h + a] = jnp.broadcast_to(egl[a * c:a * c + 1], (1, LANE))

    def stage(dst, fa, fb):
        for p in range(n_p):
            m_ref[dst, p] = _dotb(fa(p), fb(p))

    ld = lambda p: m_ref[1, p]
    stage(2, ld, ld)
    stage(3, lambda p: m_ref[2, p], lambda p: m_ref[2, p])
    stage(4, lambda p: eye - ld(p), lambda p: eye + m_ref[2, p])
    stage(5, lambda p: m_ref[4, p], lambda p: eye + m_ref[3, p])
    stage(2, lambda p: m_ref[5, p], lambda p: m_ref[0, p] - ld(p))
    stage(3, lambda p: m_ref[2, p], lambda p: m_ref[2, p])
    stage(4, lambda p: m_ref[3, p], lambda p: m_ref[3, p])
    stage(6, lambda p: eye - m_ref[2, p], lambda p: eye + m_ref[3, p])
    stage(7, lambda p: m_ref[6, p], lambda p: eye + m_ref[4, p])
    stage(2, lambda p: m_ref[7, p], lambda p: m_ref[5, p])
    for p, (ci, h) in enumerate(probs):
        t16 = m_ref[2, p].astype(BF16)
        wm = jnp.dot(t16, r_ref[0, p], preferred_element_type=F32)
        um = jnp.dot(t16, r_ref[1, p], preferred_element_type=F32)
        for a in (0, 1):
            hs = slice(a * c, (a + 1) * c)
            w_ref[ci, h + a] = wm[hs].astype(w_ref.dtype)
            u_ref[ci, h + a] = um[hs]


def _dn_seq_body(w_ref, u_ref, qg_ref, kd_ref, sc_ref, eg_ref, z_ref, ng_ref, o_ref, so_ref, s_ref, r_ref):
    nck, heads, c = w_ref.shape[0], w_ref.shape[1], w_ref.shape[2]
    ci = pl.program_id(1)

    @pl.when(ci == 0)
    def _():
        s_ref[...] = jnp.zeros_like(s_ref)

    def chunk(n, carry):
        rs = pl.ds(pl.multiple_of(n * c, c), c)
        for h in range(heads):
            r_ref[h] = jnp.dot(jnp.concatenate([w_ref[n, h], qg_ref[n, h]], axis=0),
                               s_ref[h].astype(BF16), preferred_element_type=F32)
        for h in range(heads):
            hs = slice(h * DN_HEAD_DIM, (h + 1) * DN_HEAD_DIM)
            vn16 = (u_ref[n, h] - r_ref[h, 0:c]).astype(BF16)
            o = r_ref[h, c:2 * c] + jnp.dot(sc_ref[n, h], vn16, preferred_element_type=F32)
            s_ref[h] = s_ref[h] * eg_ref[n, h] + _dot_tn(kd_ref[n, h], vn16)
            o_ref[rs, hs] = (_rms(o, ng_ref[...]) * z_ref[rs, hs]).astype(o_ref.dtype)
        return carry

    lax.fori_loop(0, nck, chunk, 0)

    @pl.when(ci == pl.num_programs(1) - 1)
    def _():
        so_ref[...] = s_ref[...]


DN_WY_CHUNKS = 4
DN_SEQ_CHUNKS = 8


def _dn_prompt(qkv, conv_w, z, ba, al, dt, ng, layer, cfg):
    tp, seq, bp = cfg["tp"], cfg["seq"], cfg["bp"]
    w3 = qkv.shape[1]
    w = w3 // 3
    heads = w // DN_HEAD_DIM
    c, d = DN_CHUNK, DN_HEAD_DIM
    nc = tp // c
    kw, ks = DN_WY_CHUNKS, DN_SEQ_CHUNKS
    assert nc % kw == 0 and (seq // c) % ks == 0 and heads % 2 == 0
    vec = pl.BlockSpec((None, 1, LANE), lambda *_: (layer, 0, 0))
    hm = lambda n, last: pl.BlockSpec((n, heads, c, last), lambda i: (i, 0, 0, 0))
    eg_spec = lambda n: pl.BlockSpec((n, heads, 1, LANE), lambda i: (i, 0, 0, 0))
    sds = jax.ShapeDtypeStruct
    rpb = kw * c // SUBLANE
    wm, um, qg, kd, sc, eg = pl.pallas_call(
        functools.partial(_dn_wy_body, steps_per_seq=seq // (kw * c)), grid=(nc // kw,),
        in_specs=[pl.BlockSpec((kw * c, w3), lambda i: (i, 0)),
                  pl.BlockSpec((SUBLANE, w3), lambda i: (jnp.maximum(i * rpb - 1, 0), 0)),
                  pl.BlockSpec((None, CONV_W, w3), lambda i: (layer, 0, 0)),
                  pl.BlockSpec((kw * c, LANE), lambda i: (i, 0)), vec, vec],
        out_specs=[hm(kw, d), hm(kw, d), hm(kw, d), hm(kw, d), hm(kw, c), eg_spec(kw)],
        out_shape=[sds((nc, heads, c, d), BF16), sds((nc, heads, c, d), F32), sds((nc, heads, c, d), BF16),
                   sds((nc, heads, c, d), BF16), sds((nc, heads, c, c), BF16), sds((nc, heads, 1, LANE), F32)],
        scratch_shapes=[pltpu.VMEM((8, kw * heads // 2, 2 * c, 2 * c), F32),
                        pltpu.VMEM((2, kw * heads // 2, 2 * c, d), BF16),
                        pltpu.VMEM((3, kw * c, w), F32), pltpu.VMEM((kw * c + SUBLANE, w3), F32)],
        compiler_params=_params(("arbitrary",)), name="dn_wy")(qkv, qkv, conv_w, ba, al, dt)
    nps = seq // c // ks
    hm2 = lambda last: pl.BlockSpec((ks, heads, c, last), lambda b, n: (b * nps + n, 0, 0, 0))
    return pl.pallas_call(
        _dn_seq_body, grid=(bp, nps),
        in_specs=[hm2(d), hm2(d), hm2(d), hm2(d), hm2(c),
                  pl.BlockSpec((ks, heads, 1, LANE), lambda b, n: (b * nps + n, 0, 0, 0)),
                  pl.BlockSpec((ks * c, w), lambda b, n: (b * nps + n, 0)),
                  pl.BlockSpec((None, 1, LANE), lambda b, n: (layer, 0, 0))],
        out_specs=[pl.BlockSpec((ks * c, w), lambda b, n: (b * nps + n, 0)),
                   pl.BlockSpec((None, heads, d, d), lambda b, n: (b, 0, 0, 0))],
        out_shape=[sds((tp, w), BF16), sds((bp, heads, d, d), F32)],
        scratch_shapes=[pltpu.VMEM((heads, d, d), F32), pltpu.VMEM((heads, 2 * c, d), F32)],
        compiler_params=_params(("arbitrary", "arbitrary")), name="dn_seq")(wm, um, qg, kd, sc, eg, z, ng)


def _dn_chunk_body(q_ref, k_ref, v_ref, z_ref, ba_ref, s0_ref, al_ref, dt_ref, ng_ref, chain_ref,
                   o_ref, so_ref, m_ref, sc_ref, ws_ref, kd_ref, vn_ref, *, groups, n_valid):
    c = q_ref.shape[0]
    heads = q_ref.shape[1] // DN_HEAD_DIM
    rpg = c // groups
    assert rpg == SUBLANE

    row = lax.broadcasted_iota(I32, (c, c), 0)
    col = lax.broadcasted_iota(I32, (c, c), 1)
    same = (row // rpg) == (col // rpg)
    tri = (row >= col) & same
    strict = (row > col) & same
    rowl = lax.broadcasted_iota(I32, (c, LANE), 0)
    beta_all, g_all = _dn_gates(ba_ref[...], al_ref[...], dt_ref[...], heads, (rowl % rpg) < n_valid)
    gcum_all = _dotf(tri.astype(F32), g_all)
    rg = lax.broadcasted_iota(I32, (c, 1), 0) // rpg
    rg2 = jnp.concatenate([rg, rg], axis=0)

    eye = (row == col).astype(F32)
    glast = lambda h, gi: gcum_all[(gi + 1) * rpg - 1:(gi + 1) * rpg, heads + h:heads + h + 1]

    for h in range(heads):
        hs = slice(h * DN_HEAD_DIM, (h + 1) * DN_HEAD_DIM)
        q, k = q_ref[:, hs], k_ref[:, hs]
        beta = beta_all[:, h:h + 1]
        gcol = gcum_all[:, heads + h:heads + h + 1]
        gmat = jnp.broadcast_to(
            jnp.concatenate([gcol, jnp.zeros((LANE - c, 1), F32)], axis=0), (LANE, LANE))
        decay = jnp.exp(jnp.where(tri, gmat[0:c, 0:c] - gmat.T[0:c, 0:c], NEG_BIG))
        kb = k * beta
        k16 = k.astype(BF16)
        m_ref[0, h] = jnp.where(strict, _dot_nt(kb.astype(BF16), k16) * decay, 0.0)
        sc_ref[h] = (_dot_nt(q.astype(BF16), k16) * decay).astype(BF16)
        egc = jnp.exp(gcol)
        lhs = jnp.concatenate([kb * egc, q * egc], axis=0).astype(BF16)
        ws = jnp.zeros((2 * c, DN_HEAD_DIM), F32)
        glast_rows = jnp.zeros((c, 1), F32)
        for gi in range(groups):
            r = jnp.dot(lhs, s0_ref[gi, h].astype(BF16), preferred_element_type=F32)
            ws = jnp.where(rg2 == gi, r, ws)
            glast_rows = jnp.where(rg == gi, glast(h, gi), glast_rows)
        ws_ref[h] = ws
        kd_ref[h] = (k * jnp.exp(glast_rows - gcol)).astype(BF16)

    def stage(dst, fa, fb):
        for h in range(heads):
            m_ref[dst, h] = _dotb(fa(h), fb(h))

    stage(1, lambda h: m_ref[0, h], lambda h: m_ref[0, h])
    stage(2, lambda h: m_ref[1, h], lambda h: m_ref[1, h])
    stage(3, lambda h: eye - m_ref[0, h], lambda h: eye + m_ref[1, h])
    stage(1, lambda h: m_ref[3, h], lambda h: eye + m_ref[2, h])
    for h in range(heads):
        hs = slice(h * DN_HEAD_DIM, (h + 1) * DN_HEAD_DIM)
        vn_ref[h] = _dotb(m_ref[1, h], v_ref[:, hs] * beta_all[:, h:h + 1] - ws_ref[h, 0:c]).astype(BF16)
    for h in range(heads):
        hs = slice(h * DN_HEAD_DIM, (h + 1) * DN_HEAD_DIM)
        vn16 = vn_ref[h]
        o = ws_ref[h, c:2 * c] + jnp.dot(sc_ref[h], vn16, preferred_element_type=F32)
        kdec = kd_ref[h]
        for gi in range(groups):
            upd = _dot_tn(jnp.where(rg == gi, kdec, jnp.zeros_like(kdec)), vn16)
            so_ref[gi, h] = s0_ref[gi, h] * jnp.exp(glast(h, gi)) + upd
        o_ref[:, hs] = (_rms(o, ng_ref[...]) * z_ref[:, hs]).astype(o_ref.dtype)


def _dn_sample(qkv, z, ba, state, chain, al, dt, ng, layer, *, n_valid):
    rows, w3 = qkv.shape
    w = w3 // 3
    heads = w // DN_HEAD_DIM
    c = DN_CHUNK
    groups = c // SUBLANE
    rspec = lambda col: pl.BlockSpec((c, w), lambda b: (b, col))
    vec = pl.BlockSpec((None, 1, LANE), lambda b: (layer, 0, 0))
    sspec = pl.BlockSpec((None, groups, heads, DN_HEAD_DIM, DN_HEAD_DIM), lambda b: (layer, b, 0, 0, 0))
    return pl.pallas_call(
        functools.partial(_dn_chunk_body, groups=groups, n_valid=n_valid),
        grid=(rows // c,),
        in_specs=[rspec(0), rspec(1), rspec(2), rspec(0), pl.BlockSpec((c, LANE), lambda b: (b, 0)),
                  sspec, vec, vec, vec, pl.BlockSpec(memory_space=pl.ANY)],
        out_specs=[rspec(0), sspec],
        out_shape=[jax.ShapeDtypeStruct((rows, w), BF16), jax.ShapeDtypeStruct(chain.shape, F32)],
        scratch_shapes=[pltpu.VMEM((4, heads, c, c), F32), pltpu.VMEM((heads, c, c), BF16),
                        pltpu.VMEM((heads, 2 * c, DN_HEAD_DIM), F32), pltpu.VMEM((heads, c, DN_HEAD_DIM), BF16),
                        pltpu.VMEM((heads, c, DN_HEAD_DIM), BF16)],
        input_output_aliases={9: 1},
        compiler_params=_params(("arbitrary",)), name="dn_sample")(
            qkv, qkv, qkv, z, ba, state, al, dt, ng, chain)


def _layer_norm(v, g, b):
    mu = jnp.mean(v, axis=-1, keepdims=True)
    vc = v - mu
    return vc * lax.rsqrt(jnp.mean(vc * vc, axis=-1, keepdims=True) + LN_EPS) * g + b


def _gmlp_body(uv_ref, g_ref, b_ref, ws_ref, bst_ref, cw_ref, cb_ref, o_ref, vn_ref, *, npt, bs):
    w = uv_ref.shape[1] // 2
    i = pl.program_id(0)
    vn = _layer_norm(uv_ref[:, w:].astype(F32), g_ref[...], b_ref[...])

    @pl.when(i < npt)
    def _():
        groups = ws_ref.shape[0]
        gw = w // groups
        v16 = vn.astype(BF16)
        row = lax.broadcasted_iota(I32, (GM_CHUNK, GM_CHUNK), 0)
        col = lax.broadcasted_iota(I32, (GM_CHUNK, GM_CHUNK), 1)
        wcs = [jnp.where(row >= col, ws_ref[g], 0.0).astype(BF16) for g in range(groups)]
        bst = bst_ref[...]
        for c in range(uv_ref.shape[0] // GM_CHUNK):
            rs = slice(c * GM_CHUNK, (c + 1) * GM_CHUNK)
            for g in range(groups):
                cs = slice(g * gw, (g + 1) * gw)
                s = jnp.dot(wcs[g], v16[rs, cs], preferred_element_type=F32) + bst[:, g:g + 1]
                o_ref[rs, cs] = (uv_ref[rs, cs] * s).astype(o_ref.dtype)

    @pl.when(i >= npt)
    def _():
        steps = uv_ref.shape[0] // bs
        vn_ref[...] = vn
        for t in range(steps):
            s = cb_ref[t:t + 1, :]
            for u in range(t + 1):
                s = s + cw_ref[t * steps + u:t * steps + u + 1, :] * vn[u * bs:(u + 1) * bs]
            o_ref[t * bs:(t + 1) * bs, :] = (uv_ref[t * bs:(t + 1) * bs, 0:w] * s).astype(o_ref.dtype)


def _gmlp(uv, ln_g, ln_b, ws, bst, cw, cb, layer, cfg):
    tl, bs = cfg["ts"], cfg["bs"]
    t_rows, w2 = uv.shape
    w = w2 // 2
    groups = ws.shape[1]
    steps = tl // bs
    vec = pl.BlockSpec((None, 1, w), lambda i: (layer, 0, 0))
    return pl.pallas_call(
        functools.partial(_gmlp_body, npt=cfg["tp"] // tl, bs=bs), grid=(t_rows // tl,),
        in_specs=[pl.BlockSpec((tl, w2), lambda i: (i, 0)), vec, vec,
                  pl.BlockSpec((None, groups, GM_CHUNK, GM_CHUNK), lambda i: (layer, 0, 0, 0)),
                  pl.BlockSpec((None, GM_CHUNK, LANE), lambda i: (layer, 0, 0)),
                  pl.BlockSpec((None, steps * steps, w), lambda i: (layer, 0, 0)),
                  pl.BlockSpec((None, SUBLANE, w), lambda i: (layer, 0, 0))],
        out_specs=[pl.BlockSpec((tl, w), lambda i: (i, 0)), pl.BlockSpec((tl, w), lambda i: (0, 0))],
        out_shape=[jax.ShapeDtypeStruct((t_rows, w), BF16), jax.ShapeDtypeStruct((tl, w), F32)],
        compiler_params=_params(("arbitrary",)), name="gmlp")(uv, ln_g, ln_b, ws, bst, cw, cb)


def _lru_gates(xc, wa_ref, ba, wx_ref, bx, lam):
    nb = wa_ref.shape[0]
    bw = xc.shape[1] // nb
    x16 = xc.astype(BF16)
    rs, gs = [], []
    for h in range(nb):
        xs = x16[:, h * bw:(h + 1) * bw]
        rs.append(jnp.dot(xs, wa_ref[h].astype(BF16), preferred_element_type=F32))
        gs.append(jnp.dot(xs, wx_ref[h].astype(BF16), preferred_element_type=F32))
    r = jax.nn.sigmoid(jnp.concatenate(rs, axis=1) + ba)
    ig = jax.nn.sigmoid(jnp.concatenate(gs, axis=1) + bx)
    log_a = -LRU_C * r * jax.nn.softplus(-lam)
    th = jnp.tanh(log_a)
    return jnp.exp(log_a), jnp.sqrt(-2.0 * th / (1.0 - th)), ig


def _lru_body(x_ref, prev_ref, y_ref, hist_ref, h0_ref, cw_ref, cb_ref, wa_ref, ba_ref, wx_ref, bx_ref,
              lam_ref, o_ref, hlp_ref, hls_ref, buf_ref, a_ref, b_ref, h_ref, *, npt, tps, bs, pos0):
    te, w = x_ref.shape
    i = pl.program_id(0)

    def gates(xc):
        return _lru_gates(xc, wa_ref, ba_ref[...], wx_ref, bx_ref[...], lam_ref[...])

    @pl.when(i < npt)
    def _():
        first = i % tps == 0
        xc = _conv_rows(x_ref, prev_ref, buf_ref, cw_ref[...], first) + cb_ref[...]
        a, mult, ig = gates(xc)
        reset = first & (lax.broadcasted_iota(I32, (te, w), 0) == 0)
        a_ref[...] = jnp.where(reset, 0.0, a)
        b_ref[...] = jnp.where(reset, 1.0, mult) * ig * xc

        @pl.when(first)
        def _():
            h_ref[...] = jnp.zeros_like(h_ref)

        sub = lax.broadcasted_iota(I32, (SUBLANE, w), 0)

        def slab(s, h):
            rs = pl.ds(pl.multiple_of(s * SUBLANE, SUBLANE), SUBLANE)
            av, bv = a_ref[rs, :], b_ref[rs, :]
            for dlt in (1, 2, 4):
                ash, bsh = pltpu.roll(av, dlt, 0), pltpu.roll(bv, dlt, 0)
                m = sub >= dlt
                bv = jnp.where(m, av * bsh + bv, bv)
                av = jnp.where(m, av * ash, av)
            hs = av * h + bv
            b_ref[rs, :] = hs
            return hs[SUBLANE - 1:SUBLANE, :]

        h_last = lax.fori_loop(0, te // SUBLANE, slab, h_ref[...])
        h_ref[...] = h_last
        hlp_ref[...] = h_last
        o_ref[...] = (y_ref[...] * b_ref[...]).astype(o_ref.dtype)

    @pl.when(i >= npt)
    def _():
        ys = _conv_slabs(x_ref, hist_ref, cw_ref[...], bs)
        h = h0_ref[...]
        for t, y in enumerate(ys):
            xc = y + cb_ref[...]
            a, mult, ig = gates(xc)
            if pos0 + t == 0:
                a, mult = jnp.zeros_like(a), jnp.ones_like(mult)
            h = a * h + mult * ig * xc
            o_ref[t * bs:(t + 1) * bs, :] = (y_ref[t * bs:(t + 1) * bs, :] * h).astype(o_ref.dtype)
        hls_ref[...] = h


def _lru(lx, ly, hist_s, h0_s, p, layer, cfg):
    tl, bs, bp = cfg["ts"], cfg["bs"], cfg["bp"]
    t_rows, w = lx.shape
    nb = p["lru_wa"].shape[1]
    bw = w // nb
    npt, tps = cfg["tp"] // tl, cfg["seq"] // tl
    rpb = tl // SUBLANE
    vec = pl.BlockSpec((None, 1, w), lambda i: (layer, 0, 0))
    blk = pl.BlockSpec((None, nb, bw, bw), lambda i: (layer, 0, 0, 0))
    row = pl.BlockSpec((tl, w), lambda i: (i, 0))
    oc, hl_p, hl_s = pl.pallas_call(
        functools.partial(_lru_body, npt=npt, tps=tps, bs=bs, pos0=cfg["pos0_s"]), grid=(t_rows // tl,),
        in_specs=[row, pl.BlockSpec((SUBLANE, w), lambda i: (jnp.maximum(i * rpb - 1, 0), 0)), row,
                  pl.BlockSpec(((CONV_W - 1) * bs, w), lambda i: (0, 0)),
                  pl.BlockSpec((None, bs, w), lambda i: (layer, 0, 0)),
                  pl.BlockSpec((None, CONV_W, w), lambda i: (layer, 0, 0)), vec, blk, vec, blk, vec, vec],
        out_specs=[row, pl.BlockSpec((None, 1, w), lambda i: (jnp.minimum(i // tps, bp - 1), 0, 0)),
                   pl.BlockSpec((bs, w), lambda i: (0, 0))],
        out_shape=[jax.ShapeDtypeStruct((t_rows, w), BF16), jax.ShapeDtypeStruct((bp, 1, w), F32),
                   jax.ShapeDtypeStruct((bs, w), F32)],
        scratch_shapes=[pltpu.VMEM((tl + SUBLANE, w), F32), pltpu.VMEM((tl, w), F32),
                        pltpu.VMEM((tl, w), F32), pltpu.VMEM((1, w), F32)],
        compiler_params=_params(("arbitrary",)), name="lru")(
            lx, lx, ly, hist_s, h0_s, p["lru_conv_w"], p["lru_conv_b"], p["lru_wa"], p["lru_ba"],
            p["lru_wx"], p["lru_bx"], p["lru_lambda"])
    return oc, hl_p.reshape(bp, w), hl_s


def _time_major(a):
    return jnp.swapaxes(a, 0, 1).reshape(a.shape[0] * a.shape[1], a.shape[2])


def _batch_major(a, bs):
    return jnp.swapaxes(a.reshape(a.shape[0] // bs, bs, a.shape[1]), 0, 1)


def _moe_plan(route, n_exp, tm):
    t_rows = route.shape[0]
    flat_e = route[:, 2:2 + TOP_K].astype(I32).reshape(-1)
    oh = (flat_e[:, None] == jnp.arange(n_exp, dtype=I32)[None, :]).astype(I32)
    csum = jnp.cumsum(oh, axis=0)
    rank = jnp.take_along_axis(csum, flat_e[:, None], axis=1)[:, 0] - 1
    counts = csum[-1]
    tiles_e = (counts + tm - 1) // tm
    tile_end = jnp.cumsum(tiles_e)
    dest = ((tile_end - tiles_e) * tm)[flat_e] + rank
    n_tiles = pl.cdiv(TOP_K * t_rows, tm) + n_exp
    src = jnp.zeros((n_tiles * tm,), I32).at[dest].set(jnp.arange(TOP_K * t_rows, dtype=I32) // TOP_K)
    tile_e = jnp.sum((jnp.arange(n_tiles, dtype=I32)[:, None] >= tile_end[None, :]).astype(I32), axis=1)
    tile_e = jnp.minimum(tile_e, n_exp - 1)
    dest = dest.reshape(t_rows, TOP_K)
    return dest[:, 0], dest[:, 1], src, tile_e, tile_end[-1].astype(I32)


def kernel(x_prompt, x_sample, state_dn_S, state_dn_conv, state_lru_h, state_lru_conv, c_prompt, c_sample,
           w_mod, b_mod, norm_g, w_in, dn_conv_w, dn_a_log, dn_dt_bias, dn_norm_g, gm_ln_g, gm_ln_b, gm_ws,
           gm_bs, lru_conv_w, lru_conv_b, lru_wa, lru_ba, lru_wx, lru_bx, lru_lambda, w_branch, w_out,
           ffn_w_gate, ffn_w_up, ffn_w_down, router_w, router_b, moe_w_gate, moe_w_up, moe_w_down):
    bp, seq, d = x_prompt.shape
    bs, dec = x_sample.shape[0], x_sample.shape[1]
    depth = w_mod.shape[0]
    w = d // 2
    heads = w // DN_HEAD_DIM
    n_exp = moe_w_gate.shape[1]
    tp, ts = bp * seq, bs * dec
    t_rows = tp + ts
    past_len = 16384
    assert dec <= SUBLANE and dec >= CONV_W - 1 and ts % GM_CHUNK == 0 and seq % ts == 0
    assert tp % ts == 0 and ts % TE == 0 and bs % SUBLANE == 0 and t_rows % (16 * 2 * DENSE_ROW_TILES) == 0
    cfg = dict(bp=bp, bs=bs, seq=seq, tp=tp, ts=ts, pos0_s=past_len)
    tm_d = t_rows // DENSE_ROW_TILES

    def dense_meta(slab, ni=DENSE_ROW_TILES):
        return jnp.concatenate([jnp.full((ni,), slab, I32), jnp.array([ni], I32)])

    n_c = bp + bs
    n_c_pad = -(-n_c // SUBLANE) * SUBLANE
    c_all = jnp.concatenate([c_prompt, c_sample, jnp.zeros((n_c_pad - n_c, d), F32)], axis=0)
    b_mod3 = b_mod.reshape(depth, 1, N_MOD * d)
    mods = []
    for l in range(depth):
        meta = jnp.array([l, 1], I32)
        mods.append(_mm(c_all, w_mod, meta, name="mod", tm=n_c_pad, pre_act="silu", bias=b_mod3))
    mod = jnp.stack(mods).reshape(depth, n_c_pad, N_MOD, d)
    cfg["mod_p"] = mod[:, :bp].reshape(depth * bp, N_MOD, d)
    cfg["mod_s"] = jnp.swapaxes(mod[:, bp:n_c], 1, 2).reshape(depth * N_MOD, bs, d)
    cfg["norm_g"] = norm_g.reshape(depth * 4, 1, d)

    n_qkv, n_z = 3 * w, w
    c_ba = n_qkv + n_z
    c_rest = c_ba + 2 * heads
    assert c_rest % SUBLANE == 0
    w_in_t = jnp.swapaxes(w_in, 1, 2)
    lane_pad = lambda a, off: jnp.pad(a, ((0, 0), (off, LANE - off - a.shape[1]))).reshape(depth, 1, LANE)
    al_pad, dt_pad = lane_pad(dn_a_log, heads), lane_pad(dn_dt_bias, heads)
    dn_ng = dn_norm_g.reshape(depth, 1, DN_HEAD_DIM)
    vec3 = lambda a: a.reshape(depth, 1, a.shape[-1])
    lru_p = dict(lru_conv_w=lru_conv_w, lru_conv_b=vec3(lru_conv_b), lru_wa=lru_wa, lru_ba=vec3(lru_ba),
                 lru_wx=lru_wx, lru_bx=vec3(lru_bx), lru_lambda=vec3(lru_lambda))
    gm_g3, gm_b3 = vec3(gm_ln_g), vec3(gm_ln_b)
    gm_bst = jnp.pad(jnp.swapaxes(gm_bs, 1, 2), ((0, 0), (0, 0), (0, LANE - gm_bs.shape[1])))
    gw = w // gm_ws.shape[1]
    gm_cw = jnp.repeat(jnp.swapaxes(gm_ws[:, :, :dec, :dec], 1, 3).reshape(depth, dec, dec, -1), gw, axis=-1)
    gm_cw = jnp.swapaxes(gm_cw, 1, 2).reshape(depth, dec * dec, w)
    gm_cb = jnp.repeat(jnp.swapaxes(gm_bs[:, :, :dec], 1, 2), gw, axis=-1)
    gm_cb = jnp.pad(gm_cb, ((0, 0), (0, SUBLANE - dec), (0, 0)))
    wb3 = w_branch.reshape(depth * 3, w, d)
    n_moe = router_w.shape[0]
    rw_pad = jnp.pad(router_w, ((0, 0), (0, 0), (0, LANE - n_exp)))
    rb_pad = jnp.pad(router_b, ((0, 0), (0, LANE - n_exp)), constant_values=NEG_BIG).reshape(n_moe, 1, LANE)
    d_ff = ffn_w_gate.shape[2]
    moe_wg = moe_w_gate.reshape(n_moe * n_exp, d, d_ff)
    moe_wu = moe_w_up.reshape(n_moe * n_exp, d, d_ff)
    moe_wd = moe_w_down.reshape(n_moe * n_exp, d_ff, d)

    x = jnp.concatenate([x_prompt.reshape(tp, d), _time_major(x_sample)], axis=0)
    s_chain = jnp.zeros(state_dn_S.shape, F32)
    outs = {k: [] for k in ("S_p", "dc_p", "h_p", "lc_p", "dc_s", "h_s", "lc_s", "v_s")}

    def last_rows(a):
        return jnp.stack([a[(b + 1) * seq - (CONV_W - 1):(b + 1) * seq] for b in range(bp)])

    def to_blocks(a):
        a = jnp.pad(_batch_major(a, bs), ((0, 0), (0, SUBLANE - dec), (0, 0)))
        return a.reshape(bs * SUBLANE, a.shape[2])

    (hh,) = _post(x, None, cfg, res=None, nxt=(0, 0, 1, 0), name="pre0")
    for l in range(depth):
        meta = dense_meta(l)
        proj = functools.partial(_mm, hh, w_in_t, meta, w_nk=True, tm=tm_d, tn=TN_WIDE)
        proj16 = functools.partial(proj, out_dtype=BF16)
        qkv = proj(name="in_qkv", n=n_qkv)
        zs = proj16(name="in_z", n_off=n_qkv, n=n_z, act="silu")
        ba = proj(name="in_ba", n_off=c_ba, n=LANE, tn=LANE)
        uv = proj16(name="in_uv", n_off=c_rest, n=2 * w, act="gelu")
        lx = proj(name="in_lx", n_off=c_rest + 2 * w, n=w)
        ly = proj16(name="in_ly", n_off=c_rest + 3 * w, n=w, act="gelu")
        gates = proj16(name="in_merge", n_off=c_rest + 4 * w, n=3 * d, act="sigmoid")

        dn_hist = _time_major(state_dn_conv[l])
        qkv_s = _dn_prep_sample(qkv, dn_hist, dn_conv_w, l, cfg)
        oa_p, s_p = _dn_prompt(qkv, dn_conv_w, zs, ba, al_pad, dt_pad, dn_ng, l, cfg)
        oa_sb, s_chain = _dn_sample(to_blocks(qkv_s), to_blocks(zs[tp:]), to_blocks(ba[tp:]),
                                    state_dn_S, s_chain, al_pad, dt_pad, dn_ng, l, n_valid=dec)
        oa = jnp.concatenate([oa_p, _time_major(oa_sb.reshape(bs, SUBLANE, w)[:, :dec])], axis=0)
        dc_hist = jnp.concatenate([dn_hist, qkv[tp:]], axis=0)

        ob, vn_s = _gmlp(uv, gm_g3, gm_b3, gm_ws, gm_bst, gm_cw, gm_cb, l, cfg)
        lru_hist = _time_major(state_lru_conv[l])
        oc, hl_p, hl_s = _lru(lx, ly, lru_hist, state_lru_h, lru_p, l, cfg)
        lc_hist = jnp.concatenate([lru_hist, lx[tp:]], axis=0)

        outs["S_p"].append(s_p)
        outs["dc_p"].append(last_rows(qkv))
        outs["h_p"].append(hl_p)
        outs["lc_p"].append(last_rows(lx))
        outs["dc_s"].append(_batch_major(dc_hist[dec * bs:], bs))
        outs["h_s"].append(hl_s)
        outs["lc_s"].append(_batch_major(lc_hist[dec * bs:], bs))
        outs["v_s"].append(_batch_major(vn_s, bs))

        m = _branch_mm(oa, ob, oc, gates, wb3, meta, name="branch", tm=tm_d)
        y = _mm(m, w_out, meta, name="w_out", tm=tm_d, tn=TN_WIDE, out_dtype=BF16)
        j = l // 2
        if l % 2 == 0:
            x, hh = _post(x, y, cfg, res=(l, 1, 2), nxt=(l, 2, 4, 3), name="post_mix")
            hid = _ffn_up(hh, ffn_w_gate, ffn_w_up, dense_meta(j, DENSE_ROW_TILES // 2), name="ffn_up",
                          tm=2 * tm_d)
            y = _mm(hid, ffn_w_down, dense_meta(j, 2 * DENSE_ROW_TILES), name="ffn_down", tm=tm_d // 2,
                    out_dtype=BF16)
        else:
            x, hh32, route = _post(x, y, cfg, res=(l, 1, 2), nxt=(l, 2, 4, 3), router=(rw_pad, rb_pad, j),
                                   hh_dtype=F32, name="post_mix_r")
            d0, d1, src, tile_e, nused = _moe_plan(route, n_exp, TM)
            xs = _dispatch(hh32, src, nused.reshape(1), tm=TM, name="moe_dispatch")
            emeta = jnp.concatenate([j * n_exp + tile_e, nused.reshape(1)])
            hid = _ffn_up(xs, moe_wg, moe_wu, emeta, name="moe_up")
            ys = _mm(hid, moe_wd, emeta, name="moe_down")
            y = _combine(ys, d0, d1, route, name="moe_combine")
        if l + 1 < depth:
            x, hh = _post(x, y, cfg, res=(l, 3, 5), nxt=(l + 1, 0, 1, 0), name="post_ffn")
        else:
            (x,) = _post(x, y, cfg, res=(l, 3, 5), nxt=None, name="post_last")

    st = lambda k: jnp.stack(outs[k])
    return (x[:tp].reshape(bp, seq, d), _batch_major(x[tp:], bs),
            st("S_p"), st("dc_p"), st("h_p"), st("lc_p"),
            s_chain, st("dc_s"), st("h_s"), st("lc_s"), st("v_s"))
```

```python
import functools

import jax
import jax.numpy as jnp
from jax import lax
from jax.experimental import pallas as pl
from jax.experimental.pallas import tpu as pltpu

F32, BF16, I32 = jnp.float32, jnp.bfloat16, jnp.int32
HI = lax.Precision.HIGHEST

LANE = 128
SUBLANE = 8
VMEM_LIMIT = 56 << 20
DN_HEAD_DIM = 128
DN_CHUNK = 64
CONV_W = 4
GM_CHUNK = 128
LRU_BLOCKS = 8
LRU_C = 8.0
N_MOD = 6
TOP_K = 2
RMS_EPS = 1e-6
LN_EPS = 1e-5
L2_EPS = 1e-6
NEG_BIG = -1e30

TM = 256
DISPATCH_TILE = 512
TN = 512
TN_WIDE = 1024
DENSE_ROW_TILES = 8
TE = 512
DMA_ISSUE_UNROLL = 8


def _params(sem):
    return pltpu.CompilerParams(dimension_semantics=sem, vmem_limit_bytes=VMEM_LIMIT)


def _act(name, x):
    if name is None:
        return x
    if name == "silu":
        return x * jax.nn.sigmoid(x)
    if name == "gelu":
        return jax.nn.gelu(x)
    if name == "sigmoid":
        return jax.nn.sigmoid(x)
    raise ValueError(name)


def _rms(x, g):
    return x * lax.rsqrt(jnp.mean(x * x, axis=-1, keepdims=True) + RMS_EPS) * g


def _cast_once(meta_ref, i, pairs):
    prev = meta_ref[jnp.maximum(i - 1, 0)]

    @pl.when((i == 0) | (meta_ref[i] != prev))
    def _():
        for src, dst in pairs:
            dst[...] = src[...].astype(BF16)


def _mm_body(meta_ref, x_ref, w_ref, *rest, ni, act, pre_act, cast_w, has_bias, w_nk):
    rest = list(rest)
    b_ref = rest.pop(0) if has_bias else None
    o_ref = rest.pop(0)
    wbf_ref = rest.pop(0) if cast_w else None
    i = pl.program_id(1)
    nused = meta_ref[ni]

    @pl.when(i < nused)
    def _():
        if cast_w:
            _cast_once(meta_ref, i, [(w_ref, wbf_ref)])
            w = wbf_ref[...]
        else:
            w = w_ref[...]
        w = w[0] if w_nk else w
        x = x_ref[...]
        if pre_act is not None:
            x = _act(pre_act, x.astype(F32))
        dims = (((1,), (1,)), ((), ())) if w_nk else (((1,), (0,)), ((), ()))
        acc = lax.dot_general(x.astype(BF16), w, dims, preferred_element_type=F32)
        if has_bias:
            acc = acc + b_ref[...]
        o_ref[...] = _act(act, acc).astype(o_ref.dtype)

    @pl.when(i >= nused)
    def _():
        o_ref[...] = jnp.zeros_like(o_ref)


def _mm(x, w, meta, *, name, n_off=0, n=None, tm=TM, tn=TN, act=None, pre_act=None,
        bias=None, out_dtype=F32, w_nk=False):
    m_rows, k = x.shape
    n = w.shape[1 if w_nk else 2] if n is None else n
    tm = min(tm, m_rows)
    ni, nj = pl.cdiv(m_rows, tm), n // tn
    assert n % tn == 0 and meta.shape[0] == ni + 1
    cast_w = w.dtype != BF16
    body = functools.partial(_mm_body, ni=ni, act=act, pre_act=pre_act, cast_w=cast_w,
                             has_bias=bias is not None, w_nk=w_nk)
    if w_nk:
        assert n_off % SUBLANE == 0 and tn % SUBLANE == 0
        wspec = pl.BlockSpec((pl.Element(1), pl.Element(tn), pl.Element(k)),
                             lambda j, i, m: (m[i], (n_off // SUBLANE + j * (tn // SUBLANE)) * SUBLANE, 0))
    else:
        wspec = pl.BlockSpec((None, k, tn), lambda j, i, m: (m[i], 0, j + n_off))
    in_specs = [pl.BlockSpec((tm, k), lambda j, i, m: (jnp.minimum(i, m[ni] - 1), 0)), wspec]
    args = [x, w]
    if bias is not None:
        in_specs.append(pl.BlockSpec((None, 1, tn), lambda j, i, m: (m[i], 0, j + n_off)))
        args.append(bias)
    scratch = [pltpu.VMEM((1, tn, k) if w_nk else (k, tn), BF16)] if cast_w else []
    return pl.pallas_call(
        body,
        grid_spec=pltpu.PrefetchScalarGridSpec(
            num_scalar_prefetch=1, grid=(nj, ni), in_specs=in_specs,
            out_specs=pl.BlockSpec((tm, tn), lambda j, i, m: (i, j)), scratch_shapes=scratch),
        out_shape=jax.ShapeDtypeStruct((m_rows, n), out_dtype),
        compiler_params=_params(("arbitrary", "arbitrary")), name=name)(meta, *args)


def _ffn_up_body(meta_ref, x_ref, wg_ref, wu_ref, o_ref, wgb_ref, wub_ref, *, ni):
    i = pl.program_id(1)
    nused = meta_ref[ni]

    @pl.when(i < nused)
    def _():
        _cast_once(meta_ref, i, [(wg_ref, wgb_ref), (wu_ref, wub_ref)])
        x = x_ref[...]
        g = jnp.dot(x, wgb_ref[...], preferred_element_type=F32)
        u = jnp.dot(x, wub_ref[...], preferred_element_type=F32)
        o_ref[...] = (g * jax.nn.sigmoid(g) * u).astype(o_ref.dtype)

    @pl.when(i >= nused)
    def _():
        o_ref[...] = jnp.zeros_like(o_ref)


def _ffn_up(x, wg, wu, meta, *, name, tm=TM, tn=TN):
    m_rows, k = x.shape
    n = wg.shape[2]
    ni, nj = m_rows // tm, n // tn
    assert m_rows % tm == 0 and n % tn == 0 and meta.shape[0] == ni + 1
    wspec = pl.BlockSpec((None, k, tn), lambda j, i, m: (m[i], 0, j))
    return pl.pallas_call(
        functools.partial(_ffn_up_body, ni=ni),
        grid_spec=pltpu.PrefetchScalarGridSpec(
            num_scalar_prefetch=1, grid=(nj, ni),
            in_specs=[pl.BlockSpec((tm, k), lambda j, i, m: (jnp.minimum(i, m[ni] - 1), 0)),
                      wspec, wspec],
            out_specs=pl.BlockSpec((tm, tn), lambda j, i, m: (i, j)),
            scratch_shapes=[pltpu.VMEM((k, tn), BF16), pltpu.VMEM((k, tn), BF16)]),
        out_shape=jax.ShapeDtypeStruct((m_rows, n), BF16),
        compiler_params=_params(("arbitrary", "arbitrary")), name=name)(meta, x, wg, wu)


def _branch_body(meta_ref, xa_ref, xb_ref, xc_ref, ga_ref, gb_ref, gc_ref,
                 w0_ref, w1_ref, w2_ref, o_ref, s0_ref, s1_ref, s2_ref):
    i = pl.program_id(1)
    _cast_once(meta_ref, i, [(w0_ref, s0_ref), (w1_ref, s1_ref), (w2_ref, s2_ref)])
    acc = ga_ref[...] * jnp.dot(xa_ref[...], s0_ref[...], preferred_element_type=F32)
    acc = acc + gb_ref[...] * jnp.dot(xb_ref[...], s1_ref[...], preferred_element_type=F32)
    acc = acc + gc_ref[...] * jnp.dot(xc_ref[...], s2_ref[...], preferred_element_type=F32)
    o_ref[...] = acc.astype(o_ref.dtype)


def _branch_mm(oa, ob, oc, gates, wb, meta, *, name, tm=TM, tn=TN):
    m_rows, k = oa.shape
    n = wb.shape[2]
    ni, nj = m_rows // tm, n // tn
    xspec = pl.BlockSpec((tm, k), lambda j, i, m: (i, 0))

    def gspec(b):
        return pl.BlockSpec((tm, tn), lambda j, i, m: (i, j + b * nj))

    def wspec(b):
        return pl.BlockSpec((None, k, tn), lambda j, i, m: (m[i] * 3 + b, 0, j))

    return pl.pallas_call(
        _branch_body,
        grid_spec=pltpu.PrefetchScalarGridSpec(
            num_scalar_prefetch=1, grid=(nj, ni),
            in_specs=[xspec, xspec, xspec, gspec(0), gspec(1), gspec(2), wspec(0), wspec(1), wspec(2)],
            out_specs=pl.BlockSpec((tm, tn), lambda j, i, m: (i, j)),
            scratch_shapes=[pltpu.VMEM((k, tn), BF16)] * 3),
        out_shape=jax.ShapeDtypeStruct((m_rows, n), BF16),
        compiler_params=_params(("arbitrary", "arbitrary")), name=name)(
            meta, oa, ob, oc, gates, gates, gates, wb, wb, wb)


def _route(hh, rw_ref, rb_ref):
    lg = jnp.dot(hh, rw_ref[...], precision=HI, preferred_element_type=F32) + rb_ref[...]
    lane = lax.broadcasted_iota(I32, lg.shape, 1)
    m1 = jnp.max(lg, axis=1, keepdims=True)
    i1 = jnp.min(jnp.where(lg == m1, lane, LANE), axis=1, keepdims=True)
    lg2 = jnp.where(lane == i1, NEG_BIG * 2, lg)
    m2 = jnp.max(lg2, axis=1, keepdims=True)
    i2 = jnp.min(jnp.where(lg2 == m2, lane, LANE), axis=1, keepdims=True)
    e = jnp.exp(m2 - m1)
    p1 = 1.0 / (1.0 + e)
    p2 = e / (1.0 + e)
    return jnp.where(lane == 0, p1, jnp.where(lane == 1, p2, jnp.where(
        lane == 2, i1.astype(F32), jnp.where(lane == 3, i2.astype(F32), 0.0))))


def _post_body(*refs, npt, reps, res, nxt, router):
    it = iter(refs)
    x_ref = next(it)
    y_ref = next(it) if res else None
    gres_ref = next(it) if res else None
    gnx_ref = next(it) if nxt else None
    mpr_ref = next(it) if res else None
    mpn_ref = next(it) if nxt else None
    msg_ref = next(it) if res else None
    msc_ref = next(it) if nxt else None
    msh_ref = next(it) if nxt else None
    rw_ref = next(it) if router else None
    rb_ref = next(it) if router else None
    xo_ref = next(it) if res else None
    hh_ref = next(it) if nxt else None
    rt_ref = next(it) if router else None
    i = pl.program_id(0)

    def compute(gate, sc, sh):
        x = x_ref[...]
        if res:
            x = x + gate * _rms(y_ref[...].astype(F32), gres_ref[...])
            xo_ref[...] = x
        if nxt:
            hh = _rms(x, gnx_ref[...]) * (1.0 + sc) + sh
            hh_ref[...] = hh.astype(hh_ref.dtype)
            if router:
                rt_ref[...] = _route(hh, rw_ref, rb_ref)

    def rep(r):
        return jnp.concatenate([r[...]] * reps, axis=0) if reps > 1 else r[...]

    @pl.when(i < npt)
    def _():
        compute(mpr_ref[res[2]:res[2] + 1, :] if res else None,
                mpn_ref[nxt[2]:nxt[2] + 1, :] if nxt else None,
                mpn_ref[nxt[3]:nxt[3] + 1, :] if nxt else None)

    @pl.when(i >= npt)
    def _():
        compute(rep(msg_ref) if res else None, rep(msc_ref) if nxt else None,
                rep(msh_ref) if nxt else None)


def _post(x, y, cfg, *, res, nxt, router=None, hh_dtype=BF16, name):
    t_rows, d = x.shape
    te, bp, bs = TE, cfg["bp"], cfg["bs"]
    npt, nt, tpb = cfg["tp"] // te, t_rows // te, cfg["seq"] // te
    assert te % bs == 0 and cfg["seq"] % te == 0 and t_rows % te == 0
    row = pl.BlockSpec((te, d), lambda i: (i, 0))
    in_specs, args = [row], [x]
    if res:
        in_specs.append(row)
        args.append(y)

    def gspec(layer, r):
        return pl.BlockSpec((None, 1, d), lambda i: (layer * 4 + r, 0, 0))

    def mpspec(layer):
        return pl.BlockSpec((None, N_MOD, d), lambda i: (layer * bp + jnp.minimum(i // tpb, bp - 1), 0, 0))

    def msspec(layer, plane):
        return pl.BlockSpec((None, bs, d), lambda i: (layer * N_MOD + plane, 0, 0))

    if res:
        in_specs.append(gspec(res[0], res[1])); args.append(cfg["norm_g"])
    if nxt:
        in_specs.append(gspec(nxt[0], nxt[1])); args.append(cfg["norm_g"])
    if res:
        in_specs.append(mpspec(res[0])); args.append(cfg["mod_p"])
    if nxt:
        in_specs.append(mpspec(nxt[0])); args.append(cfg["mod_p"])
    if res:
        in_specs.append(msspec(res[0], res[2])); args.append(cfg["mod_s"])
    if nxt:
        in_specs.append(msspec(nxt[0], nxt[2])); args.append(cfg["mod_s"])
        in_specs.append(msspec(nxt[0], nxt[3])); args.append(cfg["mod_s"])
    if router:
        rw, rb, slab = router
        in_specs.append(pl.BlockSpec((None, d, LANE), lambda i: (slab, 0, 0))); args.append(rw)
        in_specs.append(pl.BlockSpec((None, 1, LANE), lambda i: (slab, 0, 0))); args.append(rb)
    out_specs, out_shape = [], []
    if res:
        out_specs.append(row); out_shape.append(jax.ShapeDtypeStruct((t_rows, d), F32))
    if nxt:
        out_specs.append(row); out_shape.append(jax.ShapeDtypeStruct((t_rows, d), hh_dtype))
    if router:
        out_specs.append(pl.BlockSpec((te, LANE), lambda i: (i, 0)))
        out_shape.append(jax.ShapeDtypeStruct((t_rows, LANE), F32))
    body = functools.partial(_post_body, npt=npt, reps=te // bs, res=res, nxt=nxt, router=bool(router))
    return pl.pallas_call(body, grid=(nt,), in_specs=in_specs, out_specs=out_specs, out_shape=out_shape,
                          compiler_params=_params(("arbitrary",)), name=name)(*args)


def _row_copy(src_hbm, row, dst_ref, r, sem):
    return pltpu.make_async_copy(src_hbm.at[pl.ds(row, 1), :], dst_ref.at[pl.ds(r, 1), :], sem)


def _gather_rows(idx_ref, base, src_hbm, dst_ref, sem, n):
    def start(g, c):
        for u in range(DMA_ISSUE_UNROLL):
            r = g * DMA_ISSUE_UNROLL + u
            _row_copy(src_hbm, idx_ref[base + r], dst_ref, r, sem).start(priority=u % 2)
        return c

    def wait(r, c):
        _row_copy(src_hbm, 0, dst_ref, r, sem).wait()
        return c

    lax.fori_loop(0, n // DMA_ISSUE_UNROLL, start, 0)
    lax.fori_loop(0, n, wait, 0, unroll=DMA_ISSUE_UNROLL)


def _dispatch_body(src_ref, nused_ref, x_hbm, o_ref, buf_ref, sem, *, tm):
    i = pl.program_id(0)

    @pl.when(i < nused_ref[0])
    def _():
        _gather_rows(src_ref, i * tm, x_hbm, buf_ref, sem, tm)
        o_ref[...] = buf_ref[...].astype(o_ref.dtype)

    @pl.when(i >= nused_ref[0])
    def _():
        o_ref[...] = jnp.zeros_like(o_ref)


def _dispatch(hh, src, nused, *, tm, name):
    cap = src.shape[0]
    d = hh.shape[1]
    return pl.pallas_call(
        functools.partial(_dispatch_body, tm=tm),
        grid_spec=pltpu.PrefetchScalarGridSpec(
            num_scalar_prefetch=2, grid=(cap // tm,),
            in_specs=[pl.BlockSpec(memory_space=pl.ANY)],
            out_specs=pl.BlockSpec((tm, d), lambda i, s, n: (i, 0)),
            scratch_shapes=[pltpu.VMEM((tm, d), hh.dtype), pltpu.SemaphoreType.DMA(())]),
        out_shape=jax.ShapeDtypeStruct((cap, d), BF16),
        compiler_params=_params(("arbitrary",)), name=name)(src, nused, hh)


def _combine_body(d0_ref, d1_ref, y_hbm, p_ref, o_ref, a_ref, b_ref, sem_a, sem_b, *, te):
    i = pl.program_id(0)

    def start(r, c):
        _row_copy(y_hbm, d0_ref[i * te + r], a_ref, r, sem_a).start(priority=0)
        _row_copy(y_hbm, d1_ref[i * te + r], b_ref, r, sem_b).start(priority=1)
        return c

    def wait(r, c):
        _row_copy(y_hbm, 0, a_ref, r, sem_a).wait()
        _row_copy(y_hbm, 0, b_ref, r, sem_b).wait()
        return c

    lax.fori_loop(0, te, start, 0, unroll=DMA_ISSUE_UNROLL)
    lax.fori_loop(0, te, wait, 0, unroll=DMA_ISSUE_UNROLL)
    p = p_ref[...]
    o_ref[...] = (p[:, 0:1] * a_ref[...] + p[:, 1:2] * b_ref[...]).astype(o_ref.dtype)


def _combine(ys, dest0, dest1, route, *, name):
    t_rows = route.shape[0]
    d = ys.shape[1]
    te = TE
    return pl.pallas_call(
        functools.partial(_combine_body, te=te),
        grid_spec=pltpu.PrefetchScalarGridSpec(
            num_scalar_prefetch=2, grid=(t_rows // te,),
            in_specs=[pl.BlockSpec(memory_space=pl.ANY),
                      pl.BlockSpec((te, LANE), lambda i, a, b: (i, 0))],
            out_specs=pl.BlockSpec((te, d), lambda i, a, b: (i, 0)),
            scratch_shapes=[pltpu.VMEM((te, d), F32), pltpu.VMEM((te, d), F32),
                            pltpu.SemaphoreType.DMA(()), pltpu.SemaphoreType.DMA(())]),
        out_shape=jax.ShapeDtypeStruct((t_rows, d), BF16),
        compiler_params=_params(("arbitrary",)), name=name)(dest0, dest1, ys, route)


def _dn_post(y, j):
    y = y * jax.nn.sigmoid(y)
    outs = []
    for h in range(y.shape[1] // DN_HEAD_DIM):
        yh = y[:, h * DN_HEAD_DIM:(h + 1) * DN_HEAD_DIM]
        n = lax.rsqrt(jnp.sum(yh * yh, axis=-1, keepdims=True) + L2_EPS)
        s = jnp.where(j == 0, n * (DN_HEAD_DIM ** -0.5), jnp.where(j == 1, n, 1.0))
        outs.append(yh * s)
    return jnp.concatenate(outs, axis=1)


def _conv_rows(x_ref, prev_ref, buf_ref, w, first):
    te = x_ref.shape[0]
    x = x_ref[...]
    buf_ref[0:SUBLANE, :] = jnp.where(first, 0.0, prev_ref[...])
    buf_ref[SUBLANE:, :] = x
    y = w[CONV_W - 1:CONV_W, :] * x
    for dlt in range(1, CONV_W):
        y = y + w[CONV_W - 1 - dlt:CONV_W - dlt, :] * buf_ref[SUBLANE - dlt:SUBLANE - dlt + te, :]
    return y


def _conv_slabs(x_ref, hist_ref, w, bs):
    xp = jnp.concatenate([hist_ref[...], x_ref[...]], axis=0)
    steps = x_ref.shape[0] // bs
    ys = []
    for t in range(steps):
        y = w[0:1, :] * xp[t * bs:(t + 1) * bs]
        for j in range(1, CONV_W):
            y = y + w[j:j + 1, :] * xp[(t + j) * bs:(t + j + 1) * bs]
        ys.append(y)
    return ys


def _dn_prep_s_body(x_ref, hist_ref, w_ref, o_ref, *, bs):
    ys = _conv_slabs(x_ref, hist_ref, w_ref[...], bs)
    o_ref[...] = _dn_post(jnp.concatenate(ys, axis=0), pl.program_id(0))


def _dn_prep_sample(qkv, hist_s, conv_w, layer, cfg):
    tp, ts, bs = cfg["tp"], cfg["ts"], cfg["bs"]
    w3 = qkv.shape[1]
    cb = w3 // 3
    return pl.pallas_call(
        functools.partial(_dn_prep_s_body, bs=bs), grid=(3,),
        in_specs=[pl.BlockSpec((ts, cb), lambda j: (tp // ts, j)),
                  pl.BlockSpec(((CONV_W - 1) * bs, cb), lambda j: (0, j)),
                  pl.BlockSpec((None, CONV_W, cb), lambda j: (layer, 0, j))],
        out_specs=pl.BlockSpec((ts, cb), lambda j: (0, j)),
        out_shape=jax.ShapeDtypeStruct((ts, w3), F32),
        compiler_params=_params(("arbitrary",)), name="dn_prep_s")(qkv, hist_s, conv_w)


def _dotf(a, b):
    return jnp.dot(a, b, precision=HI, preferred_element_type=F32)


def _dotb(a, b):
    return jnp.dot(a.astype(BF16), b.astype(BF16), preferred_element_type=F32)


def _dot_nt(a, b):
    return lax.dot_general(a, b, (((1,), (1,)), ((), ())), preferred_element_type=F32)


def _dot_tn(a, b):
    return lax.dot_general(a, b, (((0,), (0,)), ((), ())), preferred_element_type=F32)


def _dn_gates(ba, al, dt, heads, valid):
    lanel = lax.broadcasted_iota(I32, ba.shape, 1)
    beta = jnp.where(valid & (lanel < heads), jax.nn.sigmoid(ba), 0.0)
    g = jnp.where(valid, -jnp.exp(al) * jax.nn.softplus(ba + dt), 0.0)
    return beta, g


def _dn_wy_body(x_ref, prev_ref, cw_ref, ba_ref, al_ref, dt_ref,
                w_ref, u_ref, qg_ref, kd_ref, sc_ref, eg_ref, m_ref, r_ref, p_ref, buf_ref, *, steps_per_seq):
    c = DN_CHUNK
    rows_n = x_ref.shape[0]
    wq = x_ref.shape[1] // 3
    heads = wq // DN_HEAD_DIM
    p2 = 2 * c
    y = _conv_rows(x_ref, prev_ref, buf_ref, cw_ref[...], pl.program_id(0) % steps_per_seq == 0)
    for j in range(3):
        p_ref[j] = _dn_post(y[:, j * wq:(j + 1) * wq], j)
    q_ref, k_ref, v_ref = p_ref.at[0], p_ref.at[1], p_ref.at[2]
    beta_all, g_all = _dn_gates(ba_ref[...], al_ref[...], dt_ref[...], heads, True)
    rr = lax.broadcasted_iota(I32, (rows_n, rows_n), 0)
    cc = lax.broadcasted_iota(I32, (rows_n, rows_n), 1)
    gcum_all = _dotf(((rr >= cc) & (rr // c == cc // c)).astype(F32), g_all)
    row = lax.broadcasted_iota(I32, (p2, p2), 0)
    col = lax.broadcasted_iota(I32, (p2, p2), 1)
    same = (row // c) == (col // c)
    tri = (row >= col) & same
    strict = (row > col) & same

    eye = (row == col).astype(F32)
    blk = (row // SUBLANE) == (col // SUBLANE)

    def stack(ref, rs, h):
        return jnp.concatenate([ref[rs, (h + a) * DN_HEAD_DIM:(h + a + 1) * DN_HEAD_DIM] for a in (0, 1)], axis=0)

    def cols(arr, rs, lane0):
        return jnp.concatenate([arr[rs, lane0 + a:lane0 + a + 1] for a in (0, 1)], axis=0)

    probs = [(ci, h) for ci in range(rows_n // c) for h in range(0, heads, 2)]
    n_p = len(probs)

    for p, (ci, h) in enumerate(probs):
        rs = slice(ci * c, (ci + 1) * c)
        q, k = stack(q_ref, rs, h), stack(k_ref, rs, h)
        beta = cols(beta_all, rs, h)
        gcol = cols(gcum_all, rs, heads + h)
        glast = jnp.concatenate(
            [jnp.broadcast_to(gcum_all[(ci + 1) * c - 1:(ci + 1) * c, heads + h + a:heads + h + a + 1], (c, 1))
             for a in (0, 1)], axis=0)
        gmat = jnp.broadcast_to(gcol, (p2, p2))
        decay = jnp.exp(jnp.where(tri, gmat - gmat.T, NEG_BIG))
        kb = k * beta
        k16 = k.astype(BF16)
        lm = jnp.where(strict, _dot_nt(kb.astype(BF16), k16) * decay, 0.0)
        m_ref[0, p] = lm
        m_ref[1, p] = jnp.where(blk, lm, 0.0)
        sc = _dot_nt(q.astype(BF16), k16) * decay
        egc = jnp.exp(gcol)
        r_ref[0, p] = (kb * egc).astype(BF16)
        r_ref[1, p] = (stack(v_ref, rs, h) * beta).astype(BF16)
        qg = q * egc
        kd = k * jnp.exp(glast - gcol)
        egl = jnp.exp(glast)
        for a in (0, 1):
            hs = slice(a * c, (a + 1) * c)
            qg_ref[ci, h + a] = qg[hs].astype(qg_ref.dtype)
            kd_ref[ci, h + a] = kd[hs].astype(kd_ref.dtype)
            sc_ref[ci, h + a] = sc[hs, hs].astype(sc_ref.dtype)
            eg_ref[ci, h + a] = jnp.broadcast_to(egl[a * c:a * c + 1], (1, LANE))

    def stage(dst, fa, fb):
        for p in range(n_p):
            m_ref[dst, p] = _dotb(fa(p), fb(p))

    ld = lambda p: m_ref[1, p]
    stage(2, ld, ld)
    stage(3, lambda p: m_ref[2, p], lambda p: m_ref[2, p])
    stage(4, lambda p: eye - ld(p), lambda p: eye + m_ref[2, p])
    stage(5, lambda p: m_ref[4, p], lambda p: eye + m_ref[3, p])
    stage(2, lambda p: m_ref[5, p], lambda p: m_ref[0, p] - ld(p))
    stage(3, lambda p: m_ref[2, p], lambda p: m_ref[2, p])
    stage(4, lambda p: m_ref[3, p], lambda p: m_ref[3, p])
    stage(6, lambda p: eye - m_ref[2, p], lambda p: eye + m_ref[3, p])
    stage(7, lambda p: m_ref[6, p], lambda p: eye + m_ref[4, p])
    stage(2, lambda p: m_ref[7, p], lambda p: m_ref[5, p])
    for p, (ci, h) in enumerate(probs):
        t16 = m_ref[2, p].astype(BF16)
        wm = jnp.dot(t16, r_ref[0, p], preferred_element_type=F32)
        um = jnp.dot(t16, r_ref[1, p], preferred_element_type=F32)
        for a in (0, 1):
            hs = slice(a * c, (a + 1) * c)
            w_ref[ci, h + a] = wm[hs].astype(w_ref.dtype)
            u_ref[ci, h + a] = um[hs]


def _dn_seq_body(w_ref, u_ref, qg_ref, kd_ref, sc_ref, eg_ref, z_ref, ng_ref, o_ref, so_ref, s_ref, r_ref):
    nck, heads, c = w_ref.shape[0], w_ref.shape[1], w_ref.shape[2]
    ci = pl.program_id(1)

    @pl.when(ci == 0)
    def _():
        s_ref[...] = jnp.zeros_like(s_ref)

    def chunk(n, carry):
        rs = pl.ds(pl.multiple_of(n * c, c), c)
        for h in range(heads):
            r_ref[h] = jnp.dot(jnp.concatenate([w_ref[n, h], qg_ref[n, h]], axis=0),
                               s_ref[h].astype(BF16), preferred_element_type=F32)
        for h in range(heads):
            hs = slice(h * DN_HEAD_DIM, (h + 1) * DN_HEAD_DIM)
            vn16 = (u_ref[n, h] - r_ref[h, 0:c]).astype(BF16)
            o = r_ref[h, c:2 * c] + jnp.dot(sc_ref[n, h], vn16, preferred_element_type=F32)
            s_ref[h] = s_ref[h] * eg_ref[n, h] + _dot_tn(kd_ref[n, h], vn16)
            o_ref[rs, hs] = (_rms(o, ng_ref[...]) * z_ref[rs, hs]).astype(o_ref.dtype)
        return carry

    lax.fori_loop(0, nck, chunk, 0)

    @pl.when(ci == pl.num_programs(1) - 1)
    def _():
        so_ref[...] = s_ref[...]


DN_WY_CHUNKS = 4
DN_SEQ_CHUNKS = 8


def _dn_prompt(qkv, conv_w, z, ba, al, dt, ng, layer, cfg):
    tp, seq, bp = cfg["tp"], cfg["seq"], cfg["bp"]
    w3 = qkv.shape[1]
    w = w3 // 3
    heads = w // DN_HEAD_DIM
    c, d = DN_CHUNK, DN_HEAD_DIM
    nc = tp // c
    kw, ks = DN_WY_CHUNKS, DN_SEQ_CHUNKS
    assert nc % kw == 0 and (seq // c) % ks == 0 and heads % 2 == 0
    vec = pl.BlockSpec((None, 1, LANE), lambda *_: (layer, 0, 0))
    hm = lambda n, last: pl.BlockSpec((n, heads, c, last), lambda i: (i, 0, 0, 0))
    eg_spec = lambda n: pl.BlockSpec((n, heads, 1, LANE), lambda i: (i, 0, 0, 0))
    sds = jax.ShapeDtypeStruct
    rpb = kw * c // SUBLANE
    wm, um, qg, kd, sc, eg = pl.pallas_call(
        functools.partial(_dn_wy_body, steps_per_seq=seq // (kw * c)), grid=(nc // kw,),
        in_specs=[pl.BlockSpec((kw * c, w3), lambda i: (i, 0)),
                  pl.BlockSpec((SUBLANE, w3), lambda i: (jnp.maximum(i * rpb - 1, 0), 0)),
                  pl.BlockSpec((None, CONV_W, w3), lambda i: (layer, 0, 0)),
                  pl.BlockSpec((kw * c, LANE), lambda i: (i, 0)), vec, vec],
        out_specs=[hm(kw, d), hm(kw, d), hm(kw, d), hm(kw, d), hm(kw, c), eg_spec(kw)],
        out_shape=[sds((nc, heads, c, d), BF16), sds((nc, heads, c, d), F32), sds((nc, heads, c, d), BF16),
                   sds((nc, heads, c, d), BF16), sds((nc, heads, c, c), BF16), sds((nc, heads, 1, LANE), F32)],
        scratch_shapes=[pltpu.VMEM((8, kw * heads // 2, 2 * c, 2 * c), F32),
                        pltpu.VMEM((2, kw * heads // 2, 2 * c, d), BF16),
                        pltpu.VMEM((3, kw * c, w), F32), pltpu.VMEM((kw * c + SUBLANE, w3), F32)],
        compiler_params=_params(("arbitrary",)), name="dn_wy")(qkv, qkv, conv_w, ba, al, dt)
    nps = seq // c // ks
    hm2 = lambda last: pl.BlockSpec((ks, heads, c, last), lambda b, n: (b * nps + n, 0, 0, 0))
    return pl.pallas_call(
        _dn_seq_body, grid=(bp, nps),
        in_specs=[hm2(d), hm2(d), hm2(d), hm2(d), hm2(c),
                  pl.BlockSpec((ks, heads, 1, LANE), lambda b, n: (b * nps + n, 0, 0, 0)),
                  pl.BlockSpec((ks * c, w), lambda b, n: (b * nps + n, 0)),
                  pl.BlockSpec((None, 1, LANE), lambda b, n: (layer, 0, 0))],
        out_specs=[pl.BlockSpec((ks * c, w), lambda b, n: (b * nps + n, 0)),
                   pl.BlockSpec((None, heads, d, d), lambda b, n: (b, 0, 0, 0))],
        out_shape=[sds((tp, w), BF16), sds((bp, heads, d, d), F32)],
        scratch_shapes=[pltpu.VMEM((heads, d, d), F32), pltpu.VMEM((heads, 2 * c, d), F32)],
        compiler_params=_params(("arbitrary", "arbitrary")), name="dn_seq")(wm, um, qg, kd, sc, eg, z, ng)


def _dn_chunk_body(q_ref, k_ref, v_ref, z_ref, ba_ref, s0_ref, al_ref, dt_ref, ng_ref, chain_ref,
                   o_ref, so_ref, m_ref, sc_ref, ws_ref, kd_ref, vn_ref, *, groups, n_valid):
    c = q_ref.shape[0]
    heads = q_ref.shape[1] // DN_HEAD_DIM
    rpg = c // groups
    assert rpg == SUBLANE

    row = lax.broadcasted_iota(I32, (c, c), 0)
    col = lax.broadcasted_iota(I32, (c, c), 1)
    same = (row // rpg) == (col // rpg)
    tri = (row >= col) & same
    strict = (row > col) & same
    rowl = lax.broadcasted_iota(I32, (c, LANE), 0)
    beta_all, g_all = _dn_gates(ba_ref[...], al_ref[...], dt_ref[...], heads, (rowl % rpg) < n_valid)
    gcum_all = _dotf(tri.astype(F32), g_all)
    rg = lax.broadcasted_iota(I32, (c, 1), 0) // rpg
    rg2 = jnp.concatenate([rg, rg], axis=0)

    eye = (row == col).astype(F32)
    glast = lambda h, gi: gcum_all[(gi + 1) * rpg - 1:(gi + 1) * rpg, heads + h:heads + h + 1]

    for h in range(heads):
        hs = slice(h * DN_HEAD_DIM, (h + 1) * DN_HEAD_DIM)
        q, k = q_ref[:, hs], k_ref[:, hs]
        beta = beta_all[:, h:h + 1]
        gcol = gcum_all[:, heads + h:heads + h + 1]
        gmat = jnp.broadcast_to(
            jnp.concatenate([gcol, jnp.zeros((LANE - c, 1), F32)], axis=0), (LANE, LANE))
        decay = jnp.exp(jnp.where(tri, gmat[0:c, 0:c] - gmat.T[0:c, 0:c], NEG_BIG))
        kb = k * beta
        k16 = k.astype(BF16)
        m_ref[0, h] = jnp.where(strict, _dot_nt(kb.astype(BF16), k16) * decay, 0.0)
        sc_ref[h] = (_dot_nt(q.astype(BF16), k16) * decay).astype(BF16)
        egc = jnp.exp(gcol)
        lhs = jnp.concatenate([kb * egc, q * egc], axis=0).astype(BF16)
        ws = jnp.zeros((2 * c, DN_HEAD_DIM), F32)
        glast_rows = jnp.zeros((c, 1), F32)
        for gi in range(groups):
            r = jnp.dot(lhs, s0_ref[gi, h].astype(BF16), preferred_element_type=F32)
            ws = jnp.where(rg2 == gi, r, ws)
            glast_rows = jnp.where(rg == gi, glast(h, gi), glast_rows)
        ws_ref[h] = ws
        kd_ref[h] = (k * jnp.exp(glast_rows - gcol)).astype(BF16)

    def stage(dst, fa, fb):
        for h in range(heads):
            m_ref[dst, h] = _dotb(fa(h), fb(h))

    stage(1, lambda h: m_ref[0, h], lambda h: m_ref[0, h])
    stage(2, lambda h: m_ref[1, h], lambda h: m_ref[1, h])
    stage(3, lambda h: eye - m_ref[0, h], lambda h: eye + m_ref[1, h])
    stage(1, lambda h: m_ref[3, h], lambda h: eye + m_ref[2, h])
    for h in range(heads):
        hs = slice(h * DN_HEAD_DIM, (h + 1) * DN_HEAD_DIM)
        vn_ref[h] = _dotb(m_ref[1, h], v_ref[:, hs] * beta_all[:, h:h + 1] - ws_ref[h, 0:c]).astype(BF16)
    for h in range(heads):
        hs = slice(h * DN_HEAD_DIM, (h + 1) * DN_HEAD_DIM)
        vn16 = vn_ref[h]
        o = ws_ref[h, c:2 * c] + jnp.dot(sc_ref[h], vn16, preferred_element_type=F32)
        kdec = kd_ref[h]
        for gi in range(groups):
            upd = _dot_tn(jnp.where(rg == gi, kdec, jnp.zeros_like(kdec)), vn16)
            so_ref[gi, h] = s0_ref[gi, h] * jnp.exp(glast(h, gi)) + upd
        o_ref[:, hs] = (_rms(o, ng_ref[...]) * z_ref[:, hs]).astype(o_ref.dtype)


def _dn_sample(qkv, z, ba, state, chain, al, dt, ng, layer, *, n_valid):
    rows, w3 = qkv.shape
    w = w3 // 3
    heads = w // DN_HEAD_DIM
    c = DN_CHUNK
    groups = c // SUBLANE
    rspec = lambda col: pl.BlockSpec((c, w), lambda b: (b, col))
    vec = pl.BlockSpec((None, 1, LANE), lambda b: (layer, 0, 0))
    sspec = pl.BlockSpec((None, groups, heads, DN_HEAD_DIM, DN_HEAD_DIM), lambda b: (layer, b, 0, 0, 0))
    return pl.pallas_call(
        functools.partial(_dn_chunk_body, groups=groups, n_valid=n_valid),
        grid=(rows // c,),
        in_specs=[rspec(0), rspec(1), rspec(2), rspec(0), pl.BlockSpec((c, LANE), lambda b: (b, 0)),
                  sspec, vec, vec, vec, pl.BlockSpec(memory_space=pl.ANY)],
        out_specs=[rspec(0), sspec],
        out_shape=[jax.ShapeDtypeStruct((rows, w), BF16), jax.ShapeDtypeStruct(chain.shape, F32)],
        scratch_shapes=[pltpu.VMEM((4, heads, c, c), F32), pltpu.VMEM((heads, c, c), BF16),
                        pltpu.VMEM((heads, 2 * c, DN_HEAD_DIM), F32), pltpu.VMEM((heads, c, DN_HEAD_DIM), BF16),
                        pltpu.VMEM((heads, c, DN_HEAD_DIM), BF16)],
        input_output_aliases={9: 1},
        compiler_params=_params(("arbitrary",)), name="dn_sample")(
            qkv, qkv, qkv, z, ba, state, al, dt, ng, chain)


def _layer_norm(v, g, b):
    mu = jnp.mean(v, axis=-1, keepdims=True)
    vc = v - mu
    return vc * lax.rsqrt(jnp.mean(vc * vc, axis=-1, keepdims=True) + LN_EPS) * g + b


def _gmlp_body(uv_ref, g_ref, b_ref, ws_ref, bst_ref, cw_ref, cb_ref, o_ref, vn_ref, *, npt, bs):
    w = uv_ref.shape[1] // 2
    i = pl.program_id(0)
    vn = _layer_norm(uv_ref[:, w:].astype(F32), g_ref[...], b_ref[...])

    @pl.when(i < npt)
    def _():
        groups = ws_ref.shape[0]
        gw = w // groups
        v16 = vn.astype(BF16)
        row = lax.broadcasted_iota(I32, (GM_CHUNK, GM_CHUNK), 0)
        col = lax.broadcasted_iota(I32, (GM_CHUNK, GM_CHUNK), 1)
        wcs = [jnp.where(row >= col, ws_ref[g], 0.0).astype(BF16) for g in range(groups)]
        bst = bst_ref[...]
        for c in range(uv_ref.shape[0] // GM_CHUNK):
            rs = slice(c * GM_CHUNK, (c + 1) * GM_CHUNK)
            for g in range(groups):
                cs = slice(g * gw, (g + 1) * gw)
                s = jnp.dot(wcs[g], v16[rs, cs], preferred_element_type=F32) + bst[:, g:g + 1]
                o_ref[rs, cs] = (uv_ref[rs, cs] * s).astype(o_ref.dtype)

    @pl.when(i >= npt)
    def _():
        steps = uv_ref.shape[0] // bs
        vn_ref[...] = vn
        for t in range(steps):
            s = cb_ref[t:t + 1, :]
            for u in range(t + 1):
                s = s + cw_ref[t * steps + u:t * steps + u + 1, :] * vn[u * bs:(u + 1) * bs]
            o_ref[t * bs:(t + 1) * bs, :] = (uv_ref[t * bs:(t + 1) * bs, 0:w] * s).astype(o_ref.dtype)


def _gmlp(uv, ln_g, ln_b, ws, bst, cw, cb, layer, cfg):
    tl, bs = cfg["ts"], cfg["bs"]
    t_rows, w2 = uv.shape
    w = w2 // 2
    groups = ws.shape[1]
    steps = tl // bs
    vec = pl.BlockSpec((None, 1, w), lambda i: (layer, 0, 0))
    return pl.pallas_call(
        functools.partial(_gmlp_body, npt=cfg["tp"] // tl, bs=bs), grid=(t_rows // tl,),
        in_specs=[pl.BlockSpec((tl, w2), lambda i: (i, 0)), vec, vec,
                  pl.BlockSpec((None, groups, GM_CHUNK, GM_CHUNK), lambda i: (layer, 0, 0, 0)),
                  pl.BlockSpec((None, GM_CHUNK, LANE), lambda i: (layer, 0, 0)),
                  pl.BlockSpec((None, steps * steps, w), lambda i: (layer, 0, 0)),
                  pl.BlockSpec((None, SUBLANE, w), lambda i: (layer, 0, 0))],
        out_specs=[pl.BlockSpec((tl, w), lambda i: (i, 0)), pl.BlockSpec((tl, w), lambda i: (0, 0))],
        out_shape=[jax.ShapeDtypeStruct((t_rows, w), BF16), jax.ShapeDtypeStruct((tl, w), F32)],
        compiler_params=_params(("arbitrary",)), name="gmlp")(uv, ln_g, ln_b, ws, bst, cw, cb)


def _lru_gates(xc, wa_ref, ba, wx_ref, bx, lam):
    nb = wa_ref.shape[0]
    bw = xc.shape[1] // nb
    x16 = xc.astype(BF16)
    rs, gs = [], []
    for h in range(nb):
        xs = x16[:, h * bw:(h + 1) * bw]
        rs.append(jnp.dot(xs, wa_ref[h].astype(BF16), preferred_element_type=F32))
        gs.append(jnp.dot(xs, wx_ref[h].astype(BF16), preferred_element_type=F32))
    r = jax.nn.sigmoid(jnp.concatenate(rs, axis=1) + ba)
    ig = jax.nn.sigmoid(jnp.concatenate(gs, axis=1) + bx)
    log_a = -LRU_C * r * jax.nn.softplus(-lam)
    th = jnp.tanh(log_a)
    return jnp.exp(log_a), jnp.sqrt(-2.0 * th / (1.0 - th)), ig


def _lru_body(x_ref, prev_ref, y_ref, hist_ref, h0_ref, cw_ref, cb_ref, wa_ref, ba_ref, wx_ref, bx_ref,
              lam_ref, o_ref, hlp_ref, hls_ref, buf_ref, a_ref, b_ref, h_ref, *, npt, tps, bs, pos0):
    te, w = x_ref.shape
    i = pl.program_id(0)

    def gates(xc):
        return _lru_gates(xc, wa_ref, ba_ref[...], wx_ref, bx_ref[...], lam_ref[...])

    @pl.when(i < npt)
    def _():
        first = i % tps == 0
        xc = _conv_rows(x_ref, prev_ref, buf_ref, cw_ref[...], first) + cb_ref[...]
        a, mult, ig = gates(xc)
        reset = first & (lax.broadcasted_iota(I32, (te, w), 0) == 0)
        a_ref[...] = jnp.where(reset, 0.0, a)
        b_ref[...] = jnp.where(reset, 1.0, mult) * ig * xc

        @pl.when(first)
        def _():
            h_ref[...] = jnp.zeros_like(h_ref)

        sub = lax.broadcasted_iota(I32, (SUBLANE, w), 0)

        def slab(s, h):
            rs = pl.ds(pl.multiple_of(s * SUBLANE, SUBLANE), SUBLANE)
            av, bv = a_ref[rs, :], b_ref[rs, :]
            for dlt in (1, 2, 4):
                ash, bsh = pltpu.roll(av, dlt, 0), pltpu.roll(bv, dlt, 0)
                m = sub >= dlt
                bv = jnp.where(m, av * bsh + bv, bv)
                av = jnp.where(m, av * ash, av)
            hs = av * h + bv
            b_ref[rs, :] = hs
            return hs[SUBLANE - 1:SUBLANE, :]

        h_last = lax.fori_loop(0, te // SUBLANE, slab, h_ref[...])
        h_ref[...] = h_last
        hlp_ref[...] = h_last
        o_ref[...] = (y_ref[...] * b_ref[...]).astype(o_ref.dtype)

    @pl.when(i >= npt)
    def _():
        ys = _conv_slabs(x_ref, hist_ref, cw_ref[...], bs)
        h = h0_ref[...]
        for t, y in enumerate(ys):
            xc = y + cb_ref[...]
            a, mult, ig = gates(xc)
            if pos0 + t == 0:
                a, mult = jnp.zeros_like(a), jnp.ones_like(mult)
            h = a * h + mult * ig * xc
            o_ref[t * bs:(t + 1) * bs, :] = (y_ref[t * bs:(t + 1) * bs, :] * h).astype(o_ref.dtype)
        hls_ref[...] = h


def _lru(lx, ly, hist_s, h0_s, p, layer, cfg):
    tl, bs, bp = cfg["ts"], cfg["bs"], cfg["bp"]
    t_rows, w = lx.shape
    nb = p["lru_wa"].shape[1]
    bw = w // nb
    npt, tps = cfg["tp"] // tl, cfg["seq"] // tl
    rpb = tl // SUBLANE
    vec = pl.BlockSpec((None, 1, w), lambda i: (layer, 0, 0))
    blk = pl.BlockSpec((None, nb, bw, bw), lambda i: (layer, 0, 0, 0))
    row = pl.BlockSpec((tl, w), lambda i: (i, 0))
    oc, hl_p, hl_s = pl.pallas_call(
        functools.partial(_lru_body, npt=npt, tps=tps, bs=bs, pos0=cfg["pos0_s"]), grid=(t_rows // tl,),
        in_specs=[row, pl.BlockSpec((SUBLANE, w), lambda i: (jnp.maximum(i * rpb - 1, 0), 0)), row,
                  pl.BlockSpec(((CONV_W - 1) * bs, w), lambda i: (0, 0)),
                  pl.BlockSpec((None, bs, w), lambda i: (layer, 0, 0)),
                  pl.BlockSpec((None, CONV_W, w), lambda i: (layer, 0, 0)), vec, blk, vec, blk, vec, vec],
        out_specs=[row, pl.BlockSpec((None, 1, w), lambda i: (jnp.minimum(i // tps, bp - 1), 0, 0)),
                   pl.BlockSpec((bs, w), lambda i: (0, 0))],
        out_shape=[jax.ShapeDtypeStruct((t_rows, w), BF16), jax.ShapeDtypeStruct((bp, 1, w), F32),
                   jax.ShapeDtypeStruct((bs, w), F32)],
        scratch_shapes=[pltpu.VMEM((tl + SUBLANE, w), F32), pltpu.VMEM((tl, w), F32),
                        pltpu.VMEM((tl, w), F32), pltpu.VMEM((1, w), F32)],
        compiler_params=_params(("arbitrary",)), name="lru")(
            lx, lx, ly, hist_s, h0_s, p["lru_conv_w"], p["lru_conv_b"], p["lru_wa"], p["lru_ba"],
            p["lru_wx"], p["lru_bx"], p["lru_lambda"])
    return oc, hl_p.reshape(bp, w), hl_s


def _time_major(a):
    return jnp.swapaxes(a, 0, 1).reshape(a.shape[0] * a.shape[1], a.shape[2])


def _batch_major(a, bs):
    return jnp.swapaxes(a.reshape(a.shape[0] // bs, bs, a.shape[1]), 0, 1)


def _moe_plan(route, n_exp, tm):
    t_rows = route.shape[0]
    flat_e = route[:, 2:2 + TOP_K].astype(I32).reshape(-1)
    oh = (flat_e[:, None] == jnp.arange(n_exp, dtype=I32)[None, :]).astype(I32)
    csum = jnp.cumsum(oh, axis=0)
    rank = jnp.take_along_axis(csum, flat_e[:, None], axis=1)[:, 0] - 1
    counts = csum[-1]
    tiles_e = (counts + tm - 1) // tm
    tile_end = jnp.cumsum(tiles_e)
    dest = ((tile_end - tiles_e) * tm)[flat_e] + rank
    n_tiles = pl.cdiv(TOP_K * t_rows, tm) + n_exp
    src = jnp.zeros((n_tiles * tm,), I32).at[dest].set(jnp.arange(TOP_K * t_rows, dtype=I32) // TOP_K)
    tile_e = jnp.sum((jnp.arange(n_tiles, dtype=I32)[:, None] >= tile_end[None, :]).astype(I32), axis=1)
    tile_e = jnp.minimum(tile_e, n_exp - 1)
    dest = dest.reshape(t_rows, TOP_K)
    return dest[:, 0], dest[:, 1], src, tile_e, tile_end[-1].astype(I32)


def kernel(x_prompt, x_sample, state_dn_S, state_dn_conv, state_lru_h, state_lru_conv, c_prompt, c_sample,
           w_mod, b_mod, norm_g, w_in, dn_conv_w, dn_a_log, dn_dt_bias, dn_norm_g, gm_ln_g, gm_ln_b, gm_ws,
           gm_bs, lru_conv_w, lru_conv_b, lru_wa, lru_ba, lru_wx, lru_bx, lru_lambda, w_branch, w_out,
           ffn_w_gate, ffn_w_up, ffn_w_down, router_w, router_b, moe_w_gate, moe_w_up, moe_w_down):
    bp, seq, d = x_prompt.shape
    bs, dec = x_sample.shape[0], x_sample.shape[1]
    depth = w_mod.shape[0]
    w = d // 2
    heads = w // DN_HEAD_DIM
    n_exp = moe_w_gate.shape[1]
    tp, ts = bp * seq, bs * dec
    t_rows = tp + ts
    past_len = 16384
    assert dec <= SUBLANE and dec >= CONV_W - 1 and ts % GM_CHUNK == 0 and seq % ts == 0
    assert tp % ts == 0 and ts % TE == 0 and bs % SUBLANE == 0 and t_rows % (16 * 2 * DENSE_ROW_TILES) == 0
    cfg = dict(bp=bp, bs=bs, seq=seq, tp=tp, ts=ts, pos0_s=past_len)
    tm_d = t_rows // DENSE_ROW_TILES

    def dense_meta(slab, ni=DENSE_ROW_TILES):
        return jnp.concatenate([jnp.full((ni,), slab, I32), jnp.array([ni], I32)])

    n_c = bp + bs
    n_c_pad = -(-n_c // SUBLANE) * SUBLANE
    c_all = jnp.concatenate([c_prompt, c_sample, jnp.zeros((n_c_pad - n_c, d), F32)], axis=0)
    b_mod3 = b_mod.reshape(depth, 1, N_MOD * d)
    mods = []
    for l in range(depth):
        meta = jnp.array([l, 1], I32)
        mods.append(_mm(c_all, w_mod, meta, name="mod", tm=n_c_pad, pre_act="silu", bias=b_mod3))
    mod = jnp.stack(mods).reshape(depth, n_c_pad, N_MOD, d)
    cfg["mod_p"] = mod[:, :bp].reshape(depth * bp, N_MOD, d)
    cfg["mod_s"] = jnp.swapaxes(mod[:, bp:n_c], 1, 2).reshape(depth * N_MOD, bs, d)
    cfg["norm_g"] = norm_g.reshape(depth * 4, 1, d)

    n_qkv, n_z = 3 * w, w
    c_ba = n_qkv + n_z
    c_rest = c_ba + 2 * heads
    assert c_rest % SUBLANE == 0
    w_in_t = jnp.swapaxes(w_in, 1, 2)
    lane_pad = lambda a, off: jnp.pad(a, ((0, 0), (off, LANE - off - a.shape[1]))).reshape(depth, 1, LANE)
    al_pad, dt_pad = lane_pad(dn_a_log, heads), lane_pad(dn_dt_bias, heads)
    dn_ng = dn_norm_g.reshape(depth, 1, DN_HEAD_DIM)
    vec3 = lambda a: a.reshape(depth, 1, a.shape[-1])
    lru_p = dict(lru_conv_w=lru_conv_w, lru_conv_b=vec3(lru_conv_b), lru_wa=lru_wa, lru_ba=vec3(lru_ba),
                 lru_wx=lru_wx, lru_bx=vec3(lru_bx), lru_lambda=vec3(lru_lambda))
    gm_g3, gm_b3 = vec3(gm_ln_g), vec3(gm_ln_b)
    gm_bst = jnp.pad(jnp.swapaxes(gm_bs, 1, 2), ((0, 0), (0, 0), (0, LANE - gm_bs.shape[1])))
    gw = w // gm_ws.shape[1]
    gm_cw = jnp.repeat(jnp.swapaxes(gm_ws[:, :, :dec, :dec], 1, 3).reshape(depth, dec, dec, -1), gw, axis=-1)
    gm_cw = jnp.swapaxes(gm_cw, 1, 2).reshape(depth, dec * dec, w)
    gm_cb = jnp.repeat(jnp.swapaxes(gm_bs[:, :, :dec], 1, 2), gw, axis=-1)
    gm_cb = jnp.pad(gm_cb, ((0, 0), (0, SUBLANE - dec), (0, 0)))
    wb3 = w_branch.reshape(depth * 3, w, d)
    n_moe = router_w.shape[0]
    rw_pad = jnp.pad(router_w, ((0, 0), (0, 0), (0, LANE - n_exp)))
    rb_pad = jnp.pad(router_b, ((0, 0), (0, LANE - n_exp)), constant_values=NEG_BIG).reshape(n_moe, 1, LANE)
    d_ff = ffn_w_gate.shape[2]
    moe_wg = moe_w_gate.reshape(n_moe * n_exp, d, d_ff)
    moe_wu = moe_w_up.reshape(n_moe * n_exp, d, d_ff)
    moe_wd = moe_w_down.reshape(n_moe * n_exp, d_ff, d)

    x = jnp.concatenate([x_prompt.reshape(tp, d), _time_major(x_sample)], axis=0)
    s_chain = jnp.zeros(state_dn_S.shape, F32)
    outs = {k: [] for k in ("S_p", "dc_p", "h_p", "lc_p", "dc_s", "h_s", "lc_s", "v_s")}

    def last_rows(a):
        return jnp.stack([a[(b + 1) * seq - (CONV_W - 1):(b + 1) * seq] for b in range(bp)])

    def to_blocks(a):
        a = jnp.pad(_batch_major(a, bs), ((0, 0), (0, SUBLANE - dec), (0, 0)))
        return a.reshape(bs * SUBLANE, a.shape[2])

    (hh,) = _post(x, None, cfg, res=None, nxt=(0, 0, 1, 0), name="pre0")
    for l in range(depth):
        meta = dense_meta(l)
        proj = functools.partial(_mm, hh, w_in_t, meta, w_nk=True, tm=tm_d, tn=TN_WIDE)
        proj16 = functools.partial(proj, out_dtype=BF16)
        qkv = proj(name="in_qkv", n=n_qkv)
        zs = proj16(name="in_z", n_off=n_qkv, n=n_z, act="silu")
        ba = proj(name="in_ba", n_off=c_ba, n=LANE, tn=LANE)
        uv = proj16(name="in_uv", n_off=c_rest, n=2 * w, act="gelu")
        lx = proj(name="in_lx", n_off=c_rest + 2 * w, n=w)
        ly = proj16(name="in_ly", n_off=c_rest + 3 * w, n=w, act="gelu")
        gates = proj16(name="in_merge", n_off=c_rest + 4 * w, n=3 * d, act="sigmoid")

        dn_hist = _time_major(state_dn_conv[l])
        qkv_s = _dn_prep_sample(qkv, dn_hist, dn_conv_w, l, cfg)
        oa_p, s_p = _dn_prompt(qkv, dn_conv_w, zs, ba, al_pad, dt_pad, dn_ng, l, cfg)
        oa_sb, s_chain = _dn_sample(to_blocks(qkv_s), to_blocks(zs[tp:]), to_blocks(ba[tp:]),
                                    state_dn_S, s_chain, al_pad, dt_pad, dn_ng, l, n_valid=dec)
        oa = jnp.concatenate([oa_p, _time_major(oa_sb.reshape(bs, SUBLANE, w)[:, :dec])], axis=0)
        dc_hist = jnp.concatenate([dn_hist, qkv[tp:]], axis=0)

        ob, vn_s = _gmlp(uv, gm_g3, gm_b3, gm_ws, gm_bst, gm_cw, gm_cb, l, cfg)
        lru_hist = _time_major(state_lru_conv[l])
        oc, hl_p, hl_s = _lru(lx, ly, lru_hist, state_lru_h, lru_p, l, cfg)
        lc_hist = jnp.concatenate([lru_hist, lx[tp:]], axis=0)

        outs["S_p"].append(s_p)
        outs["dc_p"].append(last_rows(qkv))
        outs["h_p"].append(hl_p)
        outs["lc_p"].append(last_rows(lx))
        outs["dc_s"].append(_batch_major(dc_hist[dec * bs:], bs))
        outs["h_s"].append(hl_s)
        outs["lc_s"].append(_batch_major(lc_hist[dec * bs:], bs))
        outs["v_s"].append(_batch_major(vn_s, bs))

        m = _branch_mm(oa, ob, oc, gates, wb3, meta, name="branch", tm=tm_d)
        y = _mm(m, w_out, meta, name="w_out", tm=tm_d, tn=TN_WIDE, out_dtype=BF16)
        j = l // 2
        if l % 2 == 0:
            x, hh = _post(x, y, cfg, res=(l, 1, 2), nxt=(l, 2, 4, 3), name="post_mix")
            hid = _ffn_up(hh, ffn_w_gate, ffn_w_up, dense_meta(j, DENSE_ROW_TILES // 2), name="ffn_up",
                          tm=2 * tm_d)
            y = _mm(hid, ffn_w_down, dense_meta(j, 2 * DENSE_ROW_TILES), name="ffn_down", tm=tm_d // 2,
                    out_dtype=BF16)
        else:
            x, hh32, route = _post(x, y, cfg, res=(l, 1, 2), nxt=(l, 2, 4, 3), router=(rw_pad, rb_pad, j),
                                   hh_dtype=F32, name="post_mix_r")
            d0, d1, src, tile_e, nused = _moe_plan(route, n_exp, TM)
            per = DISPATCH_TILE // TM
            assert src.shape[0] % DISPATCH_TILE == 0
            xs = _dispatch(hh32, src, ((nused + per - 1) // per).reshape(1), tm=DISPATCH_TILE,
                           name="moe_dispatch")
            emeta = jnp.concatenate([j * n_exp + tile_e, nused.reshape(1)])
            hid = _ffn_up(xs, moe_wg, moe_wu, emeta, name="moe_up")
            ys = _mm(hid, moe_wd, emeta, name="moe_down")
            y = _combine(ys, d0, d1, route, name="moe_combine")
        if l + 1 < depth:
            x, hh = _post(x, y, cfg, res=(l, 3, 5), nxt=(l + 1, 0, 1, 0), name="post_ffn")
        else:
            (x,) = _post(x, y, cfg, res=(l, 3, 5), nxt=None, name="post_last")

    st = lambda k: jnp.stack(outs[k])
    return (x[:tp].reshape(bp, seq, d), _batch_major(x[tp:], bs),
            st("S_p"), st("dc_p"), st("h_p"), st("lc_p"),
            s_chain, st("dc_s"), st("h_s"), st("lc_s"), st("v_s"))
```

```python
import functools

import jax
import jax.numpy as jnp
from jax import lax
from jax.experimental import pallas as pl
from jax.experimental.pallas import tpu as pltpu

F32, BF16, I32 = jnp.float32, jnp.bfloat16, jnp.int32
HI = lax.Precision.HIGHEST

LANE = 128
SUBLANE = 8
VMEM_LIMIT = 56 << 20
DN_HEAD_DIM = 128
DN_CHUNK = 64
CONV_W = 4
GM_CHUNK = 128
LRU_BLOCKS = 8
LRU_C = 8.0
N_MOD = 6
TOP_K = 2
RMS_EPS = 1e-6
LN_EPS = 1e-5
L2_EPS = 1e-6
NEG_BIG = -1e30

TM = 512
TN = 512
TN_WIDE = 1024
DENSE_ROW_TILES = 8
TE = 512
DMA_ISSUE_UNROLL = 8


def _params(sem):
    return pltpu.CompilerParams(dimension_semantics=sem, vmem_limit_bytes=VMEM_LIMIT)


def _act(name, x):
    if name is None:
        return x
    if name == "silu":
        return x * jax.nn.sigmoid(x)
    if name == "gelu":
        return jax.nn.gelu(x)
    if name == "sigmoid":
        return jax.nn.sigmoid(x)
    raise ValueError(name)


def _rms(x, g):
    return x * lax.rsqrt(jnp.mean(x * x, axis=-1, keepdims=True) + RMS_EPS) * g


def _cast_once(meta_ref, i, pairs):
    prev = meta_ref[jnp.maximum(i - 1, 0)]

    @pl.when((i == 0) | (meta_ref[i] != prev))
    def _():
        for src, dst in pairs:
            dst[...] = src[...].astype(BF16)


def _mm_body(meta_ref, x_ref, w_ref, *rest, ni, act, pre_act, cast_w, has_bias, w_nk):
    rest = list(rest)
    b_ref = rest.pop(0) if has_bias else None
    o_ref = rest.pop(0)
    wbf_ref = rest.pop(0) if cast_w else None
    i = pl.program_id(1)
    nused = meta_ref[ni]

    @pl.when(i < nused)
    def _():
        if cast_w:
            _cast_once(meta_ref, i, [(w_ref, wbf_ref)])
            w = wbf_ref[...]
        else:
            w = w_ref[...]
        w = w[0] if w_nk else w
        x = x_ref[...]
        if pre_act is not None:
            x = _act(pre_act, x.astype(F32))
        dims = (((1,), (1,)), ((), ())) if w_nk else (((1,), (0,)), ((), ()))
        acc = lax.dot_general(x.astype(BF16), w, dims, preferred_element_type=F32)
        if has_bias:
            acc = acc + b_ref[...]
        o_ref[...] = _act(act, acc).astype(o_ref.dtype)

    @pl.when(i >= nused)
    def _():
        o_ref[...] = jnp.zeros_like(o_ref)


def _mm(x, w, meta, *, name, n_off=0, n=None, tm=TM, tn=TN, act=None, pre_act=None,
        bias=None, out_dtype=F32, w_nk=False):
    m_rows, k = x.shape
    n = w.shape[1 if w_nk else 2] if n is None else n
    tm = min(tm, m_rows)
    ni, nj = pl.cdiv(m_rows, tm), n // tn
    assert n % tn == 0 and meta.shape[0] == ni + 1
    cast_w = w.dtype != BF16
    body = functools.partial(_mm_body, ni=ni, act=act, pre_act=pre_act, cast_w=cast_w,
                             has_bias=bias is not None, w_nk=w_nk)
    if w_nk:
        assert n_off % SUBLANE == 0 and tn % SUBLANE == 0
        wspec = pl.BlockSpec((pl.Element(1), pl.Element(tn), pl.Element(k)),
                             lambda j, i, m: (m[i], (n_off // SUBLANE + j * (tn // SUBLANE)) * SUBLANE, 0))
    else:
        wspec = pl.BlockSpec((None, k, tn), lambda j, i, m: (m[i], 0, j + n_off))
    in_specs = [pl.BlockSpec((tm, k), lambda j, i, m: (jnp.minimum(i, m[ni] - 1), 0)), wspec]
    args = [x, w]
    if bias is not None:
        in_specs.append(pl.BlockSpec((None, 1, tn), lambda j, i, m: (m[i], 0, j + n_off)))
        args.append(bias)
    scratch = [pltpu.VMEM((1, tn, k) if w_nk else (k, tn), BF16)] if cast_w else []
    return pl.pallas_call(
        body,
        grid_spec=pltpu.PrefetchScalarGridSpec(
            num_scalar_prefetch=1, grid=(nj, ni), in_specs=in_specs,
            out_specs=pl.BlockSpec((tm, tn), lambda j, i, m: (i, j)), scratch_shapes=scratch),
        out_shape=jax.ShapeDtypeStruct((m_rows, n), out_dtype),
        compiler_params=_params(("arbitrary", "arbitrary")), name=name)(meta, *args)


def _ffn_up_body(meta_ref, x_ref, wg_ref, wu_ref, o_ref, wgb_ref, wub_ref, *, ni):
    i = pl.program_id(1)
    nused = meta_ref[ni]

    @pl.when(i < nused)
    def _():
        _cast_once(meta_ref, i, [(wg_ref, wgb_ref), (wu_ref, wub_ref)])
        x = x_ref[...]
        g = jnp.dot(x, wgb_ref[...], preferred_element_type=F32)
        u = jnp.dot(x, wub_ref[...], preferred_element_type=F32)
        o_ref[...] = (g * jax.nn.sigmoid(g) * u).astype(o_ref.dtype)

    @pl.when(i >= nused)
    def _():
        o_ref[...] = jnp.zeros_like(o_ref)


def _ffn_up(x, wg, wu, meta, *, name, tm=TM, tn=TN):
    m_rows, k = x.shape
    n = wg.shape[2]
    ni, nj = m_rows // tm, n // tn
    assert m_rows % tm == 0 and n % tn == 0 and meta.shape[0] == ni + 1
    wspec = pl.BlockSpec((None, k, tn), lambda j, i, m: (m[i], 0, j))
    return pl.pallas_call(
        functools.partial(_ffn_up_body, ni=ni),
        grid_spec=pltpu.PrefetchScalarGridSpec(
            num_scalar_prefetch=1, grid=(nj, ni),
            in_specs=[pl.BlockSpec((tm, k), lambda j, i, m: (jnp.minimum(i, m[ni] - 1), 0)),
                      wspec, wspec],
            out_specs=pl.BlockSpec((tm, tn), lambda j, i, m: (i, j)),
            scratch_shapes=[pltpu.VMEM((k, tn), BF16), pltpu.VMEM((k, tn), BF16)]),
        out_shape=jax.ShapeDtypeStruct((m_rows, n), BF16),
        compiler_params=_params(("arbitrary", "arbitrary")), name=name)(meta, x, wg, wu)


def _branch_body(meta_ref, xa_ref, xb_ref, xc_ref, ga_ref, gb_ref, gc_ref,
                 w0_ref, w1_ref, w2_ref, o_ref, s0_ref, s1_ref, s2_ref):
    i = pl.program_id(1)
    _cast_once(meta_ref, i, [(w0_ref, s0_ref), (w1_ref, s1_ref), (w2_ref, s2_ref)])
    acc = ga_ref[...] * jnp.dot(xa_ref[...], s0_ref[...], preferred_element_type=F32)
    acc = acc + gb_ref[...] * jnp.dot(xb_ref[...], s1_ref[...], preferred_element_type=F32)
    acc = acc + gc_ref[...] * jnp.dot(xc_ref[...], s2_ref[...], preferred_element_type=F32)
    o_ref[...] = acc.astype(o_ref.dtype)


def _branch_mm(oa, ob, oc, gates, wb, meta, *, name, tm=TM, tn=TN):
    m_rows, k = oa.shape
    n = wb.shape[2]
    ni, nj = m_rows // tm, n // tn
    xspec = pl.BlockSpec((tm, k), lambda j, i, m: (i, 0))

    def gspec(b):
        return pl.BlockSpec((tm, tn), lambda j, i, m: (i, j + b * nj))

    def wspec(b):
        return pl.BlockSpec((None, k, tn), lambda j, i, m: (m[i] * 3 + b, 0, j))

    return pl.pallas_call(
        _branch_body,
        grid_spec=pltpu.PrefetchScalarGridSpec(
            num_scalar_prefetch=1, grid=(nj, ni),
            in_specs=[xspec, xspec, xspec, gspec(0), gspec(1), gspec(2), wspec(0), wspec(1), wspec(2)],
            out_specs=pl.BlockSpec((tm, tn), lambda j, i, m: (i, j)),
            scratch_shapes=[pltpu.VMEM((k, tn), BF16)] * 3),
        out_shape=jax.ShapeDtypeStruct((m_rows, n), BF16),
        compiler_params=_params(("arbitrary", "arbitrary")), name=name)(
            meta, oa, ob, oc, gates, gates, gates, wb, wb, wb)


def _route(hh, rw_ref, rb_ref):
    lg = jnp.dot(hh, rw_ref[...], precision=HI, preferred_element_type=F32) + rb_ref[...]
    lane = lax.broadcasted_iota(I32, lg.shape, 1)
    m1 = jnp.max(lg, axis=1, keepdims=True)
    i1 = jnp.min(jnp.where(lg == m1, lane, LANE), axis=1, keepdims=True)
    lg2 = jnp.where(lane == i1, NEG_BIG * 2, lg)
    m2 = jnp.max(lg2, axis=1, keepdims=True)
    i2 = jnp.min(jnp.where(lg2 == m2, lane, LANE), axis=1, keepdims=True)
    e = jnp.exp(m2 - m1)
    p1 = 1.0 / (1.0 + e)
    p2 = e / (1.0 + e)
    return jnp.where(lane == 0, p1, jnp.where(lane == 1, p2, jnp.where(
        lane == 2, i1.astype(F32), jnp.where(lane == 3, i2.astype(F32), 0.0))))


def _post_body(*refs, npt, reps, res, nxt, router):
    it = iter(refs)
    x_ref = next(it)
    y_ref = next(it) if res else None
    gres_ref = next(it) if res else None
    gnx_ref = next(it) if nxt else None
    mpr_ref = next(it) if res else None
    mpn_ref = next(it) if nxt else None
    msg_ref = next(it) if res else None
    msc_ref = next(it) if nxt else None
    msh_ref = next(it) if nxt else None
    rw_ref = next(it) if router else None
    rb_ref = next(it) if router else None
    xo_ref = next(it) if res else None
    hh_ref = next(it) if nxt else None
    rt_ref = next(it) if router else None
    i = pl.program_id(0)

    def compute(gate, sc, sh):
        x = x_ref[...]
        if res:
            x = x + gate * _rms(y_ref[...].astype(F32), gres_ref[...])
            xo_ref[...] = x
        if nxt:
            hh = _rms(x, gnx_ref[...]) * (1.0 + sc) + sh
            hh_ref[...] = hh.astype(hh_ref.dtype)
            if router:
                rt_ref[...] = _route(hh, rw_ref, rb_ref)

    def rep(r):
        return jnp.concatenate([r[...]] * reps, axis=0) if reps > 1 else r[...]

    @pl.when(i < npt)
    def _():
        compute(mpr_ref[res[2]:res[2] + 1, :] if res else None,
                mpn_ref[nxt[2]:nxt[2] + 1, :] if nxt else None,
                mpn_ref[nxt[3]:nxt[3] + 1, :] if nxt else None)

    @pl.when(i >= npt)
    def _():
        compute(rep(msg_ref) if res else None, rep(msc_ref) if nxt else None,
                rep(msh_ref) if nxt else None)


def _post(x, y, cfg, *, res, nxt, router=None, hh_dtype=BF16, name):
    t_rows, d = x.shape
    te, bp, bs = TE, cfg["bp"], cfg["bs"]
    npt, nt, tpb = cfg["tp"] // te, t_rows // te, cfg["seq"] // te
    assert te % bs == 0 and cfg["seq"] % te == 0 and t_rows % te == 0
    row = pl.BlockSpec((te, d), lambda i: (i, 0))
    in_specs, args = [row], [x]
    if res:
        in_specs.append(row)
        args.append(y)

    def gspec(layer, r):
        return pl.BlockSpec((None, 1, d), lambda i: (layer * 4 + r, 0, 0))

    def mpspec(layer):
        return pl.BlockSpec((None, N_MOD, d), lambda i: (layer * bp + jnp.minimum(i // tpb, bp - 1), 0, 0))

    def msspec(layer, plane):
        return pl.BlockSpec((None, bs, d), lambda i: (layer * N_MOD + plane, 0, 0))

    if res:
        in_specs.append(gspec(res[0], res[1])); args.append(cfg["norm_g"])
    if nxt:
        in_specs.append(gspec(nxt[0], nxt[1])); args.append(cfg["norm_g"])
    if res:
        in_specs.append(mpspec(res[0])); args.append(cfg["mod_p"])
    if nxt:
        in_specs.append(mpspec(nxt[0])); args.append(cfg["mod_p"])
    if res:
        in_specs.append(msspec(res[0], res[2])); args.append(cfg["mod_s"])
    if nxt:
        in_specs.append(msspec(nxt[0], nxt[2])); args.append(cfg["mod_s"])
        in_specs.append(msspec(nxt[0], nxt[3])); args.append(cfg["mod_s"])
    if router:
        rw, rb, slab = router
        in_specs.append(pl.BlockSpec((None, d, LANE), lambda i: (slab, 0, 0))); args.append(rw)
        in_specs.append(pl.BlockSpec((None, 1, LANE), lambda i: (slab, 0, 0))); args.append(rb)
    out_specs, out_shape = [], []
    if res:
        out_specs.append(row); out_shape.append(jax.ShapeDtypeStruct((t_rows, d), F32))
    if nxt:
        out_specs.append(row); out_shape.append(jax.ShapeDtypeStruct((t_rows, d), hh_dtype))
    if router:
        out_specs.append(pl.BlockSpec((te, LANE), lambda i: (i, 0)))
        out_shape.append(jax.ShapeDtypeStruct((t_rows, LANE), F32))
    body = functools.partial(_post_body, npt=npt, reps=te // bs, res=res, nxt=nxt, router=bool(router))
    return pl.pallas_call(body, grid=(nt,), in_specs=in_specs, out_specs=out_specs, out_shape=out_shape,
                          compiler_params=_params(("arbitrary",)), name=name)(*args)


def _row_copy(src_hbm, row, dst_ref, r, sem):
    return pltpu.make_async_copy(src_hbm.at[pl.ds(row, 1), :], dst_ref.at[pl.ds(r, 1), :], sem)


def _gather_start(idx_ref, base, src_hbm, dst_ref, sem, n):
    def start(g, c):
        for u in range(DMA_ISSUE_UNROLL):
            r = g * DMA_ISSUE_UNROLL + u
            _row_copy(src_hbm, idx_ref[base + r], dst_ref, r, sem).start(priority=u % 2)
        return c

    lax.fori_loop(0, n // DMA_ISSUE_UNROLL, start, 0)


def _gather_wait(src_hbm, dst_ref, sem, n):
    def wait(r, c):
        _row_copy(src_hbm, 0, dst_ref, r, sem).wait()
        return c

    lax.fori_loop(0, n, wait, 0, unroll=DMA_ISSUE_UNROLL)


def _dispatch_body(src_ref, nused_ref, x_hbm, o_ref, buf_ref, sem, *, tm):
    i = pl.program_id(0)
    nused = nused_ref[0]

    def start_tile(t):
        slot = t % 2
        _gather_start(src_ref, t * tm, x_hbm, buf_ref.at[slot], sem.at[slot], tm)

    @pl.when((i == 0) & (nused > 0))
    def _():
        start_tile(i)

    @pl.when(i + 1 < nused)
    def _():
        start_tile(i + 1)

    @pl.when(i < nused)
    def _():
        slot = i % 2
        _gather_wait(x_hbm, buf_ref.at[slot], sem.at[slot], tm)
        o_ref[...] = buf_ref[slot].astype(o_ref.dtype)

    @pl.when(i >= nused)
    def _():
        o_ref[...] = jnp.zeros_like(o_ref)


def _dispatch(hh, src, nused, *, tm, name):
    cap = src.shape[0]
    d = hh.shape[1]
    return pl.pallas_call(
        functools.partial(_dispatch_body, tm=tm),
        grid_spec=pltpu.PrefetchScalarGridSpec(
            num_scalar_prefetch=2, grid=(cap // tm,),
            in_specs=[pl.BlockSpec(memory_space=pl.ANY)],
            out_specs=pl.BlockSpec((tm, d), lambda i, s, n: (i, 0)),
            scratch_shapes=[pltpu.VMEM((2, tm, d), hh.dtype), pltpu.SemaphoreType.DMA((2,))]),
        out_shape=jax.ShapeDtypeStruct((cap, d), BF16),
        compiler_params=_params(("arbitrary",)), name=name)(src, nused, hh)


def _combine_body(d0_ref, d1_ref, y_hbm, p_ref, o_ref, a_ref, b_ref, sem_a, sem_b, *, te):
    i = pl.program_id(0)

    def start_tile(t):
        slot = t % 2

        def start(r, c):
            _row_copy(y_hbm, d0_ref[t * te + r], a_ref.at[slot], r, sem_a.at[slot]).start(priority=0)
            _row_copy(y_hbm, d1_ref[t * te + r], b_ref.at[slot], r, sem_b.at[slot]).start(priority=1)
            return c

        lax.fori_loop(0, te, start, 0, unroll=DMA_ISSUE_UNROLL)

    @pl.when(i == 0)
    def _():
        start_tile(i)

    @pl.when(i + 1 < pl.num_programs(0))
    def _():
        start_tile(i + 1)

    slot = i % 2
    _gather_wait(y_hbm, a_ref.at[slot], sem_a.at[slot], te)
    _gather_wait(y_hbm, b_ref.at[slot], sem_b.at[slot], te)
    p = p_ref[...]
    o_ref[...] = (p[:, 0:1] * a_ref[slot] + p[:, 1:2] * b_ref[slot]).astype(o_ref.dtype)


def _combine(ys, dest0, dest1, route, *, name):
    t_rows = route.shape[0]
    d = ys.shape[1]
    te = TE
    return pl.pallas_call(
        functools.partial(_combine_body, te=te),
        grid_spec=pltpu.PrefetchScalarGridSpec(
            num_scalar_prefetch=2, grid=(t_rows // te,),
            in_specs=[pl.BlockSpec(memory_space=pl.ANY),
                      pl.BlockSpec((te, LANE), lambda i, a, b: (i, 0))],
            out_specs=pl.BlockSpec((te, d), lambda i, a, b: (i, 0)),
            scratch_shapes=[pltpu.VMEM((2, te, d), F32), pltpu.VMEM((2, te, d), F32),
                            pltpu.SemaphoreType.DMA((2,)), pltpu.SemaphoreType.DMA((2,))]),
        out_shape=jax.ShapeDtypeStruct((t_rows, d), BF16),
        compiler_params=_params(("arbitrary",)), name=name)(dest0, dest1, ys, route)


def _dn_post(y, j):
    y = y * jax.nn.sigmoid(y)
    outs = []
    for h in range(y.shape[1] // DN_HEAD_DIM):
        yh = y[:, h * DN_HEAD_DIM:(h + 1) * DN_HEAD_DIM]
        n = lax.rsqrt(jnp.sum(yh * yh, axis=-1, keepdims=True) + L2_EPS)
        s = jnp.where(j == 0, n * (DN_HEAD_DIM ** -0.5), jnp.where(j == 1, n, 1.0))
        outs.append(yh * s)
    return jnp.concatenate(outs, axis=1)


def _conv_rows(x_ref, prev_ref, buf_ref, w, first):
    te = x_ref.shape[0]
    x = x_ref[...]
    buf_ref[0:SUBLANE, :] = jnp.where(first, 0.0, prev_ref[...])
    buf_ref[SUBLANE:, :] = x
    y = w[CONV_W - 1:CONV_W, :] * x
    for dlt in range(1, CONV_W):
        y = y + w[CONV_W - 1 - dlt:CONV_W - dlt, :] * buf_ref[SUBLANE - dlt:SUBLANE - dlt + te, :]
    return y


def _conv_slabs(x_ref, hist_ref, w, bs):
    xp = jnp.concatenate([hist_ref[...], x_ref[...]], axis=0)
    steps = x_ref.shape[0] // bs
    ys = []
    for t in range(steps):
        y = w[0:1, :] * xp[t * bs:(t + 1) * bs]
        for j in range(1, CONV_W):
            y = y + w[j:j + 1, :] * xp[(t + j) * bs:(t + j + 1) * bs]
        ys.append(y)
    return ys


def _dn_prep_s_body(x_ref, hist_ref, w_ref, o_ref, *, bs):
    ys = _conv_slabs(x_ref, hist_ref, w_ref[...], bs)
    o_ref[...] = _dn_post(jnp.concatenate(ys, axis=0), pl.program_id(0))


def _dn_prep_sample(qkv, hist_s, conv_w, layer, cfg):
    tp, ts, bs = cfg["tp"], cfg["ts"], cfg["bs"]
    w3 = qkv.shape[1]
    cb = w3 // 3
    return pl.pallas_call(
        functools.partial(_dn_prep_s_body, bs=bs), grid=(3,),
        in_specs=[pl.BlockSpec((ts, cb), lambda j: (tp // ts, j)),
                  pl.BlockSpec(((CONV_W - 1) * bs, cb), lambda j: (0, j)),
                  pl.BlockSpec((None, CONV_W, cb), lambda j: (layer, 0, j))],
        out_specs=pl.BlockSpec((ts, cb), lambda j: (0, j)),
        out_shape=jax.ShapeDtypeStruct((ts, w3), F32),
        compiler_params=_params(("arbitrary",)), name="dn_prep_s")(qkv, hist_s, conv_w)


def _dotf(a, b):
    return jnp.dot(a, b, precision=HI, preferred_element_type=F32)


def _dotb(a, b):
    return jnp.dot(a.astype(BF16), b.astype(BF16), preferred_element_type=F32)


def _dot_nt(a, b):
    return lax.dot_general(a, b, (((1,), (1,)), ((), ())), preferred_element_type=F32)


def _dot_tn(a, b):
    return lax.dot_general(a, b, (((0,), (0,)), ((), ())), preferred_element_type=F32)


def _dn_gates(ba, al, dt, heads, valid):
    lanel = lax.broadcasted_iota(I32, ba.shape, 1)
    beta = jnp.where(valid & (lanel < heads), jax.nn.sigmoid(ba), 0.0)
    g = jnp.where(valid, -jnp.exp(al) * jax.nn.softplus(ba + dt), 0.0)
    return beta, g


def _dn_wy_body(x_ref, prev_ref, cw_ref, ba_ref, al_ref, dt_ref,
                w_ref, u_ref, qg_ref, kd_ref, sc_ref, eg_ref, m_ref, r_ref, p_ref, buf_ref, *, steps_per_seq):
    c = DN_CHUNK
    rows_n = x_ref.shape[0]
    wq = x_ref.shape[1] // 3
    heads = wq // DN_HEAD_DIM
    p2 = 2 * c
    y = _conv_rows(x_ref, prev_ref, buf_ref, cw_ref[...], pl.program_id(0) % steps_per_seq == 0)
    for j in range(3):
        p_ref[j] = _dn_post(y[:, j * wq:(j + 1) * wq], j)
    q_ref, k_ref, v_ref = p_ref.at[0], p_ref.at[1], p_ref.at[2]
    beta_all, g_all = _dn_gates(ba_ref[...], al_ref[...], dt_ref[...], heads, True)
    rr = lax.broadcasted_iota(I32, (rows_n, rows_n), 0)
    cc = lax.broadcasted_iota(I32, (rows_n, rows_n), 1)
    gcum_all = _dotf(((rr >= cc) & (rr // c == cc // c)).astype(F32), g_all)
    row = lax.broadcasted_iota(I32, (p2, p2), 0)
    col = lax.broadcasted_iota(I32, (p2, p2), 1)
    same = (row // c) == (col // c)
    tri = (row >= col) & same
    strict = (row > col) & same

    eye = (row == col).astype(F32)
    blk = (row // SUBLANE) == (col // SUBLANE)

    def stack(ref, rs, h):
        return jnp.concatenate([ref[rs, (h + a) * DN_HEAD_DIM:(h + a + 1) * DN_HEAD_DIM] for a in (0, 1)], axis=0)

    def cols(arr, rs, lane0):
        return jnp.concatenate([arr[rs, lane0 + a:lane0 + a + 1] for a in (0, 1)], axis=0)

    probs = [(ci, h) for ci in range(rows_n // c) for h in range(0, heads, 2)]
    n_p = len(probs)

    for p, (ci, h) in enumerate(probs):
        rs = slice(ci * c, (ci + 1) * c)
        q, k = stack(q_ref, rs, h), stack(k_ref, rs, h)
        beta = cols(beta_all, rs, h)
        gcol = cols(gcum_all, rs, heads + h)
        glast = jnp.concatenate(
            [jnp.broadcast_to(gcum_all[(ci + 1) * c - 1:(ci + 1) * c, heads + h + a:heads + h + a + 1], (c, 1))
             for a in (0, 1)], axis=0)
        gmat = jnp.broadcast_to(gcol, (p2, p2))
        decay = jnp.exp(jnp.where(tri, gmat - gmat.T, NEG_BIG))
        kb = k * beta
        k16 = k.astype(BF16)
        lm = jnp.where(strict, _dot_nt(kb.astype(BF16), k16) * decay, 0.0)
        m_ref[0, p] = lm
        m_ref[1, p] = jnp.where(blk, lm, 0.0)
        sc = _dot_nt(q.astype(BF16), k16) * decay
        egc = jnp.exp(gcol)
        r_ref[0, p] = (kb * egc).astype(BF16)
        r_ref[1, p] = (stack(v_ref, rs, h) * beta).astype(BF16)
        qg = q * egc
        kd = k * jnp.exp(glast - gcol)
        egl = jnp.exp(glast)
        for a in (0, 1):
            hs = slice(a * c, (a + 1) * c)
            qg_ref[ci, h + a] = qg[hs].astype(qg_ref.dtype)
            kd_ref[ci, h + a] = kd[hs].astype(kd_ref.dtype)
            sc_ref[ci, h + a] = sc[hs, hs].astype(sc_ref.dtype)
            eg_ref[ci, h + a] = jnp.broadcast_to(egl[a * c:a * c + 1], (1, LANE))

    def stage(dst, fa, fb):
        for p in range(n_p):
            m_ref[dst, p] = _dotb(fa(p), fb(p))

    ld = lambda p: m_ref[1, p]
    stage(2, ld, ld)
    stage(3, lambda p: m_ref[2, p], lambda p: m_ref[2, p])
    stage(4, lambda p: eye - ld(p), lambda p: eye + m_ref[2, p])
    stage(5, lambda p: m_ref[4, p], lambda p: eye + m_ref[3, p])
    stage(2, lambda p: m_ref[5, p], lambda p: m_ref[0, p] - ld(p))
    stage(3, lambda p: m_ref[2, p], lambda p: m_ref[2, p])
    stage(4, lambda p: m_ref[3, p], lambda p: m_ref[3, p])
    stage(6, lambda p: eye - m_ref[2, p], lambda p: eye + m_ref[3, p])
    stage(7, lambda p: m_ref[6, p], lambda p: eye + m_ref[4, p])
    stage(2, lambda p: m_ref[7, p], lambda p: m_ref[5, p])
    for p, (ci, h) in enumerate(probs):
        t16 = m_ref[2, p].astype(BF16)
        wm = jnp.dot(t16, r_ref[0, p], preferred_element_type=F32)
        um = jnp.dot(t16, r_ref[1, p], preferred_element_type=F32)
        for a in (0, 1):
            hs = slice(a * c, (a + 1) * c)
            w_ref[ci, h + a] = wm[hs].astype(w_ref.dtype)
            u_ref[ci, h + a] = um[hs]


def _dn_seq_body(w_ref, u_ref, qg_ref, kd_ref, sc_ref, eg_ref, z_ref, ng_ref, o_ref, so_ref, s_ref, r_ref):
    nck, heads, c = w_ref.shape[0], w_ref.shape[1], w_ref.shape[2]
    ci = pl.program_id(1)

    @pl.when(ci == 0)
    def _():
        s_ref[...] = jnp.zeros_like(s_ref)

    def chunk(n, carry):
        rs = pl.ds(pl.multiple_of(n * c, c), c)
        for h in range(heads):
            r_ref[h] = jnp.dot(jnp.concatenate([w_ref[n, h], qg_ref[n, h]], axis=0),
                               s_ref[h].astype(BF16), preferred_element_type=F32)
        for h in range(heads):
            hs = slice(h * DN_HEAD_DIM, (h + 1) * DN_HEAD_DIM)
            vn16 = (u_ref[n, h] - r_ref[h, 0:c]).astype(BF16)
            o = r_ref[h, c:2 * c] + jnp.dot(sc_ref[n, h], vn16, preferred_element_type=F32)
            s_ref[h] = s_ref[h] * eg_ref[n, h] + _dot_tn(kd_ref[n, h], vn16)
            o_ref[rs, hs] = (_rms(o, ng_ref[...]) * z_ref[rs, hs]).astype(o_ref.dtype)
        return carry

    lax.fori_loop(0, nck, chunk, 0)

    @pl.when(ci == pl.num_programs(1) - 1)
    def _():
        so_ref[...] = s_ref[...]


DN_WY_CHUNKS = 4
DN_SEQ_CHUNKS = 8


def _dn_prompt(qkv, conv_w, z, ba, al, dt, ng, layer, cfg):
    tp, seq, bp = cfg["tp"], cfg["seq"], cfg["bp"]
    w3 = qkv.shape[1]
    w = w3 // 3
    heads = w // DN_HEAD_DIM
    c, d = DN_CHUNK, DN_HEAD_DIM
    nc = tp // c
    kw, ks = DN_WY_CHUNKS, DN_SEQ_CHUNKS
    assert nc % kw == 0 and (seq // c) % ks == 0 and heads % 2 == 0
    vec = pl.BlockSpec((None, 1, LANE), lambda *_: (layer, 0, 0))
    hm = lambda n, last: pl.BlockSpec((n, heads, c, last), lambda i: (i, 0, 0, 0))
    eg_spec = lambda n: pl.BlockSpec((n, heads, 1, LANE), lambda i: (i, 0, 0, 0))
    sds = jax.ShapeDtypeStruct
    rpb = kw * c // SUBLANE
    wm, um, qg, kd, sc, eg = pl.pallas_call(
        functools.partial(_dn_wy_body, steps_per_seq=seq // (kw * c)), grid=(nc // kw,),
        in_specs=[pl.BlockSpec((kw * c, w3), lambda i: (i, 0)),
                  pl.BlockSpec((SUBLANE, w3), lambda i: (jnp.maximum(i * rpb - 1, 0), 0)),
                  pl.BlockSpec((None, CONV_W, w3), lambda i: (layer, 0, 0)),
                  pl.BlockSpec((kw * c, LANE), lambda i: (i, 0)), vec, vec],
        out_specs=[hm(kw, d), hm(kw, d), hm(kw, d), hm(kw, d), hm(kw, c), eg_spec(kw)],
        out_shape=[sds((nc, heads, c, d), BF16), sds((nc, heads, c, d), F32), sds((nc, heads, c, d), BF16),
                   sds((nc, heads, c, d), BF16), sds((nc, heads, c, c), BF16), sds((nc, heads, 1, LANE), F32)],
        scratch_shapes=[pltpu.VMEM((8, kw * heads // 2, 2 * c, 2 * c), F32),
                        pltpu.VMEM((2, kw * heads // 2, 2 * c, d), BF16),
                        pltpu.VMEM((3, kw * c, w), F32), pltpu.VMEM((kw * c + SUBLANE, w3), F32)],
        compiler_params=_params(("arbitrary",)), name="dn_wy")(qkv, qkv, conv_w, ba, al, dt)
    nps = seq // c // ks
    hm2 = lambda last: pl.BlockSpec((ks, heads, c, last), lambda b, n: (b * nps + n, 0, 0, 0))
    return pl.pallas_call(
        _dn_seq_body, grid=(bp, nps),
        in_specs=[hm2(d), hm2(d), hm2(d), hm2(d), hm2(c),
                  pl.BlockSpec((ks, heads, 1, LANE), lambda b, n: (b * nps + n, 0, 0, 0)),
                  pl.BlockSpec((ks * c, w), lambda b, n: (b * nps + n, 0)),
                  pl.BlockSpec((None, 1, LANE), lambda b, n: (layer, 0, 0))],
        out_specs=[pl.BlockSpec((ks * c, w), lambda b, n: (b * nps + n, 0)),
                   pl.BlockSpec((None, heads, d, d), lambda b, n: (b, 0, 0, 0))],
        out_shape=[sds((tp, w), BF16), sds((bp, heads, d, d), F32)],
        scratch_shapes=[pltpu.VMEM((heads, d, d), F32), pltpu.VMEM((heads, 2 * c, d), F32)],
        compiler_params=_params(("arbitrary", "arbitrary")), name="dn_seq")(wm, um, qg, kd, sc, eg, z, ng)


def _dn_chunk_body(q_ref, k_ref, v_ref, z_ref, ba_ref, s0_ref, al_ref, dt_ref, ng_ref, chain_ref,
                   o_ref, so_ref, m_ref, sc_ref, ws_ref, kd_ref, vn_ref, *, groups, n_valid):
    c = q_ref.shape[0]
    heads = q_ref.shape[1] // DN_HEAD_DIM
    rpg = c // groups
    assert rpg == SUBLANE

    row = lax.broadcasted_iota(I32, (c, c), 0)
    col = lax.broadcasted_iota(I32, (c, c), 1)
    same = (row // rpg) == (col // rpg)
    tri = (row >= col) & same
    strict = (row > col) & same
    rowl = lax.broadcasted_iota(I32, (c, LANE), 0)
    beta_all, g_all = _dn_gates(ba_ref[...], al_ref[...], dt_ref[...], heads, (rowl % rpg) < n_valid)
    gcum_all = _dotf(tri.astype(F32), g_all)
    rg = lax.broadcasted_iota(I32, (c, 1), 0) // rpg
    rg2 = jnp.concatenate([rg, rg], axis=0)

    eye = (row == col).astype(F32)
    glast = lambda h, gi: gcum_all[(gi + 1) * rpg - 1:(gi + 1) * rpg, heads + h:heads + h + 1]

    for h in range(heads):
        hs = slice(h * DN_HEAD_DIM, (h + 1) * DN_HEAD_DIM)
        q, k = q_ref[:, hs], k_ref[:, hs]
        beta = beta_all[:, h:h + 1]
        gcol = gcum_all[:, heads + h:heads + h + 1]
        gmat = jnp.broadcast_to(
            jnp.concatenate([gcol, jnp.zeros((LANE - c, 1), F32)], axis=0), (LANE, LANE))
        decay = jnp.exp(jnp.where(tri, gmat[0:c, 0:c] - gmat.T[0:c, 0:c], NEG_BIG))
        kb = k * beta
        k16 = k.astype(BF16)
        m_ref[0, h] = jnp.where(strict, _dot_nt(kb.astype(BF16), k16) * decay, 0.0)
        sc_ref[h] = (_dot_nt(q.astype(BF16), k16) * decay).astype(BF16)
        egc = jnp.exp(gcol)
        lhs = jnp.concatenate([kb * egc, q * egc], axis=0).astype(BF16)
        ws = jnp.zeros((2 * c, DN_HEAD_DIM), F32)
        glast_rows = jnp.zeros((c, 1), F32)
        for gi in range(groups):
            r = jnp.dot(lhs, s0_ref[gi, h].astype(BF16), preferred_element_type=F32)
            ws = jnp.where(rg2 == gi, r, ws)
            glast_rows = jnp.where(rg == gi, glast(h, gi), glast_rows)
        ws_ref[h] = ws
        kd_ref[h] = (k * jnp.exp(glast_rows - gcol)).astype(BF16)

    def stage(dst, fa, fb):
        for h in range(heads):
            m_ref[dst, h] = _dotb(fa(h), fb(h))

    stage(1, lambda h: m_ref[0, h], lambda h: m_ref[0, h])
    stage(2, lambda h: m_ref[1, h], lambda h: m_ref[1, h])
    stage(3, lambda h: eye - m_ref[0, h], lambda h: eye + m_ref[1, h])
    stage(1, lambda h: m_ref[3, h], lambda h: eye + m_ref[2, h])
    for h in range(heads):
        hs = slice(h * DN_HEAD_DIM, (h + 1) * DN_HEAD_DIM)
        vn_ref[h] = _dotb(m_ref[1, h], v_ref[:, hs] * beta_all[:, h:h + 1] - ws_ref[h, 0:c]).astype(BF16)
    for h in range(heads):
        hs = slice(h * DN_HEAD_DIM, (h + 1) * DN_HEAD_DIM)
        vn16 = vn_ref[h]
        o = ws_ref[h, c:2 * c] + jnp.dot(sc_ref[h], vn16, preferred_element_type=F32)
        kdec = kd_ref[h]
        for gi in range(groups):
            upd = _dot_tn(jnp.where(rg == gi, kdec, jnp.zeros_like(kdec)), vn16)
            so_ref[gi, h] = s0_ref[gi, h] * jnp.exp(glast(h, gi)) + upd
        o_ref[:, hs] = (_rms(o, ng_ref[...]) * z_ref[:, hs]).astype(o_ref.dtype)


def _dn_sample(qkv, z, ba, state, chain, al, dt, ng, layer, *, n_valid):
    rows, w3 = qkv.shape
    w = w3 // 3
    heads = w // DN_HEAD_DIM
    c = DN_CHUNK
    groups = c // SUBLANE
    rspec = lambda col: pl.BlockSpec((c, w), lambda b: (b, col))
    vec = pl.BlockSpec((None, 1, LANE), lambda b: (layer, 0, 0))
    sspec = pl.BlockSpec((None, groups, heads, DN_HEAD_DIM, DN_HEAD_DIM), lambda b: (layer, b, 0, 0, 0))
    return pl.pallas_call(
        functools.partial(_dn_chunk_body, groups=groups, n_valid=n_valid),
        grid=(rows // c,),
        in_specs=[rspec(0), rspec(1), rspec(2), rspec(0), pl.BlockSpec((c, LANE), lambda b: (b, 0)),
                  sspec, vec, vec, vec, pl.BlockSpec(memory_space=pl.ANY)],
        out_specs=[rspec(0), sspec],
        out_shape=[jax.ShapeDtypeStruct((rows, w), BF16), jax.ShapeDtypeStruct(chain.shape, F32)],
        scratch_shapes=[pltpu.VMEM((4, heads, c, c), F32), pltpu.VMEM((heads, c, c), BF16),
                        pltpu.VMEM((heads, 2 * c, DN_HEAD_DIM), F32), pltpu.VMEM((heads, c, DN_HEAD_DIM), BF16),
                        pltpu.VMEM((heads, c, DN_HEAD_DIM), BF16)],
        input_output_aliases={9: 1},
        compiler_params=_params(("arbitrary",)), name="dn_sample")(
            qkv, qkv, qkv, z, ba, state, al, dt, ng, chain)


def _layer_norm(v, g, b):
    mu = jnp.mean(v, axis=-1, keepdims=True)
    vc = v - mu
    return vc * lax.rsqrt(jnp.mean(vc * vc, axis=-1, keepdims=True) + LN_EPS) * g + b


def _gmlp_body(uv_ref, g_ref, b_ref, ws_ref, bst_ref, cw_ref, cb_ref, o_ref, vn_ref, *, npt, bs):
    w = uv_ref.shape[1] // 2
    i = pl.program_id(0)
    vn = _layer_norm(uv_ref[:, w:].astype(F32), g_ref[...], b_ref[...])

    @pl.when(i < npt)
    def _():
        groups = ws_ref.shape[0]
        gw = w // groups
        v16 = vn.astype(BF16)
        row = lax.broadcasted_iota(I32, (GM_CHUNK, GM_CHUNK), 0)
        col = lax.broadcasted_iota(I32, (GM_CHUNK, GM_CHUNK), 1)
        wcs = [jnp.where(row >= col, ws_ref[g], 0.0).astype(BF16) for g in range(groups)]
        bst = bst_ref[...]
        for c in range(uv_ref.shape[0] // GM_CHUNK):
            rs = slice(c * GM_CHUNK, (c + 1) * GM_CHUNK)
            for g in range(groups):
                cs = slice(g * gw, (g + 1) * gw)
                s = jnp.dot(wcs[g], v16[rs, cs], preferred_element_type=F32) + bst[:, g:g + 1]
                o_ref[rs, cs] = (uv_ref[rs, cs] * s).astype(o_ref.dtype)

    @pl.when(i >= npt)
    def _():
        steps = uv_ref.shape[0] // bs
        vn_ref[...] = vn
        for t in range(steps):
            s = cb_ref[t:t + 1, :]
            for u in range(t + 1):
                s = s + cw_ref[t * steps + u:t * steps + u + 1, :] * vn[u * bs:(u + 1) * bs]
            o_ref[t * bs:(t + 1) * bs, :] = (uv_ref[t * bs:(t + 1) * bs, 0:w] * s).astype(o_ref.dtype)


def _gmlp(uv, ln_g, ln_b, ws, bst, cw, cb, layer, cfg):
    tl, bs = cfg["ts"], cfg["bs"]
    t_rows, w2 = uv.shape
    w = w2 // 2
    groups = ws.shape[1]
    steps = tl // bs
    vec = pl.BlockSpec((None, 1, w), lambda i: (layer, 0, 0))
    return pl.pallas_call(
        functools.partial(_gmlp_body, npt=cfg["tp"] // tl, bs=bs), grid=(t_rows // tl,),
        in_specs=[pl.BlockSpec((tl, w2), lambda i: (i, 0)), vec, vec,
                  pl.BlockSpec((None, groups, GM_CHUNK, GM_CHUNK), lambda i: (layer, 0, 0, 0)),
                  pl.BlockSpec((None, GM_CHUNK, LANE), lambda i: (layer, 0, 0)),
                  pl.BlockSpec((None, steps * steps, w), lambda i: (layer, 0, 0)),
                  pl.BlockSpec((None, SUBLANE, w), lambda i: (layer, 0, 0))],
        out_specs=[pl.BlockSpec((tl, w), lambda i: (i, 0)), pl.BlockSpec((tl, w), lambda i: (0, 0))],
        out_shape=[jax.ShapeDtypeStruct((t_rows, w), BF16), jax.ShapeDtypeStruct((tl, w), F32)],
        compiler_params=_params(("arbitrary",)), name="gmlp")(uv, ln_g, ln_b, ws, bst, cw, cb)


def _lru_gates(xc, wa_ref, ba, wx_ref, bx, lam):
    nb = wa_ref.shape[0]
    bw = xc.shape[1] // nb
    x16 = xc.astype(BF16)
    rs, gs = [], []
    for h in range(nb):
        xs = x16[:, h * bw:(h + 1) * bw]
        rs.append(jnp.dot(xs, wa_ref[h].astype(BF16), preferred_element_type=F32))
        gs.append(jnp.dot(xs, wx_ref[h].astype(BF16), preferred_element_type=F32))
    r = jax.nn.sigmoid(jnp.concatenate(rs, axis=1) + ba)
    ig = jax.nn.sigmoid(jnp.concatenate(gs, axis=1) + bx)
    log_a = -LRU_C * r * jax.nn.softplus(-lam)
    th = jnp.tanh(log_a)
    return jnp.exp(log_a), jnp.sqrt(-2.0 * th / (1.0 - th)), ig


def _lru_body(x_ref, prev_ref, y_ref, hist_ref, h0_ref, cw_ref, cb_ref, wa_ref, ba_ref, wx_ref, bx_ref,
              lam_ref, o_ref, hlp_ref, hls_ref, buf_ref, a_ref, b_ref, h_ref, *, npt, tps, bs, pos0):
    te, w = x_ref.shape
    i = pl.program_id(0)

    def gates(xc):
        return _lru_gates(xc, wa_ref, ba_ref[...], wx_ref, bx_ref[...], lam_ref[...])

    @pl.when(i < npt)
    def _():
        first = i % tps == 0
        xc = _conv_rows(x_ref, prev_ref, buf_ref, cw_ref[...], first) + cb_ref[...]
        a, mult, ig = gates(xc)
        reset = first & (lax.broadcasted_iota(I32, (te, w), 0) == 0)
        a_ref[...] = jnp.where(reset, 0.0, a)
        b_ref[...] = jnp.where(reset, 1.0, mult) * ig * xc

        @pl.when(first)
        def _():
            h_ref[...] = jnp.zeros_like(h_ref)

        sub = lax.broadcasted_iota(I32, (SUBLANE, w), 0)

        def slab(s, h):
            rs = pl.ds(pl.multiple_of(s * SUBLANE, SUBLANE), SUBLANE)
            av, bv = a_ref[rs, :], b_ref[rs, :]
            for dlt in (1, 2, 4):
                ash, bsh = pltpu.roll(av, dlt, 0), pltpu.roll(bv, dlt, 0)
                m = sub >= dlt
                bv = jnp.where(m, av * bsh + bv, bv)
                av = jnp.where(m, av * ash, av)
            hs = av * h + bv
            b_ref[rs, :] = hs
            return hs[SUBLANE - 1:SUBLANE, :]

        h_last = lax.fori_loop(0, te // SUBLANE, slab, h_ref[...])
        h_ref[...] = h_last
        hlp_ref[...] = h_last
        o_ref[...] = (y_ref[...] * b_ref[...]).astype(o_ref.dtype)

    @pl.when(i >= npt)
    def _():
        ys = _conv_slabs(x_ref, hist_ref, cw_ref[...], bs)
        h = h0_ref[...]
        for t, y in enumerate(ys):
            xc = y + cb_ref[...]
            a, mult, ig = gates(xc)
            if pos0 + t == 0:
                a, mult = jnp.zeros_like(a), jnp.ones_like(mult)
            h = a * h + mult * ig * xc
            o_ref[t * bs:(t + 1) * bs, :] = (y_ref[t * bs:(t + 1) * bs, :] * h).astype(o_ref.dtype)
        hls_ref[...] = h


def _lru(lx, ly, hist_s, h0_s, p, layer, cfg):
    tl, bs, bp = cfg["ts"], cfg["bs"], cfg["bp"]
    t_rows, w = lx.shape
    nb = p["lru_wa"].shape[1]
    bw = w // nb
    npt, tps = cfg["tp"] // tl, cfg["seq"] // tl
    rpb = tl // SUBLANE
    vec = pl.BlockSpec((None, 1, w), lambda i: (layer, 0, 0))
    blk = pl.BlockSpec((None, nb, bw, bw), lambda i: (layer, 0, 0, 0))
    row = pl.BlockSpec((tl, w), lambda i: (i, 0))
    oc, hl_p, hl_s = pl.pallas_call(
        functools.partial(_lru_body, npt=npt, tps=tps, bs=bs, pos0=cfg["pos0_s"]), grid=(t_rows // tl,),
        in_specs=[row, pl.BlockSpec((SUBLANE, w), lambda i: (jnp.maximum(i * rpb - 1, 0), 0)), row,
                  pl.BlockSpec(((CONV_W - 1) * bs, w), lambda i: (0, 0)),
                  pl.BlockSpec((None, bs, w), lambda i: (layer, 0, 0)),
                  pl.BlockSpec((None, CONV_W, w), lambda i: (layer, 0, 0)), vec, blk, vec, blk, vec, vec],
        out_specs=[row, pl.BlockSpec((None, 1, w), lambda i: (jnp.minimum(i // tps, bp - 1), 0, 0)),
                   pl.BlockSpec((bs, w), lambda i: (0, 0))],
        out_shape=[jax.ShapeDtypeStruct((t_rows, w), BF16), jax.ShapeDtypeStruct((bp, 1, w), F32),
                   jax.ShapeDtypeStruct((bs, w), F32)],
        scratch_shapes=[pltpu.VMEM((tl + SUBLANE, w), F32), pltpu.VMEM((tl, w), F32),
                        pltpu.VMEM((tl, w), F32), pltpu.VMEM((1, w), F32)],
        compiler_params=_params(("arbitrary",)), name="lru")(
            lx, lx, ly, hist_s, h0_s, p["lru_conv_w"], p["lru_conv_b"], p["lru_wa"], p["lru_ba"],
            p["lru_wx"], p["lru_bx"], p["lru_lambda"])
    return oc, hl_p.reshape(bp, w), hl_s


def _time_major(a):
    return jnp.swapaxes(a, 0, 1).reshape(a.shape[0] * a.shape[1], a.shape[2])


def _batch_major(a, bs):
    return jnp.swapaxes(a.reshape(a.shape[0] // bs, bs, a.shape[1]), 0, 1)


def _moe_plan(route, n_exp, tm):
    t_rows = route.shape[0]
    flat_e = route[:, 2:2 + TOP_K].astype(I32).reshape(-1)
    oh = (flat_e[:, None] == jnp.arange(n_exp, dtype=I32)[None, :]).astype(I32)
    csum = jnp.cumsum(oh, axis=0)
    rank = jnp.take_along_axis(csum, flat_e[:, None], axis=1)[:, 0] - 1
    counts = csum[-1]
    tiles_e = (counts + tm - 1) // tm
    tile_end = jnp.cumsum(tiles_e)
    dest = ((tile_end - tiles_e) * tm)[flat_e] + rank
    n_tiles = pl.cdiv(TOP_K * t_rows, tm) + n_exp
    src = jnp.zeros((n_tiles * tm,), I32).at[dest].set(jnp.arange(TOP_K * t_rows, dtype=I32) // TOP_K)
    tile_e = jnp.sum((jnp.arange(n_tiles, dtype=I32)[:, None] >= tile_end[None, :]).astype(I32), axis=1)
    tile_e = jnp.minimum(tile_e, n_exp - 1)
    dest = dest.reshape(t_rows, TOP_K)
    return dest[:, 0], dest[:, 1], src, tile_e, tile_end[-1].astype(I32)


def kernel(x_prompt, x_sample, state_dn_S, state_dn_conv, state_lru_h, state_lru_conv, c_prompt, c_sample,
           w_mod, b_mod, norm_g, w_in, dn_conv_w, dn_a_log, dn_dt_bias, dn_norm_g, gm_ln_g, gm_ln_b, gm_ws,
           gm_bs, lru_conv_w, lru_conv_b, lru_wa, lru_ba, lru_wx, lru_bx, lru_lambda, w_branch, w_out,
           ffn_w_gate, ffn_w_up, ffn_w_down, router_w, router_b, moe_w_gate, moe_w_up, moe_w_down):
    bp, seq, d = x_prompt.shape
    bs, dec = x_sample.shape[0], x_sample.shape[1]
    depth = w_mod.shape[0]
    w = d // 2
    heads = w // DN_HEAD_DIM
    n_exp = moe_w_gate.shape[1]
    tp, ts = bp * seq, bs * dec
    t_rows = tp + ts
    past_len = 16384
    assert dec <= SUBLANE and dec >= CONV_W - 1 and ts % GM_CHUNK == 0 and seq % ts == 0
    assert tp % ts == 0 and ts % TE == 0 and bs % SUBLANE == 0 and t_rows % (16 * 2 * DENSE_ROW_TILES) == 0
    cfg = dict(bp=bp, bs=bs, seq=seq, tp=tp, ts=ts, pos0_s=past_len)
    tm_d = t_rows // DENSE_ROW_TILES

    def dense_meta(slab, ni=DENSE_ROW_TILES):
        return jnp.concatenate([jnp.full((ni,), slab, I32), jnp.array([ni], I32)])

    n_c = bp + bs
    n_c_pad = -(-n_c // SUBLANE) * SUBLANE
    c_all = jnp.concatenate([c_prompt, c_sample, jnp.zeros((n_c_pad - n_c, d), F32)], axis=0)
    b_mod3 = b_mod.reshape(depth, 1, N_MOD * d)
    mods = []
    for l in range(depth):
        meta = jnp.array([l, 1], I32)
        mods.append(_mm(c_all, w_mod, meta, name="mod", tm=n_c_pad, pre_act="silu", bias=b_mod3))
    mod = jnp.stack(mods).reshape(depth, n_c_pad, N_MOD, d)
    cfg["mod_p"] = mod[:, :bp].reshape(depth * bp, N_MOD, d)
    cfg["mod_s"] = jnp.swapaxes(mod[:, bp:n_c], 1, 2).reshape(depth * N_MOD, bs, d)
    cfg["norm_g"] = norm_g.reshape(depth * 4, 1, d)

    n_qkv, n_z = 3 * w, w
    c_ba = n_qkv + n_z
    c_rest = c_ba + 2 * heads
    assert c_rest % SUBLANE == 0
    w_in_t = jnp.swapaxes(w_in, 1, 2)
    lane_pad = lambda a, off: jnp.pad(a, ((0, 0), (off, LANE - off - a.shape[1]))).reshape(depth, 1, LANE)
    al_pad, dt_pad = lane_pad(dn_a_log, heads), lane_pad(dn_dt_bias, heads)
    dn_ng = dn_norm_g.reshape(depth, 1, DN_HEAD_DIM)
    vec3 = lambda a: a.reshape(depth, 1, a.shape[-1])
    lru_p = dict(lru_conv_w=lru_conv_w, lru_conv_b=vec3(lru_conv_b), lru_wa=lru_wa, lru_ba=vec3(lru_ba),
                 lru_wx=lru_wx, lru_bx=vec3(lru_bx), lru_lambda=vec3(lru_lambda))
    gm_g3, gm_b3 = vec3(gm_ln_g), vec3(gm_ln_b)
    gm_bst = jnp.pad(jnp.swapaxes(gm_bs, 1, 2), ((0, 0), (0, 0), (0, LANE - gm_bs.shape[1])))
    gw = w // gm_ws.shape[1]
    gm_cw = jnp.repeat(jnp.swapaxes(gm_ws[:, :, :dec, :dec], 1, 3).reshape(depth, dec, dec, -1), gw, axis=-1)
    gm_cw = jnp.swapaxes(gm_cw, 1, 2).reshape(depth, dec * dec, w)
    gm_cb = jnp.repeat(jnp.swapaxes(gm_bs[:, :, :dec], 1, 2), gw, axis=-1)
    gm_cb = jnp.pad(gm_cb, ((0, 0), (0, SUBLANE - dec), (0, 0)))
    wb3 = w_branch.reshape(depth * 3, w, d)
    n_moe = router_w.shape[0]
    rw_pad = jnp.pad(router_w, ((0, 0), (0, 0), (0, LANE - n_exp)))
    rb_pad = jnp.pad(router_b, ((0, 0), (0, LANE - n_exp)), constant_values=NEG_BIG).reshape(n_moe, 1, LANE)
    d_ff = ffn_w_gate.shape[2]
    moe_wg = moe_w_gate.reshape(n_moe * n_exp, d, d_ff)
    moe_wu = moe_w_up.reshape(n_moe * n_exp, d, d_ff)
    moe_wd = moe_w_down.reshape(n_moe * n_exp, d_ff, d)

    x = jnp.concatenate([x_prompt.reshape(tp, d), _time_major(x_sample)], axis=0)
    s_chain = jnp.zeros(state_dn_S.shape, F32)
    outs = {k: [] for k in ("S_p", "dc_p", "h_p", "lc_p", "dc_s", "h_s", "lc_s", "v_s")}

    def last_rows(a):
        return jnp.stack([a[(b + 1) * seq - (CONV_W - 1):(b + 1) * seq] for b in range(bp)])

    def to_blocks(a):
        a = jnp.pad(_batch_major(a, bs), ((0, 0), (0, SUBLANE - dec), (0, 0)))
        return a.reshape(bs * SUBLANE, a.shape[2])

    (hh,) = _post(x, None, cfg, res=None, nxt=(0, 0, 1, 0), name="pre0")
    for l in range(depth):
        meta = dense_meta(l)
        proj = functools.partial(_mm, hh, w_in_t, meta, w_nk=True, tm=tm_d, tn=TN_WIDE)
        proj16 = functools.partial(proj, out_dtype=BF16)
        qkv = proj(name="in_qkv", n=n_qkv)
        zs = proj16(name="in_z", n_off=n_qkv, n=n_z, act="silu")
        ba = proj(name="in_ba", n_off=c_ba, n=LANE, tn=LANE)
        uv = proj16(name="in_uv", n_off=c_rest, n=2 * w, act="gelu")
        lx = proj(name="in_lx", n_off=c_rest + 2 * w, n=w)
        ly = proj16(name="in_ly", n_off=c_rest + 3 * w, n=w, act="gelu")
        gates = proj16(name="in_merge", n_off=c_rest + 4 * w, n=3 * d, act="sigmoid")

        dn_hist = _time_major(state_dn_conv[l])
        qkv_s = _dn_prep_sample(qkv, dn_hist, dn_conv_w, l, cfg)
        oa_p, s_p = _dn_prompt(qkv, dn_conv_w, zs, ba, al_pad, dt_pad, dn_ng, l, cfg)
        oa_sb, s_chain = _dn_sample(to_blocks(qkv_s), to_blocks(zs[tp:]), to_blocks(ba[tp:]),
                                    state_dn_S, s_chain, al_pad, dt_pad, dn_ng, l, n_valid=dec)
        oa = jnp.concatenate([oa_p, _time_major(oa_sb.reshape(bs, SUBLANE, w)[:, :dec])], axis=0)
        dc_hist = jnp.concatenate([dn_hist, qkv[tp:]], axis=0)

        ob, vn_s = _gmlp(uv, gm_g3, gm_b3, gm_ws, gm_bst, gm_cw, gm_cb, l, cfg)
        lru_hist = _time_major(state_lru_conv[l])
        oc, hl_p, hl_s = _lru(lx, ly, lru_hist, state_lru_h, lru_p, l, cfg)
        lc_hist = jnp.concatenate([lru_hist, lx[tp:]], axis=0)

        outs["S_p"].append(s_p)
        outs["dc_p"].append(last_rows(qkv))
        outs["h_p"].append(hl_p)
        outs["lc_p"].append(last_rows(lx))
        outs["dc_s"].append(_batch_major(dc_hist[dec * bs:], bs))
        outs["h_s"].append(hl_s)
        outs["lc_s"].append(_batch_major(lc_hist[dec * bs:], bs))
        outs["v_s"].append(_batch_major(vn_s, bs))

        m = _branch_mm(oa, ob, oc, gates, wb3, meta, name="branch", tm=tm_d)
        y = _mm(m, w_out, meta, name="w_out", tm=tm_d, tn=TN_WIDE, out_dtype=BF16)
        j = l // 2
        if l % 2 == 0:
            x, hh = _post(x, y, cfg, res=(l, 1, 2), nxt=(l, 2, 4, 3), name="post_mix")
            hid = _ffn_up(hh, ffn_w_gate, ffn_w_up, dense_meta(j, DENSE_ROW_TILES // 2), name="ffn_up",
                          tm=2 * tm_d)
            y = _mm(hid, ffn_w_down, dense_meta(j, 2 * DENSE_ROW_TILES), name="ffn_down", tm=tm_d // 2,
                    out_dtype=BF16)
        else:
            x, hh32, route = _post(x, y, cfg, res=(l, 1, 2), nxt=(l, 2, 4, 3), router=(rw_pad, rb_pad, j),
                                   hh_dtype=F32, name="post_mix_r")
            d0, d1, src, tile_e, nused = _moe_plan(route, n_exp, TM)
            xs = _dispatch(hh32, src, nused.reshape(1), tm=TM, name="moe_dispatch")
            emeta = jnp.concatenate([j * n_exp + tile_e, nused.reshape(1)])
            hid = _ffn_up(xs, moe_wg, moe_wu, emeta, name="moe_up")
            ys = _mm(hid, moe_wd, emeta, name="moe_down")
            y = _combine(ys, d0, d1, route, name="moe_combine")
        if l + 1 < depth:
            x, hh = _post(x, y, cfg, res=(l, 3, 5), nxt=(l + 1, 0, 1, 0), name="post_ffn")
        else:
            (x,) = _post(x, y, cfg, res=(l, 3, 5), nxt=None, name="post_last")

    st = lambda k: jnp.stack(outs[k])
    return (x[:tp].reshape(bp, seq, d), _batch_major(x[tp:], bs),
            st("S_p"), st("dc_p"), st("h_p"), st("lc_p"),
            s_chain, st("dc_s"), st("h_s"), st("lc_s"), st("v_s"))
```
